```python
import math
import jax, jax.numpy as jnp
from jax import lax
import numpy as np

D_MODEL = 1024
BATCH = 32
SEQ = 256
DEPTH = 2
DEC_BATCH = 4
DEC_SEQ = 2048
PAST_LEN = 256

GRID_W = 64
HY_W = 512
HY_ORDER = 2
HY_EMB = 33
HY_FO = 64
HY_SHORT = 3
HY_FAST_PCT = 0.3
HY_SLOW_PCT = 1.5
HY_TARGET = 1e-2
N_RET_HEADS = 4
RET_DK = 128
RET_DV = 256
RET_QK_W = N_RET_HEADS * RET_DK
RET_V_W = N_RET_HEADS * RET_DV
RET_CHUNK = 128
ROPE_BASE = 10000.0
N_EXPERTS = 32
TOP_K = 4
D_FF = 1024
SWIGLU_LIMIT = 7.0
SWIGLU_ALPHA = 1.702
EPS = 1e-6
IN_W = 3 * HY_W + 2 * RET_QK_W + 2 * RET_V_W + 2 * D_MODEL

kernel_name = "hyena_retention_moe_diffusion_step"


def rmsnorm(x, w):
    xf = x.astype(jnp.float32)
    y = xf * lax.rsqrt(jnp.mean(xf * xf, axis=-1, keepdims=True) + EPS)
    return (y * w.astype(jnp.float32)).astype(x.dtype)


def short_conv(u, w, b):
    L = u.shape[1]
    up = jnp.pad(u, ((0, 0), (1, 1), (0, 0)))
    return up[:, :L] * w[0] + up[:, 1:L + 1] * w[1] + up[:, 2:L + 2] * w[2] + b


def hyena_filters(L, p):
    f32 = jnp.float32
    t = jnp.linspace(0.0, 1.0, L, dtype=f32)[:, None]
    bands = (HY_EMB - 1) // 2
    w = 2.0 * math.pi * jnp.arange(L, dtype=f32)[:, None] / L
    f = jnp.linspace(1e-4, bands - 1, bands, dtype=f32)[None, :]
    z = jnp.concatenate([t, jnp.cos(f * w), -jnp.sin(f * w)], axis=-1)
    freq = p["f_freq"].astype(f32)
    h = jnp.sin(freq * (z @ p["f_w1"].astype(f32) + p["f_b1"].astype(f32)))
    h = jnp.sin(freq * (h @ p["f_w2"].astype(f32) + p["f_b2"].astype(f32)))
    h = jnp.sin(freq * (h @ p["f_w3"].astype(f32) + p["f_b3"].astype(f32)))
    h = h @ p["f_wout"].astype(f32)
    max_decay = math.log(HY_TARGET) / HY_FAST_PCT
    min_decay = math.log(HY_TARGET) / HY_SLOW_PCT
    deltas = jnp.linspace(min_decay, max_decay, HY_W, dtype=f32)
    decay = jnp.exp(-t * jnp.abs(deltas))
    h = h.reshape(L, HY_ORDER, 2, HY_W) * decay[:, None, None, :]
    h_fwd, h_bwd = h[:, :, 0], h[:, :, 1]
    k = jnp.concatenate([h_fwd, jnp.zeros((1, HY_ORDER, HY_W), f32), h_bwd[1:][::-1]], axis=0)
    k = k / jnp.sum(jnp.abs(k), axis=0, keepdims=True)
    return jnp.moveaxis(k, 1, 0)


def long_conv(z, taps, skip):
    L = z.shape[1]
    zf = z.astype(jnp.float32)
    Z = jnp.fft.rfft(zf, n=2 * L, axis=1)
    K = jnp.fft.rfft(taps, n=2 * L, axis=0)
    y = jnp.fft.irfft(Z * K[None], n=2 * L, axis=1)[:, :L]
    return (y + zf * skip.astype(jnp.float32)).astype(z.dtype)


def grid_rope(rows):
    f32 = jnp.float32
    row = jnp.repeat(jnp.arange(rows, dtype=f32), GRID_W)
    col = jnp.tile(jnp.arange(GRID_W, dtype=f32), rows)
    half = RET_DK // 4
    inv = ROPE_BASE ** (-jnp.arange(half, dtype=f32) / half)
    ang = jnp.concatenate([row[:, None] * inv, col[:, None] * inv], axis=-1)
    ang = jnp.concatenate([ang, ang], axis=-1)
    return jnp.cos(ang), jnp.sin(ang)


def apply_rope(x, cos, sin):
    x1, x2 = jnp.split(x, 2, axis=-1)
    rot = jnp.concatenate([-x2, x1], axis=-1)
    return x * cos[None, :, None, :] + rot * sin[None, :, None, :]


def retention_scan(q, k, v, log_gamma, s0, strict):
    B, H, L, _ = q.shape
    C = RET_CHUNK
    nC = L // C
    qc = q.reshape(B, H, nC, C, RET_DK)
    kc = k.reshape(B, H, nC, C, RET_DK)
    vc = v.reshape(B, H, nC, C, RET_DV)
    pos = jnp.arange(C, dtype=jnp.float32)
    diff = pos[:, None] - pos[None, :]
    mask = (diff > 0) if strict else (diff >= 0)
    lg = log_gamma[:, None, None]
    dmask = jnp.where(mask, jnp.exp(lg * jnp.where(mask, diff, 0.0)), 0.0)
    scores = jnp.einsum('bhcnd,bhcmd->bhcnm', qc, kc) * dmask[None, :, None]
    inner = jnp.einsum('bhcnm,bhcme->bhcne', scores, vc)
    zeta = jnp.exp(log_gamma[:, None] * (C - 1 - pos))
    kv_chunk = jnp.einsum('bhcmd,bhcme->bhcde', kc * zeta[None, :, None, :, None], vc)
    g_chunk = jnp.exp(log_gamma * C)[None, :, None, None]

    def step(s, kv):
        return g_chunk * s + kv, s

    s_final, s_starts = lax.scan(step, s0, jnp.moveaxis(kv_chunk, 2, 0))
    s_starts = jnp.moveaxis(s_starts, 0, 2)
    xi = jnp.exp(log_gamma[:, None] * (pos + 1.0))
    cross = jnp.einsum('bhcnd,bhcde->bhcne', qc * xi[None, :, None, :, None], s_starts)
    return (inner + cross).reshape(B, H, L, RET_DV), s_final


def retention_mixer(q, k, v, g, log_gamma, s0_f, s0_b, rope):
    B, L, _ = q.shape
    f32 = jnp.float32
    qh = q.reshape(B, L, N_RET_HEADS, RET_DK).astype(f32)
    kh = k.reshape(B, L, N_RET_HEADS, RET_DK).astype(f32)
    vh = v.reshape(B, L, N_RET_HEADS, RET_DV).astype(f32)
    if rope is not None:
        qh = apply_rope(qh, rope[0], rope[1])
        kh = apply_rope(kh, rope[0], rope[1])
    kh = kh * (RET_DK ** -0.5)
    qh, kh, vh = (jnp.transpose(a, (0, 2, 1, 3)) for a in (qh, kh, vh))
    o_f, s_f = retention_scan(qh, kh, vh, log_gamma[0], s0_f.astype(f32), False)
    o_b, s_b = retention_scan(qh[:, :, ::-1], kh[:, :, ::-1], vh[:, :, ::-1], log_gamma[1],
                              s0_b.astype(f32), True)
    o = o_f + o_b[:, :, ::-1]
    o = o * lax.rsqrt(jnp.mean(o * o, axis=-1, keepdims=True) + EPS)
    o = jnp.transpose(o, (0, 2, 1, 3)).reshape(B, L, RET_V_W).astype(g.dtype)
    return jax.nn.silu(g) * o, s_f, s_b


def moe(h, p):
    B, L, D = h.shape
    t = h.reshape(B * L, D)
    logits = (t @ p["router_w"] + p["router_b"]).astype(jnp.float32)
    top_val, top_idx = lax.top_k(logits, TOP_K)
    top_w = jax.nn.softmax(top_val, axis=-1)
    combine = jnp.sum(jax.nn.one_hot(top_idx, N_EXPERTS, dtype=jnp.float32) * top_w[..., None], axis=1)
    out = jnp.zeros((B * L, D), jnp.float32)
    for e in range(N_EXPERTS):
        gate, up = jnp.split(t @ p["w_gate_up"][e] + p["b_gate_up"][e], 2, axis=-1)
        gate = jnp.minimum(gate, SWIGLU_LIMIT)
        up = jnp.clip(up, -SWIGLU_LIMIT, SWIGLU_LIMIT)
        act = gate * jax.nn.sigmoid(SWIGLU_ALPHA * gate) * (up + 1.0)
        y = act @ p["w_down"][e] + p["b_down"][e]
        out = out + combine[:, e:e + 1] * y.astype(jnp.float32)
    return out.astype(h.dtype).reshape(B, L, D)


def trunk_layer(x, mod, rope, s0_f, s0_b, p):
    shift1, scale1, gate1, shift2, scale2, gate2 = jnp.split(mod[:, None, :], 6, axis=-1)
    L = x.shape[1]
    u = rmsnorm(x, p["norm_mix_w"]) * (1.0 + scale1) + shift1
    proj = u @ p["w_in"]
    i0 = 3 * HY_W
    i1 = i0 + RET_QK_W
    i2 = i1 + RET_QK_W
    i3 = i2 + RET_V_W
    i4 = i3 + RET_V_W
    hy, q, k, v, g, gates = jnp.split(proj, [i0, i1, i2, i3, i4], axis=-1)
    hy = short_conv(hy, p["conv_w"], p["conv_b"])
    hv, hx1, hx2 = jnp.split(hy, 3, axis=-1)
    filters = hyena_filters(L, p)
    za = hx1 * long_conv(hv, filters[0], p["hy_skip"][0])
    za = hx2 * long_conv(za, filters[1], p["hy_skip"][1])
    log_gamma = jax.nn.log_sigmoid(p["ret_decay_logit"].astype(jnp.float32))
    zb, s_f, s_b = retention_mixer(q, k, v, g, log_gamma, s0_f, s0_b, rope)
    ga, gb = jnp.split(gates, 2, axis=-1)
    merged = jax.nn.sigmoid(ga) * (za @ p["w_branch_a"]) + jax.nn.sigmoid(gb) * (zb @ p["w_branch_b"])
    x = x + gate1 * (merged @ p["w_out"])
    hmo = rmsnorm(x, p["norm_ffn_w"]) * (1.0 + scale2) + shift2
    x = x + gate2 * moe(hmo, p)
    return x, s_f, s_b


def setup_inputs(seed: int = 0) -> dict:
    key = jax.random.key(seed)
    kit = iter(jax.random.split(key, 40))
    f32 = jnp.float32

    def nrm(shape, s):
        return jax.random.normal(next(kit), shape, f32) * s

    expo = np.linspace(5.0, 12.0, N_RET_HEADS)
    logit0 = jnp.asarray(np.log(2.0 ** expo - 1.0), f32)
    return {
        "x_prompt": nrm((BATCH, SEQ, D_MODEL), 1.0),
        "x_sample": nrm((DEC_BATCH, DEC_SEQ, D_MODEL), 1.0),
        "state_ret": nrm((DEC_BATCH, DEPTH, 2, N_RET_HEADS, RET_DK, RET_DV), 0.5),
        "c": nrm((DEC_BATCH, D_MODEL), 1.0),
        "c_ctx": nrm((D_MODEL,), 1.0),
        "ada_w": nrm((DEPTH, D_MODEL, 6 * D_MODEL), 0.5 * D_MODEL ** -0.5),
        "ada_b": nrm((DEPTH, 6 * D_MODEL), 0.02),
        "norm_mix_w": 1.0 + nrm((DEPTH, D_MODEL), 0.01),
        "w_in": nrm((DEPTH, D_MODEL, IN_W), D_MODEL ** -0.5),
        "conv_w": nrm((DEPTH, HY_SHORT, 3 * HY_W), 0.5),
        "conv_b": nrm((DEPTH, 3 * HY_W), 0.02),
        "f_w1": nrm((DEPTH, HY_EMB, HY_FO), HY_EMB ** -0.5),
        "f_b1": nrm((DEPTH, HY_FO), 0.1),
        "f_w2": nrm((DEPTH, HY_FO, HY_FO), HY_FO ** -0.5),
        "f_b2": nrm((DEPTH, HY_FO), 0.1),
        "f_w3": nrm((DEPTH, HY_FO, HY_FO), HY_FO ** -0.5),
        "f_b3": nrm((DEPTH, HY_FO), 0.1),
        "f_freq": 1.0 + nrm((DEPTH, HY_FO), 0.01),
        "f_wout": nrm((DEPTH, HY_FO, HY_ORDER * 2 * HY_W), HY_FO ** -0.5),
        "hy_skip": nrm((DEPTH, HY_ORDER, HY_W), 0.5),
        "ret_decay_logit": logit0[None, None, :] + nrm((DEPTH, 2, N_RET_HEADS), 0.05),
        "w_branch_a": nrm((DEPTH, HY_W, D_MODEL), HY_W ** -0.5),
        "w_branch_b": nrm((DEPTH, RET_V_W, D_MODEL), RET_V_W ** -0.5),
        "w_out": nrm((DEPTH, D_MODEL, D_MODEL), D_MODEL ** -0.5),
        "norm_ffn_w": 1.0 + nrm((DEPTH, D_MODEL), 0.01),
        "router_w": nrm((DEPTH, D_MODEL, N_EXPERTS), D_MODEL ** -0.5),
        "router_b": nrm((DEPTH, N_EXPERTS), 0.01),
        "w_gate_up": nrm((DEPTH, N_EXPERTS, D_MODEL, 2 * D_FF), D_MODEL ** -0.5),
        "b_gate_up": nrm((DEPTH, N_EXPERTS, 2 * D_FF), 0.01),
        "w_down": nrm((DEPTH, N_EXPERTS, D_FF, D_MODEL), D_FF ** -0.5),
        "b_down": nrm((DEPTH, N_EXPERTS, D_MODEL), 0.01),
        "final_norm_w": 1.0 + nrm((D_MODEL,), 0.01),
    }


def reference(x_prompt, x_sample, state_ret, c, c_ctx, ada_w, ada_b, norm_mix_w, w_in, conv_w, conv_b,
              f_w1, f_b1, f_w2, f_b2, f_w3, f_b3, f_freq, f_wout, hy_skip, ret_decay_logit,
              w_branch_a, w_branch_b, w_out, norm_ffn_w, router_w, router_b, w_gate_up, b_gate_up,
              w_down, b_down, final_norm_w):
    rows = x_sample.shape[1] // GRID_W
    rope = grid_rope(rows)
    zero_state = jnp.zeros((x_prompt.shape[0], N_RET_HEADS, RET_DK, RET_DV), jnp.float32)
    xp, xs = x_prompt, x_sample
    new_states = []
    for l in range(DEPTH):
        p = {
            "norm_mix_w": norm_mix_w[l], "w_in": w_in[l], "conv_w": conv_w[l], "conv_b": conv_b[l],
            "f_w1": f_w1[l], "f_b1": f_b1[l], "f_w2": f_w2[l], "f_b2": f_b2[l],
            "f_w3": f_w3[l], "f_b3": f_b3[l], "f_freq": f_freq[l], "f_wout": f_wout[l],
            "hy_skip": hy_skip[l], "ret_decay_logit": ret_decay_logit[l],
            "w_branch_a": w_branch_a[l], "w_branch_b": w_branch_b[l], "w_out": w_out[l],
            "norm_ffn_w": norm_ffn_w[l], "router_w": router_w[l], "router_b": router_b[l],
            "w_gate_up": w_gate_up[l], "b_gate_up": b_gate_up[l], "w_down": w_down[l], "b_down": b_down[l],
        }
        mod_ctx = jax.nn.silu(c_ctx)[None, :] @ ada_w[l] + ada_b[l]
        mod_lat = jax.nn.silu(c) @ ada_w[l] + ada_b[l]
        xp, s_f, s_b = trunk_layer(xp, mod_ctx, None, zero_state, zero_state, p)
        new_states.append(jnp.stack([s_f, s_b], axis=1))
        xs, _, _ = trunk_layer(xs, mod_lat, rope, state_ret[:, l, 0], state_ret[:, l, 1], p)
    state_ret_new = jnp.stack(new_states, axis=1).astype(x_prompt.dtype)
    y_prompt = rmsnorm(xp, final_norm_w)
    y_sample = rmsnorm(xs, final_norm_w)
    return (y_prompt, y_sample, state_ret_new)
```

```python
import functools
import math

import jax
import jax.numpy as jnp
from jax import lax
from jax.experimental import pallas as pl
from jax.experimental.pallas import tpu as pltpu

F32 = jnp.float32
BF16 = jnp.bfloat16

GRID_W = 64
HY_W = 512
HY_ORDER = 2
HY_EMB = 33
HY_FAST_PCT = 0.3
HY_SLOW_PCT = 1.5
HY_TARGET = 1e-2
N_RET_HEADS = 4
RET_DK = 128
RET_DV = 256
RET_CHUNK = 128
ROPE_BASE = 10000.0
N_EXPERTS = 32
TOP_K = 4
SWIGLU_LIMIT = 7.0
SWIGLU_ALPHA = 1.702
EPS = 1e-6

LANES = 128
VMEM_LIMIT = 56 * 1024 * 1024

COL_V, COL_G, COL_GA, COL_GB = 0, 1024, 2048, 3072
COL_HV, COL_HX1, COL_HX2, COL_Q, COL_K = 4096, 4608, 5120, 5632, 6144
IN_W = 6656


def _dot(a, b):
    return jnp.dot(a, b, preferred_element_type=F32)


def _split_bf16(a):
    hi = a.astype(BF16)
    lo = (a - hi.astype(F32)).astype(BF16)
    return hi, lo


def _dot3(a, b):
    ah, al = _split_bf16(a)
    bh, bl = _split_bf16(b)
    return _dot(ah, bh) + (_dot(ah, bl) + _dot(al, bh))


def _silu(x):
    return x * jax.nn.sigmoid(x)


def _params(sem, vmem=VMEM_LIMIT):
    return pltpu.CompilerParams(dimension_semantics=sem, vmem_limit_bytes=vmem)


def _resident(shape, index_map):
    return pl.BlockSpec(shape, index_map, pipeline_mode=pl.Buffered(1))


def _mod_kernel(c_ref, w_ref, b_ref, o_ref):
    o_ref[0] = _dot3(_silu(c_ref[...]), w_ref[0]) + b_ref[0]


def _modulation(cvec, ada_w, ada_b):
    depth, d, six_d = ada_w.shape
    tn = six_d // 4
    return pl.pallas_call(
        _mod_kernel,
        grid=(depth, six_d // tn),
        in_specs=[
            pl.BlockSpec((8, d), lambda l, j: (0, 0)),
            pl.BlockSpec((1, d, tn), lambda l, j: (l, 0, j)),
            pl.BlockSpec((1, 1, tn), lambda l, j: (l, 0, j)),
        ],
        out_specs=pl.BlockSpec((1, 8, tn), lambda l, j: (l, 0, j)),
        out_shape=jax.ShapeDtypeStruct((depth, 8, six_d), F32),
        compiler_params=_params(("parallel", "parallel")),
        name="adaln_mod",
    )(cvec, ada_w, ada_b.reshape(depth, 1, six_d))


def _in_kernel(x_ref, nw_ref, mod_ref, w_ref, o_ref, *, d, cw):
    x = x_ref[...]
    y = x * lax.rsqrt(jnp.mean(x * x, axis=-1, keepdims=True) + EPS) * nw_ref[...]
    shift = mod_ref[0, :, 0:d]
    scale = mod_ref[0, :, d:2 * d]
    u = (y * (1.0 + scale) + shift).astype(BF16)
    for j in range(IN_W // cw):
        o_ref[:, j * cw:(j + 1) * cw] = _dot(u, w_ref[:, j * cw:(j + 1) * cw]).astype(BF16)


def _in_proj(x, norm_w, mod_rows, w_in_bf, tm):
    t, d = x.shape
    return pl.pallas_call(
        functools.partial(_in_kernel, d=d, cw=512),
        grid=(t // tm,),
        in_specs=[
            pl.BlockSpec((tm, d), lambda i: (i, 0)),
            pl.BlockSpec((1, d), lambda i: (0, 0)),
            pl.BlockSpec((1, 1, 2 * d), lambda i: (i, 0, 0)),
            _resident((d, IN_W), lambda i: (0, 0)),
        ],
        out_specs=pl.BlockSpec((tm, IN_W), lambda i: (i, 0)),
        out_shape=jax.ShapeDtypeStruct((t, IN_W), BF16),
        compiler_params=_params(("parallel",)),
        name="in_proj",
    )(x, norm_w.reshape(1, d), mod_rows, w_in_bf)


def _filt_kernel(z_ref, t_ref, w1, b1, w2, b2, w3, b3, fr, wo, dl_ref, h_ref, s_ref, *, tl):
    i = pl.program_id(0)
    freq = fr[...]
    h = jnp.sin(freq * (_dot3(z_ref[...], w1[...]) + b1[...]))
    h = jnp.sin(freq * (_dot3(h, w2[...]) + b2[...]))
    h = jnp.sin(freq * (_dot3(h, w3[...]) + b3[...]))
    h = _dot3(h, wo[...])
    decay = jnp.exp(-t_ref[...] * jnp.abs(dl_ref[...]))
    h = h * jnp.concatenate([decay] * (2 * HY_ORDER), axis=1)
    h_ref[...] = h
    row = i * tl + lax.broadcasted_iota(jnp.int32, (tl, 1), 0)
    col = lax.broadcasted_iota(jnp.int32, (1, 2 * HY_ORDER * HY_W), 1)
    is_bwd = ((col // HY_W) % 2) == 1
    part = jnp.sum(jnp.where((row == 0) & is_bwd, 0.0, jnp.abs(h)), axis=0, keepdims=True)

    @pl.when(i == 0)
    def _():
        s_ref[...] = jnp.zeros_like(s_ref)

    s_ref[...] += part


def _hyena_taps(l, zfeat, tcol, deltas, w1, b1, w2, b2, w3, b3, freq, wout):
    tl = min(l, 256)
    nw = 2 * HY_ORDER * HY_W
    full = lambda shape: pl.BlockSpec(shape, lambda i: (0,) * len(shape))
    return pl.pallas_call(
        functools.partial(_filt_kernel, tl=tl),
        grid=(l // tl,),
        in_specs=[
            pl.BlockSpec((tl, LANES), lambda i: (i, 0)),
            pl.BlockSpec((tl, 1), lambda i: (i, 0)),
            full((LANES, LANES)), full((1, LANES)),
            full((LANES, LANES)), full((1, LANES)),
            full((LANES, LANES)), full((1, LANES)),
            full((1, LANES)),
            full((LANES, nw)),
            full((1, HY_W)),
        ],
        out_specs=[pl.BlockSpec((tl, nw), lambda i: (i, 0)), pl.BlockSpec((1, nw), lambda i: (0, 0))],
        out_shape=[jax.ShapeDtypeStruct((l, nw), F32), jax.ShapeDtypeStruct((1, nw), F32)],
        compiler_params=_params(("arbitrary",)),
        name=f"hyena_taps_{l}",
    )(zfeat, tcol, w1, b1, w2, b2, w3, b3, freq, wout, deltas)


def _fdft_kernel(c_ref, s_ref, hf_ref, hb_ref, sf_ref, sb_ref, kc_ref, ki_ref, kn_ref, *, l):
    row = lax.broadcasted_iota(jnp.int32, (l, 1), 0)
    hf = hf_ref[...]
    hb = jnp.where(row == 0, 0.0, hb_ref[...])
    inv = 1.0 / (sf_ref[...] + sb_ref[...])
    a = hf + hb
    b = hb - hf
    wf = jnp.where(row == 0, 0.5 / l, 1.0 / l)
    kc_ref[...] = _dot(c_ref[...], a.astype(BF16)) * wf * inv
    ki_ref[...] = _dot(s_ref[...], b.astype(BF16)) * wf * inv
    sgn = jnp.where(row % 2 == 0, 1.0, -1.0)
    kn_ref[...] = jnp.sum(a * sgn, axis=0, keepdims=True) * inv * (0.5 / l)


def _filter_spectra(l, cmat, smat, taps, sums):
    wt = 256
    nb = HY_W // wt
    two_w = 2 * HY_W
    return pl.pallas_call(
        functools.partial(_fdft_kernel, l=l),
        grid=(HY_ORDER, nb),
        in_specs=[
            _resident((l, l), lambda o, j: (0, 0)),
            _resident((l, l), lambda o, j: (0, 0)),
            pl.BlockSpec((l, wt), lambda o, j: (0, o * 2 * nb + j)),
            pl.BlockSpec((l, wt), lambda o, j: (0, o * 2 * nb + nb + j)),
            pl.BlockSpec((1, wt), lambda o, j: (0, o * 2 * nb + j)),
            pl.BlockSpec((1, wt), lambda o, j: (0, o * 2 * nb + nb + j)),
        ],
        out_specs=[
            pl.BlockSpec((l, wt), lambda o, j: (0, o * nb + j)),
            pl.BlockSpec((l, wt), lambda o, j: (0, o * nb + j)),
            pl.BlockSpec((1, wt), lambda o, j: (0, o * nb + j)),
        ],
        out_shape=[
            jax.ShapeDtypeStruct((l, HY_ORDER * HY_W), F32),
            jax.ShapeDtypeStruct((l, HY_ORDER * HY_W), F32),
            jax.ShapeDtypeStruct((1, HY_ORDER * HY_W), F32),
        ],
        compiler_params=_params(("parallel", "parallel")),
        name=f"hyena_spectra_{l}",
    )(cmat, smat, taps, taps, sums, sums)


def _hy_kernel(c_ref, s_ref, hv_ref, hx1_ref, hx2_ref, cw_ref, cb_ref,
               kc0, ki0, kc1, ki1, kn_ref, sk_ref, o_ref, z_scr, zb_scr, p_scr, q_scr, *, l, rc):
    row = lax.broadcasted_iota(jnp.int32, (l, 1), 0)
    first = row == 0
    last = row == l - 1
    sgn = jnp.where(row % 2 == 0, 1.0, -1.0)
    cw = cw_ref[...]
    cb = cb_ref[...]

    def short_conv(ref, part):
        x = ref[...].astype(F32)
        xm = jnp.where(first, 0.0, pltpu.roll(x, 1, 0))
        xp = jnp.where(last, 0.0, pltpu.roll(x, l - 1, 0))
        return (xm * cw[0, part:part + 1] + x * cw[1, part:part + 1] + xp * cw[2, part:part + 1]
                + cb[part:part + 1])

    z_scr[...] = short_conv(hv_ref, 0)
    for o, (gate_ref, kc, ki) in enumerate(((hx1_ref, kc0, ki0), (hx2_ref, kc1, ki1))):
        z = z_scr[...]
        zb_scr[...] = z.astype(BF16)
        nyq = jnp.sum(z * sgn, axis=0, keepdims=True) * kn_ref[o:o + 1]
        for r in range(l // rc):
            sl = slice(r * rc, (r + 1) * rc)
            zc = _dot(c_ref[sl, :], zb_scr[...])
            zs = _dot(s_ref[sl, :], zb_scr[...])
            p_scr[sl, :] = (zc * kc[sl, :] + zs * ki[sl, :]).astype(BF16)
            q_scr[sl, :] = (zs * kc[sl, :] - zc * ki[sl, :]).astype(BF16)
        gate = short_conv(gate_ref, o + 1)
        for r in range(l // rc):
            sl = slice(r * rc, (r + 1) * rc)
            y = _dot(c_ref[sl, :], p_scr[...]) + _dot(s_ref[sl, :], q_scr[...])
            y = y + sgn[sl] * nyq + z_scr[sl, :] * sk_ref[o:o + 1]
            z_scr[sl, :] = gate[sl] * y
    o_ref[...] = z_scr[...].astype(BF16)


def _hyena_branch(proj, row_blk0, nb, l, cmat, smat, conv_w, conv_b, kc, ki, kn, skip):
    wt = 256
    nw = HY_W // wt
    rc = min(l, 512)
    hv0, hx10, hx20 = COL_HV // wt, COL_HX1 // wt, COL_HX2 // wt
    seq = lambda c0: pl.BlockSpec((l, wt), lambda j, b: (row_blk0 + b, c0 + j))
    tab = lambda o: pl.BlockSpec((l, wt), lambda j, b: (0, o * nw + j))
    return pl.pallas_call(
        functools.partial(_hy_kernel, l=l, rc=rc),
        grid=(nw, nb),
        in_specs=[
            _resident((l, l), lambda j, b: (0, 0)),
            _resident((l, l), lambda j, b: (0, 0)),
            seq(hv0), seq(hx10), seq(hx20),
            pl.BlockSpec((3, 3, wt), lambda j, b: (0, 0, j)),
            pl.BlockSpec((3, wt), lambda j, b: (0, j)),
            tab(0), tab(0), tab(1), tab(1),
            pl.BlockSpec((HY_ORDER, wt), lambda j, b: (0, j)),
            pl.BlockSpec((HY_ORDER, wt), lambda j, b: (0, j)),
        ],
        out_specs=pl.BlockSpec((l, wt), lambda j, b: (b, j)),
        out_shape=jax.ShapeDtypeStruct((nb * l, HY_W), BF16),
        scratch_shapes=[
            pltpu.VMEM((l, wt), F32), pltpu.VMEM((l, wt), BF16),
            pltpu.VMEM((l, wt), BF16), pltpu.VMEM((l, wt), BF16),
        ],
        compiler_params=_params(("parallel", "parallel")),
        name=f"hyena_{l}",
    )(cmat, smat, proj, proj, proj, conv_w.reshape(3, 3, HY_W), conv_b.reshape(3, HY_W),
      kc, ki, kc, ki, kn.reshape(HY_ORDER, HY_W), skip)


def _log_sigmoid(x):
    return jnp.minimum(x, 0.0) - jnp.log1p(jnp.exp(-jnp.abs(x)))


def _ret_kernel(*refs, l, use_rope, has_s0):
    it = iter(refs)
    dl_ref, q_ref, k_ref, v_ref, g_ref = (next(it) for _ in range(5))
    cos_ref, sin_ref = (next(it), next(it)) if use_rope else (None, None)
    s0_ref = next(it) if has_s0 else None
    o_ref, st_ref, kr_scr, r_scr, s_scr = (next(it) for _ in range(5))

    c = RET_CHUNK
    nc = l // c
    lgf = _log_sigmoid(dl_ref[0, 0])
    lgb = _log_sigmoid(dl_ref[1, 0])
    pos = lax.broadcasted_iota(jnp.int32, (c, 1), 0).astype(F32)
    diff = pos - lax.broadcasted_iota(jnp.int32, (1, c), 1).astype(F32)
    dm = jnp.where(diff >= 0, jnp.exp(lgf * diff), jnp.exp(-lgb * diff))
    zeta_f = jnp.exp(lgf * (c - 1.0 - pos))
    zeta_b = jnp.exp(lgb * pos)
    xi_f = jnp.exp(lgf * (pos + 1.0))
    xi_b = jnp.exp(lgb * (c - pos))
    gc_f = jnp.exp(lgf * c)
    gc_b = jnp.exp(lgb * c)

    def rope(x, r0):
        if not use_rope:
            return x
        return x * cos_ref[pl.ds(r0, c), :] + pltpu.roll(x, RET_DK // 2, 1) * sin_ref[pl.ds(r0, c), :]

    def kt_v(k, v):
        return lax.dot_general(k.astype(BF16), v, (((0,), (0,)), ((), ())), preferred_element_type=F32)

    s_scr[...] = s0_ref[0, 0, 1, 0] if has_s0 else jnp.zeros_like(s_scr)

    def bwd_step(i, carry):
        ci = nc - 1 - i
        r0 = pl.multiple_of(ci * c, c)
        r_scr[ci] = s_scr[...]
        k = rope(k_ref[pl.ds(r0, c), :].astype(F32), r0) * (RET_DK ** -0.5)
        kr_scr[pl.ds(r0, c), :] = k.astype(BF16)
        s_scr[...] = gc_b * s_scr[...] + kt_v(k * zeta_b, v_ref[pl.ds(r0, c), :])
        return carry

    lax.fori_loop(0, nc, bwd_step, 0)
    st_ref[0, 1, 0] = s_scr[...]

    s_scr[...] = s0_ref[0, 0, 0, 0] if has_s0 else jnp.zeros_like(s_scr)

    def fwd_step(ci, carry):
        r0 = pl.multiple_of(ci * c, c)
        q = rope(q_ref[pl.ds(r0, c), :].astype(F32), r0)
        kb = kr_scr[pl.ds(r0, c), :]
        v = v_ref[pl.ds(r0, c), :]
        sc = lax.dot_general(q.astype(BF16), kb, (((1,), (1,)), ((), ())), preferred_element_type=F32) * dm
        qq = jnp.concatenate([q * xi_f, q * xi_b], axis=1).astype(BF16)
        st = jnp.concatenate([s_scr[...], r_scr[ci]], axis=0).astype(BF16)
        o = _dot(sc.astype(BF16), v) + _dot(qq, st)
        s_scr[...] = gc_f * s_scr[...] + kt_v(kb.astype(F32) * zeta_f, v)
        o = o * lax.rsqrt(jnp.mean(o * o, axis=-1, keepdims=True) + EPS)
        o_ref[pl.ds(r0, c), :] = (_silu(g_ref[pl.ds(r0, c), :].astype(F32)) * o).astype(BF16)
        return carry

    lax.fori_loop(0, nc, fwd_step, 0)
    st_ref[0, 0, 0] = s_scr[...]


def _retention_branch(proj, row_blk0, nb, l, decay_logit, rope_tabs, state, layer):
    h = N_RET_HEADS
    use_rope = rope_tabs is not None
    has_s0 = state is not None
    q0, k0, v0, g0 = COL_Q // RET_DK, COL_K // RET_DK, COL_V // RET_DV, COL_G // RET_DV
    seq = lambda c0, w: pl.BlockSpec((l, w), lambda b, hh: (row_blk0 + b, c0 + hh))
    in_specs = [
        pl.BlockSpec((2, 1, 1, 1), lambda b, hh: (0, hh, 0, 0)),
        seq(q0, RET_DK), seq(k0, RET_DK), seq(v0, RET_DV), seq(g0, RET_DV),
    ]
    args = [decay_logit.reshape(2, h, 1, 1), proj, proj, proj, proj]
    if use_rope:
        in_specs += [pl.BlockSpec((l, RET_DK), lambda b, hh: (0, 0))] * 2
        args += list(rope_tabs)
    if has_s0:
        in_specs.append(pl.BlockSpec((1, 1, 2, 1, RET_DK, RET_DV), lambda b, hh: (b, layer, 0, hh, 0, 0)))
        args.append(state)
    return pl.pallas_call(
        functools.partial(_ret_kernel, l=l, use_rope=use_rope, has_s0=has_s0),
        grid=(nb, h),
        in_specs=in_specs,
        out_specs=[
            pl.BlockSpec((l, RET_DV), lambda b, hh: (b, hh)),
            pl.BlockSpec((1, 2, 1, RET_DK, RET_DV), lambda b, hh: (b, 0, hh, 0, 0)),
        ],
        out_shape=[
            jax.ShapeDtypeStruct((nb * l, h * RET_DV), BF16),
            jax.ShapeDtypeStruct((nb, 2, h, RET_DK, RET_DV), F32),
        ],
        scratch_shapes=[
            pltpu.VMEM((l, RET_DK), BF16),
            pltpu.VMEM((l // RET_CHUNK, RET_DK, RET_DV), F32),
            pltpu.VMEM((RET_DK, RET_DV), F32),
        ],
        compiler_params=_params(("parallel", "parallel")),
        name=f"retention_{l}",
    )(*args)


ROW_ALIGN = 16


def _sorted_rows(tm):
    worst = TOP_K * tm + N_EXPERTS * (ROW_ALIGN - 1) + ROW_ALIGN
    return -(-worst // LANES) * LANES


def _mix_kernel(zap_ref, zas_ref, zbp_ref, zbs_ref, ga_ref, gb_ref, x_ref, mod_ref, wa_ref, wb_ref,
                wo_ref, nw_ref, rw_ref, rb_ref, x1_ref, xl_ref, route_ref, seg_ref,
                *, d, tm, rmax, n_prompt_tiles):
    i = pl.program_id(0)
    is_prompt = i < n_prompt_tiles
    za = jnp.where(is_prompt, zap_ref[...], zas_ref[...])
    zb = jnp.where(is_prompt, zbp_ref[...], zbs_ref[...])
    merged = (jax.nn.sigmoid(ga_ref[...].astype(F32)) * _dot(za, wa_ref[...])
              + jax.nn.sigmoid(gb_ref[...].astype(F32)) * _dot(zb, wb_ref[...]))
    gate1 = mod_ref[0, :, 2 * d:3 * d]
    shift2 = mod_ref[0, :, 3 * d:4 * d]
    scale2 = mod_ref[0, :, 4 * d:5 * d]
    x1 = x_ref[...] + gate1 * _dot(merged.astype(BF16), wo_ref[...])
    x1_ref[...] = x1
    hn = x1 * lax.rsqrt(jnp.mean(x1 * x1, axis=-1, keepdims=True) + EPS) * nw_ref[...]
    hmo = hn * (1.0 + scale2) + shift2

    logits = _dot3(hmo, rw_ref[...]) + rb_ref[...]
    lane = lax.broadcasted_iota(jnp.int32, (1, LANES), 1)
    idxs, vals = [], []
    for _ in range(TOP_K):
        m = jnp.max(logits, axis=1, keepdims=True)
        ik = jnp.min(jnp.where(logits == m, lane, LANES), axis=1, keepdims=True)
        idxs.append(ik)
        vals.append(m)
        logits = jnp.where(lane == ik, -jnp.inf, logits)
    exps = [jnp.exp(v - vals[0]) for v in vals]
    denom = exps[0] + exps[1] + exps[2] + exps[3]

    hot = jnp.zeros((tm, LANES), F32)
    for ik in idxs:
        hot = hot + jnp.where(lane == ik, 1.0, 0.0)
    r_i = lax.broadcasted_iota(jnp.int32, (tm, tm), 0)
    c_i = lax.broadcasted_iota(jnp.int32, (tm, tm), 1)
    below = jnp.where(c_i < r_i, 1.0, 0.0).astype(BF16)
    before = _dot(below, hot.astype(BF16))
    seg = jnp.ceil(jnp.sum(hot, axis=0, keepdims=True) * (1.0 / ROW_ALIGN)) * ROW_ALIGN
    seg_ref[0] = seg
    e_r = lax.broadcasted_iota(jnp.int32, (LANES, LANES), 0)
    e_c = lax.broadcasted_iota(jnp.int32, (LANES, LANES), 1)
    seg_start = _dot3(jnp.broadcast_to(seg, (8, LANES)), jnp.where(e_r < e_c, 1.0, 0.0))[0:1]
    base = seg_start + before

    route = jnp.zeros((tm, LANES), F32)
    for k in range(TOP_K):
        srow = jnp.sum(jnp.where(lane == idxs[k], base, 0.0), axis=1, keepdims=True)
        route = route + jnp.where(lane == k, srow, 0.0)
        route = route + jnp.where(lane == TOP_K + k, idxs[k].astype(F32), 0.0)
        route = route + jnp.where(lane == 2 * TOP_K + k, exps[k] / denom, 0.0)
    route_ref[...] = route

    rt = route.T
    r_iota = lax.broadcasted_iota(jnp.int32, (rmax, 1), 0).astype(F32)
    sel = jnp.zeros((rmax, tm), F32)
    for k in range(TOP_K):
        sel = sel + jnp.where(r_iota == rt[k:k + 1, :], 1.0, 0.0)
    xl_ref[...] = _dot(sel.astype(BF16), hmo.astype(BF16)).astype(BF16)


def _mix(za_p, za_s, zb_p, zb_s, proj, x, mod_rows, wa, wb, wo, norm_w, rw, rb, tm):
    t, d = x.shape
    n_p = za_p.shape[0] // tm
    n_s = za_s.shape[0] // tm
    nt = n_p + n_s
    rmax = _sorted_rows(tm)
    vw = N_RET_HEADS * RET_DV
    pidx = lambda i: (jnp.minimum(i, n_p - 1), 0)
    sidx = lambda i: (jnp.maximum(i - n_p, 0), 0)
    const = lambda i: (0, 0)
    return pl.pallas_call(
        functools.partial(_mix_kernel, d=d, tm=tm, rmax=rmax, n_prompt_tiles=n_p),
        grid=(nt,),
        in_specs=[
            pl.BlockSpec((tm, HY_W), pidx), pl.BlockSpec((tm, HY_W), sidx),
            pl.BlockSpec((tm, vw), pidx), pl.BlockSpec((tm, vw), sidx),
            pl.BlockSpec((tm, d), lambda i: (i, COL_GA // d)),
            pl.BlockSpec((tm, d), lambda i: (i, COL_GB // d)),
            pl.BlockSpec((tm, d), lambda i: (i, 0)),
            pl.BlockSpec((1, 1, 6 * d), lambda i: (i, 0, 0)),
            _resident((HY_W, d), const), _resident((vw, d), const), _resident((d, d), const),
            pl.BlockSpec((1, d), const),
            _resident((d, LANES), const), pl.BlockSpec((1, LANES), const),
        ],
        out_specs=[
            pl.BlockSpec((tm, d), lambda i: (i, 0)),
            pl.BlockSpec((rmax, d), lambda i: (i, 0)),
            pl.BlockSpec((tm, LANES), lambda i: (i, 0)),
            pl.BlockSpec((1, 1, LANES), lambda i: (i, 0, 0)),
        ],
        out_shape=[
            jax.ShapeDtypeStruct((t, d), F32),
            jax.ShapeDtypeStruct((nt * rmax, d), BF16),
            jax.ShapeDtypeStruct((t, LANES), F32),
            jax.ShapeDtypeStruct((nt, 1, LANES), F32),
        ],
        compiler_params=_params(("parallel",)),
        name="merge_router",
    )(za_p, za_s, zb_p, zb_s, proj, proj, x, mod_rows, wa, wb, wo, norm_w.reshape(1, d), rw, rb)


def _ffn_kernel(te_ref, nu_ref, src_ref, xl_ref, wgu_ref, bgu_ref, wd_ref, bd_ref, yl_ref,
                xbuf, ybuf, gsem, ssem, wgu_scr, wd_scr, *, dff, te, empty_chunk):
    j = pl.program_id(0)
    nu = nu_ref[0]
    cpt = te // ROW_ALIGN
    slot = j % 2

    def chunk(ci):
        return pl.ds(pl.multiple_of(ci * ROW_ALIGN, ROW_ALIGN), ROW_ALIGN)

    def gather(tile, buf, start):
        def body(ci, carry):
            cp = pltpu.make_async_copy(xl_ref.at[chunk(src_ref[tile * cpt + ci])],
                                       xbuf.at[buf, chunk(ci)], gsem.at[buf])
            cp.start() if start else cp.wait()
            return carry
        lax.fori_loop(0, cpt, body, 0)

    def scatter(tile, buf, start):
        def body(ci, carry):
            src = src_ref[tile * cpt + ci]

            @pl.when(src != empty_chunk)
            def _():
                cp = pltpu.make_async_copy(ybuf.at[buf, chunk(ci)], yl_ref.at[chunk(src)], ssem.at[buf])
                cp.start() if start else cp.wait()
            return carry
        lax.fori_loop(0, cpt, body, 0)

    @pl.when(j < nu)
    def _():
        @pl.when(j == 0)
        def _():
            gather(0, 0, True)

        @pl.when(j + 1 < nu)
        def _():
            gather(j + 1, 1 - slot, True)

        gather(j, slot, False)

        @pl.when(j >= 2)
        def _():
            scatter(j - 2, slot, False)

        @pl.when(jnp.logical_or(j == 0, te_ref[j] != te_ref[jnp.maximum(j - 1, 0)]))
        def _():
            wgu_scr[...] = wgu_ref[0].astype(BF16)
            wd_scr[...] = wd_ref[0].astype(BF16)

        hgu = _dot(xbuf[slot], wgu_scr[...]) + bgu_ref[0]
        gate = jnp.minimum(hgu[:, :dff], SWIGLU_LIMIT)
        up = jnp.clip(hgu[:, dff:], -SWIGLU_LIMIT, SWIGLU_LIMIT)
        act = gate * jax.nn.sigmoid(SWIGLU_ALPHA * gate) * (up + 1.0)
        ybuf[slot] = (_dot(act.astype(BF16), wd_scr[...]) + bd_ref[0]).astype(BF16)
        scatter(j, slot, True)

        @pl.when(j == nu - 1)
        def _():
            @pl.when(j >= 1)
            def _():
                scatter(j - 1, 1 - slot, False)
            scatter(j, slot, False)


def _expert_ffn(xl, tile_expert, n_used, src_chunk, w_gate_up, b_gate_up, w_down, b_down, te, empty_chunk):
    _, d = xl.shape
    ne, _, two_ff = w_gate_up.shape
    dff = two_ff // 2
    wsel = lambda j, tex, nu, src: (tex[j], 0, 0)
    return pl.pallas_call(
        functools.partial(_ffn_kernel, dff=dff, te=te, empty_chunk=empty_chunk),
        grid_spec=pltpu.PrefetchScalarGridSpec(
            num_scalar_prefetch=3,
            grid=(tile_expert.shape[0],),
            in_specs=[
                pl.BlockSpec(memory_space=pl.ANY),
                pl.BlockSpec((1, d, two_ff), wsel),
                pl.BlockSpec((1, 1, two_ff), wsel),
                pl.BlockSpec((1, dff, d), wsel),
                pl.BlockSpec((1, 1, d), wsel),
            ],
            out_specs=pl.BlockSpec(memory_space=pl.ANY),
            scratch_shapes=[
                pltpu.VMEM((2, te, d), BF16), pltpu.VMEM((2, te, d), BF16),
                pltpu.SemaphoreType.DMA((2,)), pltpu.SemaphoreType.DMA((2,)),
                pltpu.VMEM((d, two_ff), BF16), pltpu.VMEM((dff, d), BF16),
            ],
        ),
        out_shape=jax.ShapeDtypeStruct(xl.shape, BF16),
        input_output_aliases={3: 0},
        compiler_params=_params(("arbitrary",)),
        name="moe_ffn",
    )(tile_expert, n_used, src_chunk, xl, w_gate_up, b_gate_up.reshape(ne, 1, two_ff),
      w_down, b_down.reshape(ne, 1, d))


def _comb_kernel(yl_ref, route_ref, x1_ref, mod_ref, fw_ref, o_ref, *, d, tm, rmax, final_norm):
    route = route_ref[...]
    r_iota = lax.broadcasted_iota(jnp.int32, (1, rmax), 1).astype(F32)
    wm = jnp.zeros((tm, rmax), F32)
    for k in range(TOP_K):
        wm = wm + jnp.where(r_iota == route[:, k:k + 1], route[:, 2 * TOP_K + k:2 * TOP_K + k + 1], 0.0)
    out = _dot(wm.astype(BF16), yl_ref[...])
    x2 = x1_ref[...] + mod_ref[0, :, 5 * d:6 * d] * out
    if final_norm:
        x2 = x2 * lax.rsqrt(jnp.mean(x2 * x2, axis=-1, keepdims=True) + EPS) * fw_ref[...]
    o_ref[...] = x2


def _combine(yl, route, x1, mod_rows, final_w, tm, final_norm):
    t, d = x1.shape
    rmax = _sorted_rows(tm)
    return pl.pallas_call(
        functools.partial(_comb_kernel, d=d, tm=tm, rmax=rmax, final_norm=final_norm),
        grid=(t // tm,),
        in_specs=[
            pl.BlockSpec((rmax, d), lambda i: (i, 0)),
            pl.BlockSpec((tm, LANES), lambda i: (i, 0)),
            pl.BlockSpec((tm, d), lambda i: (i, 0)),
            pl.BlockSpec((1, 1, 6 * d), lambda i: (i, 0, 0)),
            pl.BlockSpec((1, d), lambda i: (0, 0)),
        ],
        out_specs=pl.BlockSpec((tm, d), lambda i: (i, 0)),
        out_shape=jax.ShapeDtypeStruct((t, d), F32),
        compiler_params=_params(("parallel",)),
        name="moe_combine",
    )(yl, route, x1, mod_rows, final_w.reshape(1, d))


def _dft_mats(l):
    f = jnp.arange(l, dtype=jnp.int32)
    ang = ((f[:, None] * f[None, :]) % (2 * l)).astype(F32) * (math.pi / l)
    return jnp.cos(ang).astype(BF16), jnp.sin(ang).astype(BF16)


def _filter_features(l):
    t = jnp.linspace(0.0, 1.0, l, dtype=F32)[:, None]
    bands = (HY_EMB - 1) // 2
    w = 2.0 * math.pi * jnp.arange(l, dtype=F32)[:, None] / l
    f = jnp.linspace(1e-4, bands - 1, bands, dtype=F32)[None, :]
    z = jnp.concatenate([t, jnp.cos(f * w), -jnp.sin(f * w)], axis=-1)
    return jnp.pad(z, ((0, 0), (0, LANES - HY_EMB))), t


def _rope_tables(rows):
    row = jnp.repeat(jnp.arange(rows, dtype=F32), GRID_W)
    col = jnp.tile(jnp.arange(GRID_W, dtype=F32), rows)
    half = RET_DK // 4
    inv = ROPE_BASE ** (-jnp.arange(half, dtype=F32) / half)
    ang = jnp.concatenate([row[:, None] * inv, col[:, None] * inv], axis=-1)
    ang = jnp.concatenate([ang, ang], axis=-1)
    sign = jnp.where(jnp.arange(RET_DK) < RET_DK // 2, -1.0, 1.0).astype(F32)
    return jnp.cos(ang), jnp.sin(ang) * sign


def _pad2(a, rows, cols):
    return jnp.pad(a, ((0, rows - a.shape[0]), (0, cols - a.shape[1])))


def _ffn_plan(seg, rmax, te, n_ffn_tiles):
    nt, ne = seg.shape
    cum_t = jnp.cumsum(seg, axis=0)
    tot = cum_t[-1]
    padded = ((tot + te - 1) // te) * te
    ends = jnp.cumsum(padded)
    offs = ends - padded
    n_used = ends[-1] // te
    tiles = jnp.arange(n_ffn_tiles, dtype=jnp.int32)
    tile_expert = jnp.minimum(jnp.searchsorted(ends, tiles * te, side="right"), ne - 1).astype(jnp.int32)
    tile_expert = jnp.where(tiles < n_used, tile_expert, tile_expert[jnp.maximum(n_used - 1, 0)])

    rows = jnp.arange(n_ffn_tiles * te // ROW_ALIGN, dtype=jnp.int32) * ROW_ALIGN
    e_s = jnp.minimum(jnp.searchsorted(ends, rows, side="right"), ne - 1)
    r_s = rows - offs[e_s]
    cum_sel = cum_t.T[e_s]
    i_s = jnp.minimum(jnp.sum(cum_sel <= r_s[:, None], axis=1), nt - 1)
    seg_sel = seg.T[e_s, i_s]
    seg_begin = jnp.take_along_axis(cum_sel, i_s[:, None], axis=1)[:, 0] - seg_sel
    in_tile_start = (jnp.cumsum(seg, axis=1) - seg)[i_s, e_s]
    src_row = i_s * rmax + in_tile_start + (r_s - seg_begin)
    valid = (r_s < tot[e_s]) & (rows < ends[-1])
    src_chunk = jnp.where(valid, src_row // ROW_ALIGN, rmax // ROW_ALIGN - 1).astype(jnp.int32)
    return tile_expert, n_used.reshape(1).astype(jnp.int32), src_chunk


def kernel(x_prompt, x_sample, state_ret, c, c_ctx, ada_w, ada_b, norm_mix_w, w_in, conv_w, conv_b,
           f_w1, f_b1, f_w2, f_b2, f_w3, f_b3, f_freq, f_wout, hy_skip, ret_decay_logit,
           w_branch_a, w_branch_b, w_out, norm_ffn_w, router_w, router_b, w_gate_up, b_gate_up,
           w_down, b_down, final_norm_w):
    bp, lp, d = x_prompt.shape
    bs, ls, _ = x_sample.shape
    depth = ada_w.shape[0]
    tp, ts = bp * lp, bs * ls
    t = tp + ts
    assert ls % lp == 0 and tp % ls == 0 and lp % RET_CHUNK == 0 and ls % GRID_W == 0
    tm = 512 if (tp % 512 == 0 and ls % 512 == 0) else lp
    te = 512
    nt = t // tm
    rmax = _sorted_rows(tm)
    empty_chunk = rmax // ROW_ALIGN - 1
    n_ffn_tiles = -(-(t * TOP_K + nt * N_EXPERTS * (ROW_ALIGN - 1)) // te) + N_EXPERTS

    dft = {l: _dft_mats(l) for l in (lp, ls)}
    feats = {l: _filter_features(l) for l in (lp, ls)}
    rope_tabs = _rope_tables(ls // GRID_W)
    deltas = jnp.linspace(math.log(HY_TARGET) / HY_SLOW_PCT, math.log(HY_TARGET) / HY_FAST_PCT,
                          HY_W, dtype=F32)[None, :]
    tile_mod = jnp.concatenate([jnp.full((tp // tm,), bs, jnp.int32),
                                jnp.repeat(jnp.arange(bs, dtype=jnp.int32), ls // tm)])
    perm = jnp.concatenate([jnp.arange(a, b) for a, b in (
        (2560, 3584), (3584, 4608), (4608, 5632), (5632, 6656), (0, 1536), (1536, 2560))])

    cvec = jnp.zeros((8, d), F32).at[:bs].set(c).at[bs].set(c_ctx)
    mod = _modulation(cvec, ada_w, ada_b)

    x = jnp.concatenate([x_prompt.reshape(tp, d), x_sample.reshape(ts, d)], axis=0)
    states = []
    for l in range(depth):
        mod_rows = mod[l][tile_mod][:, None, :]
        w_in_bf = w_in[l][:, perm].astype(BF16)
        proj = _in_proj(x, norm_mix_w[l], mod_rows, w_in_bf, tm)

        w1 = _pad2(f_w1[l], LANES, LANES)
        w2 = _pad2(f_w2[l], LANES, LANES)
        w3 = _pad2(f_w3[l], LANES, LANES)
        wo = _pad2(f_wout[l], LANES, f_wout.shape[2])
        b1, b2, b3, fr = (_pad2(a[l][None, :], 1, LANES) for a in (f_b1, f_b2, f_b3, f_freq))
        za, zb = {}, {}
        for grp, (row_blk0, nb, ln) in {"p": (0, bp, lp), "s": (tp // ls, bs, ls)}.items():
            cm, sm = dft[ln]
            taps, sums = _hyena_taps(ln, feats[ln][0], feats[ln][1], deltas, w1, b1, w2, b2, w3, b3, fr, wo)
            kc, ki, kn = _filter_spectra(ln, cm, sm, taps, sums)
            za[grp] = _hyena_branch(proj, row_blk0, nb, ln, cm, sm, conv_w[l], conv_b[l], kc, ki, kn, hy_skip[l])
            zb[grp], st = _retention_branch(proj, row_blk0, nb, ln, ret_decay_logit[l],
                                            rope_tabs if grp == "s" else None,
                                            state_ret if grp == "s" else None, l)
            if grp == "p":
                states.append(st)

        rw = _pad2(router_w[l], d, LANES)
        rb = jnp.full((1, LANES), -1e30, F32).at[0, :N_EXPERTS].set(router_b[l])
        x1, xl, route, seg = _mix(za["p"], za["s"], zb["p"], zb["s"], proj, x, mod_rows,
                                  w_branch_a[l].astype(BF16), w_branch_b[l].astype(BF16),
                                  w_out[l].astype(BF16), norm_ffn_w[l], rw, rb, tm)
        tile_expert, n_used, src_chunk = _ffn_plan(seg[:, 0, :N_EXPERTS].astype(jnp.int32), rmax, te, n_ffn_tiles)
        yl = _expert_ffn(xl, tile_expert, n_used, src_chunk, w_gate_up[l], b_gate_up[l],
                         w_down[l], b_down[l], te, empty_chunk)
        x = _combine(yl, route, x1, mod_rows, final_norm_w, tm, l == depth - 1)

    y_prompt = x[:tp].reshape(bp, lp, d)
    y_sample = x[tp:].reshape(bs, ls, d)
    state_ret_new = jnp.stack(states, axis=1)
    return (y_prompt, y_sample, state_ret_new)
```

```python
import functools
import math

import jax
import jax.numpy as jnp
import numpy as np
from jax import lax
from jax.experimental import pallas as pl
from jax.experimental.pallas import tpu as pltpu

F32 = jnp.float32
BF16 = jnp.bfloat16

GRID_W = 64
HY_W = 512
HY_ORDER = 2
HY_EMB = 33
HY_FAST_PCT = 0.3
HY_SLOW_PCT = 1.5
HY_TARGET = 1e-2
N_RET_HEADS = 4
RET_DK = 128
RET_DV = 256
RET_CHUNK = 128
ROPE_BASE = 10000.0
N_EXPERTS = 32
TOP_K = 4
SWIGLU_LIMIT = 7.0
SWIGLU_ALPHA = 1.702
EPS = 1e-6

LANES = 128
VMEM_LIMIT = 56 * 1024 * 1024

COL_V, COL_G, COL_GA, COL_GB = 0, 1024, 2048, 3072
COL_HV, COL_HX1, COL_HX2, COL_Q, COL_K = 4096, 4608, 5120, 5632, 6144
IN_W = 6656
IN_COL_BLOCKS = (5, 6, 7, 8, 9, 10, 11, 12, 0, 1, 2, 3, 4)


def _dot(a, b):
    return jnp.dot(a, b, preferred_element_type=F32)


def _split_bf16(a):
    hi = a.astype(BF16)
    lo = (a - hi.astype(F32)).astype(BF16)
    return hi, lo


def _dot3(a, b):
    ah, al = _split_bf16(a)
    bh, bl = _split_bf16(b)
    return _dot(ah, bh) + (_dot(ah, bl) + _dot(al, bh))


def _silu(x):
    return x * jax.nn.sigmoid(x)


def _params(sem, vmem=VMEM_LIMIT):
    return pltpu.CompilerParams(dimension_semantics=sem, vmem_limit_bytes=vmem)


def _resident(shape, index_map):
    return pl.BlockSpec(shape, index_map, pipeline_mode=pl.Buffered(1))


def _mod_kernel(c_ref, w_ref, b_ref, o_ref):
    o_ref[0] = _dot3(_silu(c_ref[...]), w_ref[0]) + b_ref[0]


def _modulation(cvec, ada_w, ada_b):
    depth, d, six_d = ada_w.shape
    tn = six_d // 4
    return pl.pallas_call(
        _mod_kernel,
        grid=(depth, six_d // tn),
        in_specs=[
            pl.BlockSpec((8, d), lambda l, j: (0, 0)),
            pl.BlockSpec((1, d, tn), lambda l, j: (l, 0, j)),
            pl.BlockSpec((1, 1, tn), lambda l, j: (l, 0, j)),
        ],
        out_specs=pl.BlockSpec((1, 8, tn), lambda l, j: (l, 0, j)),
        out_shape=jax.ShapeDtypeStruct((depth, 8, six_d), F32),
        compiler_params=_params(("parallel", "parallel")),
        name="adaln_mod",
    )(cvec, ada_w, ada_b.reshape(depth, 1, six_d))


def _in_kernel(x_ref, nw_ref, mod_ref, w_ref, o_ref, *, d, cw):
    x = x_ref[...]
    y = x * lax.rsqrt(jnp.mean(x * x, axis=-1, keepdims=True) + EPS) * nw_ref[...]
    shift = mod_ref[0, :, 0:d]
    scale = mod_ref[0, :, d:2 * d]
    u = (y * (1.0 + scale) + shift).astype(BF16)
    for j, src in enumerate(IN_COL_BLOCKS):
        o_ref[:, j * cw:(j + 1) * cw] = _dot(u, w_ref[:, src * cw:(src + 1) * cw]).astype(BF16)


def _in_proj(x, norm_w, mod3, mod_row, w_in_bf, tm):
    t, d = x.shape
    return pl.pallas_call(
        functools.partial(_in_kernel, d=d, cw=IN_W // len(IN_COL_BLOCKS)),
        grid=(t // tm,),
        in_specs=[
            pl.BlockSpec((tm, d), lambda i: (i, 0)),
            pl.BlockSpec((1, d), lambda i: (0, 0)),
            pl.BlockSpec((1, 1, 2 * d), lambda i: (mod_row(i), 0, 0)),
            _resident((d, IN_W), lambda i: (0, 0)),
        ],
        out_specs=pl.BlockSpec((tm, IN_W), lambda i: (i, 0)),
        out_shape=jax.ShapeDtypeStruct((t, IN_W), BF16),
        compiler_params=_params(("parallel",)),
        name="in_proj",
    )(x, norm_w.reshape(1, d), mod3, w_in_bf)


def _filt_kernel(z_ref, t_ref, w1, b1, w2, b2, w3, b3, fr, wo, dl_ref, h_ref, s_ref, *, tl):
    i = pl.program_id(0)
    freq = fr[...]
    h = jnp.sin(freq * (_dot3(z_ref[...], w1[...]) + b1[...]))
    h = jnp.sin(freq * (_dot3(h, w2[...]) + b2[...]))
    h = jnp.sin(freq * (_dot3(h, w3[...]) + b3[...]))
    h = _dot3(h, wo[...])
    decay = jnp.exp(-t_ref[...] * jnp.abs(dl_ref[...]))
    h = h * jnp.concatenate([decay] * (2 * HY_ORDER), axis=1)
    h_ref[...] = h
    row = i * tl + lax.broadcasted_iota(jnp.int32, (tl, 1), 0)
    col = lax.broadcasted_iota(jnp.int32, (1, 2 * HY_ORDER * HY_W), 1)
    is_bwd = ((col // HY_W) % 2) == 1
    part = jnp.sum(jnp.where((row == 0) & is_bwd, 0.0, jnp.abs(h)), axis=0, keepdims=True)

    @pl.when(i == 0)
    def _():
        s_ref[...] = jnp.zeros_like(s_ref)

    s_ref[...] += part


def _hyena_taps(l, zfeat, tcol, deltas, w1, b1, w2, b2, w3, b3, freq, wout):
    tl = min(l, 256)
    nw = 2 * HY_ORDER * HY_W
    full = lambda shape: pl.BlockSpec(shape, lambda i: (0,) * len(shape))
    return pl.pallas_call(
        functools.partial(_filt_kernel, tl=tl),
        grid=(l // tl,),
        in_specs=[
            pl.BlockSpec((tl, LANES), lambda i: (i, 0)),
            pl.BlockSpec((tl, 1), lambda i: (i, 0)),
            full((LANES, LANES)), full((1, LANES)),
            full((LANES, LANES)), full((1, LANES)),
            full((LANES, LANES)), full((1, LANES)),
            full((1, LANES)),
            full((LANES, nw)),
            full((1, HY_W)),
        ],
        out_specs=[pl.BlockSpec((tl, nw), lambda i: (i, 0)), pl.BlockSpec((1, nw), lambda i: (0, 0))],
        out_shape=[jax.ShapeDtypeStruct((l, nw), F32), jax.ShapeDtypeStruct((1, nw), F32)],
        compiler_params=_params(("arbitrary",)),
        name=f"hyena_taps_{l}",
    )(zfeat, tcol, w1, b1, w2, b2, w3, b3, freq, wout, deltas)


def _fdft_kernel(c_ref, s_ref, hf_ref, hb_ref, sf_ref, sb_ref, kc_ref, ki_ref, kn_ref, *, l):
    row = lax.broadcasted_iota(jnp.int32, (l, 1), 0)
    hf = hf_ref[...]
    hb = jnp.where(row == 0, 0.0, hb_ref[...])
    inv = 1.0 / (sf_ref[...] + sb_ref[...])
    a = hf + hb
    b = hb - hf
    wf = jnp.where(row == 0, 0.5 / l, 1.0 / l)
    kc_ref[...] = _dot(c_ref[...], a.astype(BF16)) * wf * inv
    ki_ref[...] = _dot(s_ref[...], b.astype(BF16)) * wf * inv
    sgn = jnp.where(row % 2 == 0, 1.0, -1.0)
    kn_ref[...] = jnp.sum(a * sgn, axis=0, keepdims=True) * inv * (0.5 / l)


def _filter_spectra(l, cmat, smat, taps, sums):
    wt = 256
    nb = HY_W // wt
    two_w = 2 * HY_W
    return pl.pallas_call(
        functools.partial(_fdft_kernel, l=l),
        grid=(HY_ORDER, nb),
        in_specs=[
            _resident((l, l), lambda o, j: (0, 0)),
            _resident((l, l), lambda o, j: (0, 0)),
            pl.BlockSpec((l, wt), lambda o, j: (0, o * 2 * nb + j)),
            pl.BlockSpec((l, wt), lambda o, j: (0, o * 2 * nb + nb + j)),
            pl.BlockSpec((1, wt), lambda o, j: (0, o * 2 * nb + j)),
            pl.BlockSpec((1, wt), lambda o, j: (0, o * 2 * nb + nb + j)),
        ],
        out_specs=[
            pl.BlockSpec((l, wt), lambda o, j: (0, o * nb + j)),
            pl.BlockSpec((l, wt), lambda o, j: (0, o * nb + j)),
            pl.BlockSpec((1, wt), lambda o, j: (0, o * nb + j)),
        ],
        out_shape=[
            jax.ShapeDtypeStruct((l, HY_ORDER * HY_W), F32),
            jax.ShapeDtypeStruct((l, HY_ORDER * HY_W), F32),
            jax.ShapeDtypeStruct((1, HY_ORDER * HY_W), F32),
        ],
        compiler_params=_params(("parallel", "parallel")),
        name=f"hyena_spectra_{l}",
    )(cmat, smat, taps, taps, sums, sums)


def _hy_kernel(c_ref, s_ref, hv_ref, hx1_ref, hx2_ref, cw_ref, cb_ref,
               kc0, ki0, kc1, ki1, kn_ref, sk_ref, o_ref, z_scr, zb_scr, p_scr, q_scr, *, l, rc):
    row = lax.broadcasted_iota(jnp.int32, (l, 1), 0)
    first = row == 0
    last = row == l - 1
    sgn = jnp.where(row % 2 == 0, 1.0, -1.0)
    cw = cw_ref[...]
    cb = cb_ref[...]

    def short_conv(ref, part):
        x = ref[...].astype(F32)
        xm = jnp.where(first, 0.0, pltpu.roll(x, 1, 0))
        xp = jnp.where(last, 0.0, pltpu.roll(x, l - 1, 0))
        return (xm * cw[0, part:part + 1] + x * cw[1, part:part + 1] + xp * cw[2, part:part + 1]
                + cb[part:part + 1])

    z_scr[...] = short_conv(hv_ref, 0)
    for o, (gate_ref, kc, ki) in enumerate(((hx1_ref, kc0, ki0), (hx2_ref, kc1, ki1))):
        z = z_scr[...]
        zb_scr[...] = z.astype(BF16)
        nyq = jnp.sum(z * sgn, axis=0, keepdims=True) * kn_ref[o:o + 1]
        for r in range(l // rc):
            sl = slice(r * rc, (r + 1) * rc)
            zc = _dot(c_ref[sl, :], zb_scr[...])
            zs = _dot(s_ref[sl, :], zb_scr[...])
            p_scr[sl, :] = (zc * kc[sl, :] + zs * ki[sl, :]).astype(BF16)
            q_scr[sl, :] = (zs * kc[sl, :] - zc * ki[sl, :]).astype(BF16)
        gate = short_conv(gate_ref, o + 1)
        for r in range(l // rc):
            sl = slice(r * rc, (r + 1) * rc)
            y = _dot(c_ref[sl, :], p_scr[...]) + _dot(s_ref[sl, :], q_scr[...])
            y = y + sgn[sl] * nyq + z_scr[sl, :] * sk_ref[o:o + 1]
            z_scr[sl, :] = gate[sl] * y
    o_ref[...] = z_scr[...].astype(BF16)


def _hyena_branch(proj, row_blk0, nb, l, cmat, smat, conv_w, conv_b, kc, ki, kn, skip):
    wt = 256
    nw = HY_W // wt
    rc = min(l, 512)
    hv0, hx10, hx20 = COL_HV // wt, COL_HX1 // wt, COL_HX2 // wt
    seq = lambda c0: pl.BlockSpec((l, wt), lambda j, b: (row_blk0 + b, c0 + j))
    tab = lambda o: pl.BlockSpec((l, wt), lambda j, b: (0, o * nw + j))
    return pl.pallas_call(
        functools.partial(_hy_kernel, l=l, rc=rc),
        grid=(nw, nb),
        in_specs=[
            _resident((l, l), lambda j, b: (0, 0)),
            _resident((l, l), lambda j, b: (0, 0)),
            seq(hv0), seq(hx10), seq(hx20),
            pl.BlockSpec((3, 3, wt), lambda j, b: (0, 0, j)),
            pl.BlockSpec((3, wt), lambda j, b: (0, j)),
            tab(0), tab(0), tab(1), tab(1),
            pl.BlockSpec((HY_ORDER, wt), lambda j, b: (0, j)),
            pl.BlockSpec((HY_ORDER, wt), lambda j, b: (0, j)),
        ],
        out_specs=pl.BlockSpec((l, wt), lambda j, b: (b, j)),
        out_shape=jax.ShapeDtypeStruct((nb * l, HY_W), BF16),
        scratch_shapes=[
            pltpu.VMEM((l, wt), F32), pltpu.VMEM((l, wt), BF16),
            pltpu.VMEM((l, wt), BF16), pltpu.VMEM((l, wt), BF16),
        ],
        compiler_params=_params(("parallel", "parallel")),
        name=f"hyena_{l}",
    )(cmat, smat, proj, proj, proj, conv_w.reshape(3, 3, HY_W), conv_b.reshape(3, HY_W),
      kc, ki, kc, ki, kn.reshape(HY_ORDER, HY_W), skip)


def _log_sigmoid(x):
    return jnp.minimum(x, 0.0) - jnp.log1p(jnp.exp(-jnp.abs(x)))


def _ret_kernel(*refs, l, hb, use_rope, has_s0):
    it = iter(refs)
    dl_ref, q_ref, k_ref, v_ref, g_ref = (next(it) for _ in range(5))
    cos_ref, sin_ref = (next(it), next(it)) if use_rope else (None, None)
    s0_ref = next(it) if has_s0 else None
    o_ref, st_ref, kr_scr, kv_scr = (next(it) for _ in range(4))

    c, dk, dv = RET_CHUNK, RET_DK, RET_DV
    nc = l // c
    unroll = min(nc, 4)
    pos = lax.broadcasted_iota(jnp.int32, (c, 1), 0).astype(F32)
    diff = pos - lax.broadcasted_iota(jnp.int32, (1, c), 1).astype(F32)

    def rope(x, r0):
        if not use_rope:
            return x
        return x * cos_ref[pl.ds(r0, c), :] + pltpu.roll(x, dk // 2, 1) * sin_ref[pl.ds(r0, c), :]

    for h in range(hb):
        qs = slice(h * dk, (h + 1) * dk)
        vs = slice(h * dv, (h + 1) * dv)
        lgf = _log_sigmoid(dl_ref[0, h])
        lgb = _log_sigmoid(dl_ref[1, h])
        dm = jnp.where(diff >= 0, jnp.exp(lgf * diff), jnp.exp(-lgb * diff))
        zeta_f = jnp.exp(lgf * (c - 1.0 - pos))
        zeta_b = jnp.exp(lgb * pos)
        xi_f = jnp.exp(lgf * (pos + 1.0))
        xi_b = jnp.exp(lgb * (c - pos))
        gc_f = jnp.exp(lgf * c)
        gc_b = jnp.exp(lgb * c)

        def kv_step(ci, carry):
            r0 = pl.multiple_of(ci * c, c)
            k = rope(k_ref[pl.ds(r0, c), qs].astype(F32), r0) * (dk ** -0.5)
            kr_scr[pl.ds(r0, c), qs] = k.astype(BF16)
            kk = jnp.concatenate([k * zeta_f, k * zeta_b], axis=1).astype(BF16)
            kv_scr[h, ci] = lax.dot_general(kk, v_ref[pl.ds(r0, c), vs], (((0,), (0,)), ((), ())),
                                            preferred_element_type=F32)
            return carry

        lax.fori_loop(0, nc, kv_step, 0, unroll=unroll)

        def fwd_scan(ci, s):
            upd = kv_scr[h, ci, 0:dk, :]
            kv_scr[h, ci, 0:dk, :] = s
            return gc_f * s + upd

        def bwd_scan(i, r):
            ci = nc - 1 - i
            upd = kv_scr[h, ci, dk:2 * dk, :]
            kv_scr[h, ci, dk:2 * dk, :] = r
            return gc_b * r + upd

        zero = jnp.zeros((dk, dv), F32)
        st_ref[0, 0, h] = lax.fori_loop(0, nc, fwd_scan, s0_ref[0, 0, 0, h] if has_s0 else zero)
        st_ref[0, 1, h] = lax.fori_loop(0, nc, bwd_scan, s0_ref[0, 0, 1, h] if has_s0 else zero)

        def out_step(ci, carry):
            r0 = pl.multiple_of(ci * c, c)
            q = rope(q_ref[pl.ds(r0, c), qs].astype(F32), r0)
            sc = lax.dot_general(q.astype(BF16), kr_scr[pl.ds(r0, c), qs], (((1,), (1,)), ((), ())),
                                 preferred_element_type=F32) * dm
            qq = jnp.concatenate([q * xi_f, q * xi_b], axis=1).astype(BF16)
            o = _dot(sc.astype(BF16), v_ref[pl.ds(r0, c), vs]) + _dot(qq, kv_scr[h, ci].astype(BF16))
            o = o * lax.rsqrt(jnp.mean(o * o, axis=-1, keepdims=True) + EPS)
            o_ref[pl.ds(r0, c), vs] = (_silu(g_ref[pl.ds(r0, c), vs].astype(F32)) * o).astype(BF16)
            return carry

        lax.fori_loop(0, nc, out_step, 0, unroll=unroll)


def _retention_branch(proj, row_blk0, nb, l, decay_logit, rope_tabs, state, layer):
    nh, dk, dv = N_RET_HEADS, RET_DK, RET_DV
    use_rope = rope_tabs is not None
    has_s0 = state is not None
    hb = nh if l <= 512 else 1
    ng = nh // hb
    seq = lambda col, w: pl.BlockSpec((l, hb * w), lambda b, g: (row_blk0 + b, col // (hb * w) + g))
    in_specs = [
        pl.BlockSpec((2, hb, 1, 1), lambda b, g: (0, g, 0, 0)),
        seq(COL_Q, dk), seq(COL_K, dk), seq(COL_V, dv), seq(COL_G, dv),
    ]
    args = [decay_logit.reshape(2, nh, 1, 1), proj, proj, proj, proj]
    if use_rope:
        in_specs += [pl.BlockSpec((l, dk), lambda b, g: (0, 0))] * 2
        args += list(rope_tabs)
    if has_s0:
        in_specs.append(pl.BlockSpec((1, 1, 2, hb, dk, dv), lambda b, g: (b, layer, 0, g, 0, 0)))
        args.append(state)
    return pl.pallas_call(
        functools.partial(_ret_kernel, l=l, hb=hb, use_rope=use_rope, has_s0=has_s0),
        grid=(nb, ng),
        in_specs=in_specs,
        out_specs=[
            pl.BlockSpec((l, hb * dv), lambda b, g: (b, g)),
            pl.BlockSpec((1, 2, hb, dk, dv), lambda b, g: (b, 0, g, 0, 0)),
        ],
        out_shape=[
            jax.ShapeDtypeStruct((nb * l, nh * dv), BF16),
            jax.ShapeDtypeStruct((nb, 2, nh, dk, dv), F32),
        ],
        scratch_shapes=[
            pltpu.VMEM((l, hb * dk), BF16),
            pltpu.VMEM((hb, l // RET_CHUNK, 2 * dk, dv), F32),
        ],
        compiler_params=_params(("parallel", "parallel")),
        name=f"retention_{l}",
    )(*args)


ROW_ALIGN = 16


def _sorted_rows(tm):
    worst = TOP_K * tm + N_EXPERTS * (ROW_ALIGN - 1) + ROW_ALIGN
    return -(-worst // LANES) * LANES


def _mix_kernel(zap_ref, zas_ref, zbp_ref, zbs_ref, gap_ref, gas_ref, gbp_ref, gbs_ref, xp_ref, xs_ref,
                mod_ref, wa_ref, wb_ref, wo_ref, nw_ref, rw_ref, rb_ref, x1_ref, xl_ref, route_ref, seg_ref,
                *, d, tm, rmax, n_prompt_tiles):
    is_prompt = pl.program_id(0) < n_prompt_tiles
    pick = lambda p_ref, s_ref: jnp.where(is_prompt, p_ref[...], s_ref[...])
    za, zb = pick(zap_ref, zas_ref), pick(zbp_ref, zbs_ref)
    ga, gb = pick(gap_ref, gas_ref), pick(gbp_ref, gbs_ref)
    merged = (jax.nn.sigmoid(ga.astype(F32)) * _dot(za, wa_ref[...])
              + jax.nn.sigmoid(gb.astype(F32)) * _dot(zb, wb_ref[...]))
    gate1 = mod_ref[0, :, 2 * d:3 * d]
    shift2 = mod_ref[0, :, 3 * d:4 * d]
    scale2 = mod_ref[0, :, 4 * d:5 * d]
    x1 = pick(xp_ref, xs_ref) + gate1 * _dot(merged.astype(BF16), wo_ref[...])
    x1_ref[...] = x1
    hn = x1 * lax.rsqrt(jnp.mean(x1 * x1, axis=-1, keepdims=True) + EPS) * nw_ref[...]
    hmo = hn * (1.0 + scale2) + shift2

    logits = _dot3(hmo, rw_ref[...]) + rb_ref[...]
    lane = lax.broadcasted_iota(jnp.int32, (1, LANES), 1)
    idxs, vals = [], []
    for _ in range(TOP_K):
        m = jnp.max(logits, axis=1, keepdims=True)
        ik = jnp.min(jnp.where(logits == m, lane, LANES), axis=1, keepdims=True)
        idxs.append(ik)
        vals.append(m)
        logits = jnp.where(lane == ik, -jnp.inf, logits)
    exps = [jnp.exp(v - vals[0]) for v in vals]
    denom = exps[0] + exps[1] + exps[2] + exps[3]

    hot = jnp.zeros((tm, LANES), F32)
    for ik in idxs:
        hot = hot + jnp.where(lane == ik, 1.0, 0.0)
    r_i = lax.broadcasted_iota(jnp.int32, (tm, tm), 0)
    c_i = lax.broadcasted_iota(jnp.int32, (tm, tm), 1)
    below = jnp.where(c_i < r_i, 1.0, 0.0).astype(BF16)
    before = _dot(below, hot.astype(BF16))
    seg = jnp.ceil(jnp.sum(hot, axis=0, keepdims=True) * (1.0 / ROW_ALIGN)) * ROW_ALIGN
    seg_ref[0] = seg
    e_r = lax.broadcasted_iota(jnp.int32, (LANES, LANES), 0)
    e_c = lax.broadcasted_iota(jnp.int32, (LANES, LANES), 1)
    seg_start = _dot3(jnp.broadcast_to(seg, (8, LANES)), jnp.where(e_r < e_c, 1.0, 0.0))[0:1]
    base = seg_start + before

    route = jnp.zeros((tm, LANES), F32)
    for k in range(TOP_K):
        srow = jnp.sum(jnp.where(lane == idxs[k], base, 0.0), axis=1, keepdims=True)
        route = route + jnp.where(lane == k, srow, 0.0)
        route = route + jnp.where(lane == TOP_K + k, idxs[k].astype(F32), 0.0)
        route = route + jnp.where(lane == 2 * TOP_K + k, exps[k] / denom, 0.0)
    route_ref[...] = route

    rt = route.T
    r_iota = lax.broadcasted_iota(jnp.int32, (rmax, 1), 0).astype(F32)
    sel = jnp.zeros((rmax, tm), F32)
    for k in range(TOP_K):
        sel = sel + jnp.where(r_iota == rt[k:k + 1, :], 1.0, 0.0)
    xl_ref[...] = _dot(sel.astype(BF16), hmo.astype(BF16)).astype(BF16)


def _mix(za, zb, proj, x, mod3, mod_row, wa, wb, wo, norm_w, rw, rb, tm):
    d = x[0].shape[1]
    n_p = x[0].shape[0] // tm
    n_s = x[1].shape[0] // tm
    nt = n_p + n_s
    t = nt * tm
    rmax = _sorted_rows(tm)
    vw = N_RET_HEADS * RET_DV
    pair = lambda w, col: [pl.BlockSpec((tm, w), lambda i: (jnp.minimum(i, n_p - 1), col)),
                           pl.BlockSpec((tm, w), lambda i: (jnp.maximum(i - n_p, 0), col))]
    const = lambda i: (0, 0)
    return pl.pallas_call(
        functools.partial(_mix_kernel, d=d, tm=tm, rmax=rmax, n_prompt_tiles=n_p),
        grid=(nt,),
        in_specs=[
            *pair(HY_W, 0), *pair(vw, 0), *pair(d, COL_GA // d), *pair(d, COL_GB // d), *pair(d, 0),
            pl.BlockSpec((1, 1, 6 * d), lambda i: (mod_row(i), 0, 0)),
            _resident((HY_W, d), const), _resident((vw, d), const), _resident((d, d), const),
            pl.BlockSpec((1, d), const),
            _resident((d, LANES), const), pl.BlockSpec((1, LANES), const),
        ],
        out_specs=[
            pl.BlockSpec((tm, d), lambda i: (i, 0)),
            pl.BlockSpec((rmax, d), lambda i: (i, 0)),
            pl.BlockSpec((tm, LANES), lambda i: (i, 0)),
            pl.BlockSpec((1, 1, LANES), lambda i: (i, 0, 0)),
        ],
        out_shape=[
            jax.ShapeDtypeStruct((t, d), F32),
            jax.ShapeDtypeStruct((nt * rmax, d), BF16),
            jax.ShapeDtypeStruct((t, LANES), F32),
            jax.ShapeDtypeStruct((nt, 1, LANES), F32),
        ],
        compiler_params=_params(("parallel",)),
        name="merge_router",
    )(*za, *zb, *proj, *proj, *x, mod3, wa, wb, wo, norm_w.reshape(1, d), rw, rb)


def _ffn_kernel(te_ref, nu_ref, src_ref, xl_ref, wgu_ref, bgu_ref, wd_ref, bd_ref, yl_ref,
                xbuf, ybuf, gsem, ssem, wgu_scr, wd_scr, *, dff, te, empty_chunk):
    j = pl.program_id(0)
    nu = nu_ref[0]
    cpt = te // ROW_ALIGN
    slot = j % 2

    def chunk(ci):
        return pl.ds(pl.multiple_of(ci * ROW_ALIGN, ROW_ALIGN), ROW_ALIGN)

    def gather(tile, buf, start):
        def body(ci, carry):
            cp = pltpu.make_async_copy(xl_ref.at[chunk(src_ref[tile * cpt + ci])],
                                       xbuf.at[buf, chunk(ci)], gsem.at[buf])
            cp.start() if start else cp.wait()
            return carry
        lax.fori_loop(0, cpt, body, 0)

    def scatter(tile, buf, start):
        def body(ci, carry):
            src = src_ref[tile * cpt + ci]

            @pl.when(src != empty_chunk)
            def _():
                cp = pltpu.make_async_copy(ybuf.at[buf, chunk(ci)], yl_ref.at[chunk(src)], ssem.at[buf])
                cp.start() if start else cp.wait()
            return carry
        lax.fori_loop(0, cpt, body, 0)

    @pl.when(j < nu)
    def _():
        @pl.when(j == 0)
        def _():
            gather(0, 0, True)

        @pl.when(j + 1 < nu)
        def _():
            gather(j + 1, 1 - slot, True)

        gather(j, slot, False)

        @pl.when(j >= 2)
        def _():
            scatter(j - 2, slot, False)

        @pl.when(jnp.logical_or(j == 0, te_ref[j] != te_ref[jnp.maximum(j - 1, 0)]))
        def _():
            wgu_scr[...] = wgu_ref[0].astype(BF16)
            wd_scr[...] = wd_ref[0].astype(BF16)

        hgu = _dot(xbuf[slot], wgu_scr[...]) + bgu_ref[0]
        gate = jnp.minimum(hgu[:, :dff], SWIGLU_LIMIT)
        up = jnp.clip(hgu[:, dff:], -SWIGLU_LIMIT, SWIGLU_LIMIT)
        act = gate * jax.nn.sigmoid(SWIGLU_ALPHA * gate) * (up + 1.0)
        ybuf[slot] = (_dot(act.astype(BF16), wd_scr[...]) + bd_ref[0]).astype(BF16)
        scatter(j, slot, True)

        @pl.when(j == nu - 1)
        def _():
            @pl.when(j >= 1)
            def _():
                scatter(j - 1, 1 - slot, False)
            scatter(j, slot, False)


def _expert_ffn(xl, tile_expert, n_used, src_chunk, w_gate_up, b_gate_up, w_down, b_down, layer, te,
                empty_chunk):
    _, d = xl.shape
    depth, n_exp, _, two_ff = w_gate_up.shape
    ne = depth * n_exp
    dff = two_ff // 2
    wsel = lambda j, tex, nu, src: (layer * n_exp + tex[j], 0, 0)
    return pl.pallas_call(
        functools.partial(_ffn_kernel, dff=dff, te=te, empty_chunk=empty_chunk),
        grid_spec=pltpu.PrefetchScalarGridSpec(
            num_scalar_prefetch=3,
            grid=(tile_expert.shape[0],),
            in_specs=[
                pl.BlockSpec(memory_space=pl.ANY),
                pl.BlockSpec((1, d, two_ff), wsel),
                pl.BlockSpec((1, 1, two_ff), wsel),
                pl.BlockSpec((1, dff, d), wsel),
                pl.BlockSpec((1, 1, d), wsel),
            ],
            out_specs=pl.BlockSpec(memory_space=pl.ANY),
            scratch_shapes=[
                pltpu.VMEM((2, te, d), BF16), pltpu.VMEM((2, te, d), BF16),
                pltpu.SemaphoreType.DMA((2,)), pltpu.SemaphoreType.DMA((2,)),
                pltpu.VMEM((d, two_ff), BF16), pltpu.VMEM((dff, d), BF16),
            ],
        ),
        out_shape=jax.ShapeDtypeStruct(xl.shape, BF16),
        input_output_aliases={3: 0},
        compiler_params=_params(("arbitrary",)),
        name="moe_ffn",
    )(tile_expert, n_used, src_chunk, xl, w_gate_up.reshape(ne, d, two_ff), b_gate_up.reshape(ne, 1, two_ff),
      w_down.reshape(ne, dff, d), b_down.reshape(ne, 1, d))


def _comb_kernel(yl_ref, route_ref, x1_ref, mod_ref, fw_ref, o_ref, *, d, tm, rmax, final_norm):
    route = route_ref[...]
    r_iota = lax.broadcasted_iota(jnp.int32, (1, rmax), 1).astype(F32)
    wm = jnp.zeros((tm, rmax), F32)
    for k in range(TOP_K):
        wm = wm + jnp.where(r_iota == route[:, k:k + 1], route[:, 2 * TOP_K + k:2 * TOP_K + k + 1], 0.0)
    out = _dot(wm.astype(BF16), yl_ref[...])
    x2 = x1_ref[...] + mod_ref[0, :, 5 * d:6 * d] * out
    if final_norm:
        x2 = x2 * lax.rsqrt(jnp.mean(x2 * x2, axis=-1, keepdims=True) + EPS) * fw_ref[...]
    o_ref[...] = x2


def _combine(yl, route, x1, mod3, mod_row, final_w, tile0, n_tiles, tm, final_norm):
    d = x1.shape[1]
    rmax = _sorted_rows(tm)
    return pl.pallas_call(
        functools.partial(_comb_kernel, d=d, tm=tm, rmax=rmax, final_norm=final_norm),
        grid=(n_tiles,),
        in_specs=[
            pl.BlockSpec((rmax, d), lambda i: (tile0 + i, 0)),
            pl.BlockSpec((tm, LANES), lambda i: (tile0 + i, 0)),
            pl.BlockSpec((tm, d), lambda i: (tile0 + i, 0)),
            pl.BlockSpec((1, 1, 6 * d), lambda i: (mod_row(tile0 + i), 0, 0)),
            pl.BlockSpec((1, d), lambda i: (0, 0)),
        ],
        out_specs=pl.BlockSpec((tm, d), lambda i: (i, 0)),
        out_shape=jax.ShapeDtypeStruct((n_tiles * tm, d), F32),
        compiler_params=_params(("parallel",)),
        name="moe_combine",
    )(yl, route, x1, mod3, final_w.reshape(1, d))


@functools.lru_cache(maxsize=None)
def _dft_mats(l):
    f = np.arange(l, dtype=np.int64)
    ang = ((f[:, None] * f[None, :]) % (2 * l)).astype(np.float64) * (math.pi / l)
    return np.cos(ang).astype(np.float32), np.sin(ang).astype(np.float32)


@functools.lru_cache(maxsize=None)
def _filter_features(l):
    f32 = np.float32
    t = np.linspace(0.0, 1.0, l, dtype=f32)[:, None]
    bands = (HY_EMB - 1) // 2
    w = f32(2.0 * math.pi) * np.arange(l, dtype=f32)[:, None] / f32(l)
    f = np.linspace(1e-4, bands - 1, bands, dtype=f32)[None, :]
    z = np.concatenate([t, np.cos(f * w), -np.sin(f * w)], axis=-1).astype(f32)
    return np.pad(z, ((0, 0), (0, LANES - HY_EMB))), t


@functools.lru_cache(maxsize=None)
def _rope_tables(rows):
    f32 = np.float32
    row = np.repeat(np.arange(rows, dtype=f32), GRID_W)
    col = np.tile(np.arange(GRID_W, dtype=f32), rows)
    half = RET_DK // 4
    inv = (f32(ROPE_BASE) ** (-np.arange(half, dtype=f32) / f32(half))).astype(f32)
    ang = np.concatenate([row[:, None] * inv, col[:, None] * inv], axis=-1)
    ang = np.concatenate([ang, ang], axis=-1).astype(f32)
    sign = np.where(np.arange(RET_DK) < RET_DK // 2, -1.0, 1.0).astype(f32)
    return np.cos(ang), np.sin(ang) * sign


@functools.lru_cache(maxsize=None)
def _decay_rates():
    return np.linspace(math.log(HY_TARGET) / HY_SLOW_PCT, math.log(HY_TARGET) / HY_FAST_PCT,
                       HY_W, dtype=np.float32)[None, :]


def _pad2(a, rows, cols):
    return jnp.pad(a, ((0, rows - a.shape[0]), (0, cols - a.shape[1])))


def _ffn_plan(seg, rmax, te, n_ffn_tiles):
    nt, ne = seg.shape
    cum_t = jnp.cumsum(seg, axis=0)
    tot = cum_t[-1]
    padded = ((tot + te - 1) // te) * te
    ends = jnp.cumsum(padded)
    offs = ends - padded
    n_used = ends[-1] // te
    expert_of = lambda rows: jnp.minimum(jnp.sum(ends[None, :] <= rows[:, None], axis=1), ne - 1)
    tiles = jnp.arange(n_ffn_tiles, dtype=jnp.int32)
    tile_expert = expert_of(jnp.minimum(tiles, n_used - 1) * te).astype(jnp.int32)

    rows = jnp.arange(n_ffn_tiles * te // ROW_ALIGN, dtype=jnp.int32) * ROW_ALIGN
    hot_e = expert_of(rows)[:, None] == jnp.arange(ne)[None, :]
    pick_e = lambda tab: jnp.sum(jnp.where(hot_e[:, None, :], tab[None], 0), axis=2)
    r_s = rows - pick_e(offs[None, :])[:, 0]
    cum_sel = pick_e(cum_t)
    i_s = jnp.minimum(jnp.sum(cum_sel <= r_s[:, None], axis=1), nt - 1)
    hot_i = i_s[:, None] == jnp.arange(nt)[None, :]
    seg_base = (jnp.arange(nt)[:, None] * rmax + jnp.cumsum(seg, axis=1) - seg) - (cum_t - seg)
    src_row = r_s + jnp.sum(jnp.where(hot_i, pick_e(seg_base), 0), axis=1)
    valid = (r_s < pick_e(tot[None, :])[:, 0]) & (rows < ends[-1])
    src_chunk = jnp.where(valid, src_row // ROW_ALIGN, rmax // ROW_ALIGN - 1).astype(jnp.int32)
    return tile_expert, n_used.reshape(1).astype(jnp.int32), src_chunk


def kernel(x_prompt, x_sample, state_ret, c, c_ctx, ada_w, ada_b, norm_mix_w, w_in, conv_w, conv_b,
           f_w1, f_b1, f_w2, f_b2, f_w3, f_b3, f_freq, f_wout, hy_skip, ret_decay_logit,
           w_branch_a, w_branch_b, w_out, norm_ffn_w, router_w, router_b, w_gate_up, b_gate_up,
           w_down, b_down, final_norm_w):
    bp, lp, d = x_prompt.shape
    bs, ls, _ = x_sample.shape
    depth = ada_w.shape[0]
    tp, ts = bp * lp, bs * ls
    t = tp + ts
    assert ls % lp == 0 and tp % ls == 0 and lp % RET_CHUNK == 0 and ls % GRID_W == 0
    tm = 512 if (tp % 512 == 0 and ls % 512 == 0) else lp
    te = 512
    nt = t // tm
    rmax = _sorted_rows(tm)
    empty_chunk = rmax // ROW_ALIGN - 1
    n_ffn_tiles = -(-(t * TOP_K + nt * N_EXPERTS * (ROW_ALIGN - 1)) // te) + N_EXPERTS

    dft = {l: tuple(jnp.asarray(a).astype(BF16) for a in _dft_mats(l)) for l in (lp, ls)}
    feats = {l: tuple(jnp.asarray(a) for a in _filter_features(l)) for l in (lp, ls)}
    rope_tabs = tuple(jnp.asarray(a) for a in _rope_tables(ls // GRID_W))
    deltas = jnp.asarray(_decay_rates())

    cvec = jnp.zeros((8, d), F32).at[:bs].set(c).at[bs].set(c_ctx)
    mod3 = _modulation(cvec, ada_w, ada_b).reshape(depth * 8, 1, 6 * d)
    n_p, n_s = tp // tm, ts // tm

    x = (x_prompt.reshape(tp, d), x_sample.reshape(ts, d))
    states = []
    for l in range(depth):
        mod_row = lambda i, l=l: l * 8 + jnp.where(i < n_p, bs, jnp.maximum(i - n_p, 0) // (ls // tm))
        w_in_bf = w_in[l].astype(BF16)
        proj = (_in_proj(x[0], norm_mix_w[l], mod3, mod_row, w_in_bf, tm),
                _in_proj(x[1], norm_mix_w[l], mod3, lambda i, f=mod_row: f(i + n_p), w_in_bf, tm))

        w1 = _pad2(f_w1[l], LANES, LANES)
        w2 = _pad2(f_w2[l], LANES, LANES)
        w3 = _pad2(f_w3[l], LANES, LANES)
        wo = _pad2(f_wout[l], LANES, f_wout.shape[2])
        b1, b2, b3, fr = (_pad2(a[l][None, :], 1, LANES) for a in (f_b1, f_b2, f_b3, f_freq))
        za, zb = [], []
        for grp, (nb, ln) in enumerate(((bp, lp), (bs, ls))):
            cm, sm = dft[ln]
            taps, sums = _hyena_taps(ln, feats[ln][0], feats[ln][1], deltas, w1, b1, w2, b2, w3, b3, fr, wo)
            kc, ki, kn = _filter_spectra(ln, cm, sm, taps, sums)
            za.append(_hyena_branch(proj[grp], 0, nb, ln, cm, sm, conv_w[l], conv_b[l], kc, ki, kn, hy_skip[l]))
            zb_g, st = _retention_branch(proj[grp], 0, nb, ln, ret_decay_logit[l],
                                         rope_tabs if grp == 1 else None,
                                         state_ret if grp == 1 else None, l)
            zb.append(zb_g)
            if grp == 0:
                states.append(st)

        rw = _pad2(router_w[l], d, LANES)
        rb = jnp.full((1, LANES), -1e30, F32).at[0, :N_EXPERTS].set(router_b[l])
        x1, xl, route, seg = _mix(za, zb, proj, x, mod3, mod_row,
                                  w_branch_a[l].astype(BF16), w_branch_b[l].astype(BF16),
                                  w_out[l].astype(BF16), norm_ffn_w[l], rw, rb, tm)
        tile_expert, n_used, src_chunk = _ffn_plan(seg[:, 0, :N_EXPERTS].astype(jnp.int32), rmax, te, n_ffn_tiles)
        yl = _expert_ffn(xl, tile_expert, n_used, src_chunk, w_gate_up, b_gate_up, w_down, b_down, l, te,
                         empty_chunk)
        last = l == depth - 1
        x = (_combine(yl, route, x1, mod3, mod_row, final_norm_w, 0, n_p, tm, last),
             _combine(yl, route, x1, mod3, mod_row, final_norm_w, n_p, n_s, tm, last))

    state_ret_new = jnp.stack(states, axis=1)
    return (x[0].reshape(bp, lp, d), x[1].reshape(bs, ls, d), state_ret_new)
```

```python
import functools
import math

import jax
import jax.numpy as jnp
import numpy as np
from jax import lax
from jax.experimental import pallas as pl
from jax.experimental.pallas import tpu as pltpu

F32 = jnp.float32
BF16 = jnp.bfloat16

GRID_W = 64
HY_W = 512
HY_ORDER = 2
HY_EMB = 33
HY_FAST_PCT = 0.3
HY_SLOW_PCT = 1.5
HY_TARGET = 1e-2
N_RET_HEADS = 4
RET_DK = 128
RET_DV = 256
RET_CHUNK = 128
ROPE_BASE = 10000.0
N_EXPERTS = 32
TOP_K = 4
SWIGLU_LIMIT = 7.0
SWIGLU_ALPHA = 1.702
EPS = 1e-6

LANES = 128
VMEM_LIMIT = 56 * 1024 * 1024

COL_V, COL_G, COL_GA, COL_GB = 0, 1024, 2048, 3072
COL_HV, COL_HX1, COL_HX2, COL_Q, COL_K = 4096, 4608, 5120, 5632, 6144
IN_W = 6656
IN_COL_BLOCKS = (5, 6, 7, 8, 9, 10, 11, 12, 0, 1, 2, 3, 4)


def _dot(a, b):
    return jnp.dot(a, b, preferred_element_type=F32)


def _split_bf16(a):
    hi = a.astype(BF16)
    lo = (a - hi.astype(F32)).astype(BF16)
    return hi, lo


def _dot3(a, b):
    ah, al = _split_bf16(a)
    bh, bl = _split_bf16(b)
    return _dot(ah, bh) + (_dot(ah, bl) + _dot(al, bh))


def _silu(x):
    return x * jax.nn.sigmoid(x)


def _params(sem, vmem=VMEM_LIMIT):
    return pltpu.CompilerParams(dimension_semantics=sem, vmem_limit_bytes=vmem)


def _resident(shape, index_map):
    return pl.BlockSpec(shape, index_map, pipeline_mode=pl.Buffered(1))


def _mod_kernel(c_ref, w_ref, b_ref, o_ref):
    o_ref[0] = _dot3(_silu(c_ref[...]), w_ref[0]) + b_ref[0]


def _modulation(cvec, ada_w, ada_b):
    depth, d, six_d = ada_w.shape
    tn = six_d // 4
    return pl.pallas_call(
        _mod_kernel,
        grid=(depth, six_d // tn),
        in_specs=[
            pl.BlockSpec((8, d), lambda l, j: (0, 0)),
            pl.BlockSpec((1, d, tn), lambda l, j: (l, 0, j)),
            pl.BlockSpec((1, 1, tn), lambda l, j: (l, 0, j)),
        ],
        out_specs=pl.BlockSpec((1, 8, tn), lambda l, j: (l, 0, j)),
        out_shape=jax.ShapeDtypeStruct((depth, 8, six_d), F32),
        compiler_params=_params(("parallel", "parallel")),
        name="adaln_mod",
    )(cvec, ada_w, ada_b.reshape(depth, 1, six_d))


def _in_kernel(x_ref, nw_ref, mod_ref, w_ref, o_ref, *, d, cw):
    x = x_ref[...]
    y = x * lax.rsqrt(jnp.mean(x * x, axis=-1, keepdims=True) + EPS) * nw_ref[...]
    shift = mod_ref[0, :, 0:d]
    scale = mod_ref[0, :, d:2 * d]
    u = (y * (1.0 + scale) + shift).astype(BF16)
    for j, src in enumerate(IN_COL_BLOCKS):
        o_ref[:, j * cw:(j + 1) * cw] = _dot(u, w_ref[:, src * cw:(src + 1) * cw]).astype(BF16)


def _in_proj(x, norm_w, mod3, mod_row, w_in_bf, tm):
    t, d = x.shape
    return pl.pallas_call(
        functools.partial(_in_kernel, d=d, cw=IN_W // len(IN_COL_BLOCKS)),
        grid=(t // tm,),
        in_specs=[
            pl.BlockSpec((tm, d), lambda i: (i, 0)),
            pl.BlockSpec((1, d), lambda i: (0, 0)),
            pl.BlockSpec((1, 1, 2 * d), lambda i: (mod_row(i), 0, 0)),
            _resident((d, IN_W), lambda i: (0, 0)),
        ],
        out_specs=pl.BlockSpec((tm, IN_W), lambda i: (i, 0)),
        out_shape=jax.ShapeDtypeStruct((t, IN_W), BF16),
        compiler_params=_params(("parallel",)),
        name="in_proj",
    )(x, norm_w.reshape(1, d), mod3, w_in_bf)


def _filt_kernel(z_ref, t_ref, w1, b1, w2, b2, w3, b3, fr, wo, dl_ref, h_ref, s_ref, *, tl):
    i = pl.program_id(0)
    freq = fr[...]
    h = jnp.sin(freq * (_dot3(z_ref[...], w1[...]) + b1[...]))
    h = jnp.sin(freq * (_dot3(h, w2[...]) + b2[...]))
    h = jnp.sin(freq * (_dot3(h, w3[...]) + b3[...]))
    h = _dot3(h, wo[...])
    decay = jnp.exp(-t_ref[...] * jnp.abs(dl_ref[...]))
    h = h * jnp.concatenate([decay] * (2 * HY_ORDER), axis=1)
    h_ref[...] = h
    row = i * tl + lax.broadcasted_iota(jnp.int32, (tl, 1), 0)
    col = lax.broadcasted_iota(jnp.int32, (1, 2 * HY_ORDER * HY_W), 1)
    is_bwd = ((col // HY_W) % 2) == 1
    part = jnp.sum(jnp.where((row == 0) & is_bwd, 0.0, jnp.abs(h)), axis=0, keepdims=True)

    @pl.when(i == 0)
    def _():
        s_ref[...] = jnp.zeros_like(s_ref)

    s_ref[...] += part


def _hyena_taps(l, zfeat, tcol, deltas, w1, b1, w2, b2, w3, b3, freq, wout):
    tl = min(l, 256)
    nw = 2 * HY_ORDER * HY_W
    full = lambda shape: pl.BlockSpec(shape, lambda i: (0,) * len(shape))
    return pl.pallas_call(
        functools.partial(_filt_kernel, tl=tl),
        grid=(l // tl,),
        in_specs=[
            pl.BlockSpec((tl, LANES), lambda i: (i, 0)),
            pl.BlockSpec((tl, 1), lambda i: (i, 0)),
            full((LANES, LANES)), full((1, LANES)),
            full((LANES, LANES)), full((1, LANES)),
            full((LANES, LANES)), full((1, LANES)),
            full((1, LANES)),
            full((LANES, nw)),
            full((1, HY_W)),
        ],
        out_specs=[pl.BlockSpec((tl, nw), lambda i: (i, 0)), pl.BlockSpec((1, nw), lambda i: (0, 0))],
        out_shape=[jax.ShapeDtypeStruct((l, nw), F32), jax.ShapeDtypeStruct((1, nw), F32)],
        compiler_params=_params(("arbitrary",)),
        name=f"hyena_taps_{l}",
    )(zfeat, tcol, w1, b1, w2, b2, w3, b3, freq, wout, deltas)


def _fdft_kernel(c_ref, s_ref, hf_ref, hb_ref, sf_ref, sb_ref, kc_ref, ki_ref, kn_ref, *, l):
    row = lax.broadcasted_iota(jnp.int32, (l, 1), 0)
    hf = hf_ref[...]
    hb = jnp.where(row == 0, 0.0, hb_ref[...])
    inv = 1.0 / (sf_ref[...] + sb_ref[...])
    a = hf + hb
    b = hb - hf
    wf = jnp.where(row == 0, 0.5 / l, 1.0 / l)
    kc_ref[...] = _dot(c_ref[...], a.astype(BF16)) * wf * inv
    ki_ref[...] = _dot(s_ref[...], b.astype(BF16)) * wf * inv
    sgn = jnp.where(row % 2 == 0, 1.0, -1.0)
    kn_ref[...] = jnp.sum(a * sgn, axis=0, keepdims=True) * inv * (0.5 / l)


def _filter_spectra(l, cmat, smat, taps, sums):
    wt = 256
    nb = HY_W // wt
    two_w = 2 * HY_W
    return pl.pallas_call(
        functools.partial(_fdft_kernel, l=l),
        grid=(HY_ORDER, nb),
        in_specs=[
            _resident((l, l), lambda o, j: (0, 0)),
            _resident((l, l), lambda o, j: (0, 0)),
            pl.BlockSpec((l, wt), lambda o, j: (0, o * 2 * nb + j)),
            pl.BlockSpec((l, wt), lambda o, j: (0, o * 2 * nb + nb + j)),
            pl.BlockSpec((1, wt), lambda o, j: (0, o * 2 * nb + j)),
            pl.BlockSpec((1, wt), lambda o, j: (0, o * 2 * nb + nb + j)),
        ],
        out_specs=[
            pl.BlockSpec((l, wt), lambda o, j: (0, o * nb + j)),
            pl.BlockSpec((l, wt), lambda o, j: (0, o * nb + j)),
            pl.BlockSpec((1, wt), lambda o, j: (0, o * nb + j)),
        ],
        out_shape=[
            jax.ShapeDtypeStruct((l, HY_ORDER * HY_W), F32),
            jax.ShapeDtypeStruct((l, HY_ORDER * HY_W), F32),
            jax.ShapeDtypeStruct((1, HY_ORDER * HY_W), F32),
        ],
        compiler_params=_params(("parallel", "parallel")),
        name=f"hyena_spectra_{l}",
    )(cmat, smat, taps, taps, sums, sums)


def _hy_kernel(c_ref, s_ref, hv_ref, hx1_ref, hx2_ref, cw_ref, cb_ref,
               kc0, ki0, kc1, ki1, kn_ref, sk_ref, o_ref, z_scr, zb_scr, p_scr, q_scr, *, l, rc, sb):
    row = lax.broadcasted_iota(jnp.int32, (l, 1), 0)
    first = row == 0
    last = row == l - 1
    sgn = jnp.where(row % 2 == 0, 1.0, -1.0)
    cw = cw_ref[...]
    cb = cb_ref[...]

    def short_conv(ref, rows, part):
        x = ref[rows, :].astype(F32)
        xm = jnp.where(first, 0.0, pltpu.roll(x, 1, 0))
        xp = jnp.where(last, 0.0, pltpu.roll(x, l - 1, 0))
        return (xm * cw[0, part:part + 1] + x * cw[1, part:part + 1] + xp * cw[2, part:part + 1]
                + cb[part:part + 1])

    for s in range(sb):
        rows = slice(s * l, (s + 1) * l)
        z_scr[s] = short_conv(hv_ref, rows, 0)
        for o, (gate_ref, kc, ki) in enumerate(((hx1_ref, kc0, ki0), (hx2_ref, kc1, ki1))):
            z = z_scr[s]
            zb_scr[s] = z.astype(BF16)
            nyq = jnp.sum(z * sgn, axis=0, keepdims=True) * kn_ref[o:o + 1]
            for r in range(l // rc):
                sl = slice(r * rc, (r + 1) * rc)
                zc = _dot(c_ref[sl, :], zb_scr[s])
                zs = _dot(s_ref[sl, :], zb_scr[s])
                p_scr[s, sl, :] = (zc * kc[sl, :] + zs * ki[sl, :]).astype(BF16)
                q_scr[s, sl, :] = (zs * kc[sl, :] - zc * ki[sl, :]).astype(BF16)
            gate = short_conv(gate_ref, rows, o + 1)
            for r in range(l // rc):
                sl = slice(r * rc, (r + 1) * rc)
                y = _dot(c_ref[sl, :], p_scr[s]) + _dot(s_ref[sl, :], q_scr[s])
                y = y + sgn[sl] * nyq + z_scr[s, sl, :] * sk_ref[o:o + 1]
                z_scr[s, sl, :] = gate[sl] * y
        o_ref[rows, :] = z_scr[s].astype(BF16)


def _hyena_branch(proj, row_blk0, nb, l, cmat, smat, conv_w, conv_b, kc, ki, kn, skip):
    wt = 256
    nw = HY_W // wt
    rc = min(l, 512)
    sb = math.gcd(nb, max(1, 2048 // l))
    assert row_blk0 % sb == 0
    hv0, hx10, hx20 = COL_HV // wt, COL_HX1 // wt, COL_HX2 // wt
    seq = lambda c0: pl.BlockSpec((sb * l, wt), lambda j, b: (row_blk0 // sb + b, c0 + j))
    tab = lambda o: pl.BlockSpec((l, wt), lambda j, b: (0, o * nw + j))
    return pl.pallas_call(
        functools.partial(_hy_kernel, l=l, rc=rc, sb=sb),
        grid=(nw, nb // sb),
        in_specs=[
            _resident((l, l), lambda j, b: (0, 0)),
            _resident((l, l), lambda j, b: (0, 0)),
            seq(hv0), seq(hx10), seq(hx20),
            pl.BlockSpec((3, 3, wt), lambda j, b: (0, 0, j)),
            pl.BlockSpec((3, wt), lambda j, b: (0, j)),
            tab(0), tab(0), tab(1), tab(1),
            pl.BlockSpec((HY_ORDER, wt), lambda j, b: (0, j)),
            pl.BlockSpec((HY_ORDER, wt), lambda j, b: (0, j)),
        ],
        out_specs=pl.BlockSpec((sb * l, wt), lambda j, b: (b, j)),
        out_shape=jax.ShapeDtypeStruct((nb * l, HY_W), BF16),
        scratch_shapes=[
            pltpu.VMEM((sb, l, wt), F32), pltpu.VMEM((sb, l, wt), BF16),
            pltpu.VMEM((sb, l, wt), BF16), pltpu.VMEM((sb, l, wt), BF16),
        ],
        compiler_params=_params(("parallel", "parallel")),
        name=f"hyena_{l}",
    )(cmat, smat, proj, proj, proj, conv_w.reshape(3, 3, HY_W), conv_b.reshape(3, HY_W),
      kc, ki, kc, ki, kn.reshape(HY_ORDER, HY_W), skip)


def _log_sigmoid(x):
    return jnp.minimum(x, 0.0) - jnp.log1p(jnp.exp(-jnp.abs(x)))


def _ret_kernel(*refs, l, hb, use_rope, has_s0):
    it = iter(refs)
    dl_ref, q_ref, k_ref, v_ref, g_ref = (next(it) for _ in range(5))
    cos_ref, sin_ref = (next(it), next(it)) if use_rope else (None, None)
    s0_ref = next(it) if has_s0 else None
    o_ref, st_ref, kr_scr, kv_scr = (next(it) for _ in range(4))

    c, dk, dv = RET_CHUNK, RET_DK, RET_DV
    nc = l // c
    unroll = min(nc, 4)
    pos = lax.broadcasted_iota(jnp.int32, (c, 1), 0).astype(F32)
    diff = pos - lax.broadcasted_iota(jnp.int32, (1, c), 1).astype(F32)

    def rope(x, r0):
        if not use_rope:
            return x
        return x * cos_ref[pl.ds(r0, c), :] + pltpu.roll(x, dk // 2, 1) * sin_ref[pl.ds(r0, c), :]

    for h in range(hb):
        qs = slice(h * dk, (h + 1) * dk)
        vs = slice(h * dv, (h + 1) * dv)
        lgf = _log_sigmoid(dl_ref[0, h])
        lgb = _log_sigmoid(dl_ref[1, h])
        dm = jnp.where(diff >= 0, jnp.exp(lgf * diff), jnp.exp(-lgb * diff))
        zeta_f = jnp.exp(lgf * (c - 1.0 - pos))
        zeta_b = jnp.exp(lgb * pos)
        xi_f = jnp.exp(lgf * (pos + 1.0))
        xi_b = jnp.exp(lgb * (c - pos))
        gc_f = jnp.exp(lgf * c)
        gc_b = jnp.exp(lgb * c)

        def kv_step(ci, carry):
            r0 = pl.multiple_of(ci * c, c)
            k = rope(k_ref[pl.ds(r0, c), qs].astype(F32), r0) * (dk ** -0.5)
            kr_scr[pl.ds(r0, c), qs] = k.astype(BF16)
            kk = jnp.concatenate([k * zeta_f, k * zeta_b], axis=1).astype(BF16)
            kv_scr[h, ci] = lax.dot_general(kk, v_ref[pl.ds(r0, c), vs], (((0,), (0,)), ((), ())),
                                            preferred_element_type=F32)
            return carry

        lax.fori_loop(0, nc, kv_step, 0, unroll=unroll)

        def fwd_scan(ci, s):
            upd = kv_scr[h, ci, 0:dk, :]
            kv_scr[h, ci, 0:dk, :] = s
            return gc_f * s + upd

        def bwd_scan(i, r):
            ci = nc - 1 - i
            upd = kv_scr[h, ci, dk:2 * dk, :]
            kv_scr[h, ci, dk:2 * dk, :] = r
            return gc_b * r + upd

        zero = jnp.zeros((dk, dv), F32)
        st_ref[0, 0, h] = lax.fori_loop(0, nc, fwd_scan, s0_ref[0, 0, 0, h] if has_s0 else zero)
        st_ref[0, 1, h] = lax.fori_loop(0, nc, bwd_scan, s0_ref[0, 0, 1, h] if has_s0 else zero)

        def out_step(ci, carry):
            r0 = pl.multiple_of(ci * c, c)
            q = rope(q_ref[pl.ds(r0, c), qs].astype(F32), r0)
            sc = lax.dot_general(q.astype(BF16), kr_scr[pl.ds(r0, c), qs], (((1,), (1,)), ((), ())),
                                 preferred_element_type=F32) * dm
            qq = jnp.concatenate([q * xi_f, q * xi_b], axis=1).astype(BF16)
            o = _dot(sc.astype(BF16), v_ref[pl.ds(r0, c), vs]) + _dot(qq, kv_scr[h, ci].astype(BF16))
            o = o * lax.rsqrt(jnp.mean(o * o, axis=-1, keepdims=True) + EPS)
            o_ref[pl.ds(r0, c), vs] = (_silu(g_ref[pl.ds(r0, c), vs].astype(F32)) * o).astype(BF16)
            return carry

        lax.fori_loop(0, nc, out_step, 0, unroll=unroll)


def _retention_branch(proj, row_blk0, nb, l, decay_logit, rope_tabs, state, layer):
    nh, dk, dv = N_RET_HEADS, RET_DK, RET_DV
    use_rope = rope_tabs is not None
    has_s0 = state is not None
    hb = nh if l <= 512 else 1
    ng = nh // hb
    seq = lambda col, w: pl.BlockSpec((l, hb * w), lambda b, g: (row_blk0 + b, col // (hb * w) + g))
    in_specs = [
        pl.BlockSpec((2, hb, 1, 1), lambda b, g: (0, g, 0, 0)),
        seq(COL_Q, dk), seq(COL_K, dk), seq(COL_V, dv), seq(COL_G, dv),
    ]
    args = [decay_logit.reshape(2, nh, 1, 1), proj, proj, proj, proj]
    if use_rope:
        in_specs += [pl.BlockSpec((l, dk), lambda b, g: (0, 0))] * 2
        args += list(rope_tabs)
    if has_s0:
        in_specs.append(pl.BlockSpec((1, 1, 2, hb, dk, dv), lambda b, g: (b, layer, 0, g, 0, 0)))
        args.append(state)
    return pl.pallas_call(
        functools.partial(_ret_kernel, l=l, hb=hb, use_rope=use_rope, has_s0=has_s0),
        grid=(nb, ng),
        in_specs=in_specs,
        out_specs=[
            pl.BlockSpec((l, hb * dv), lambda b, g: (b, g)),
            pl.BlockSpec((1, 2, hb, dk, dv), lambda b, g: (b, 0, g, 0, 0)),
        ],
        out_shape=[
            jax.ShapeDtypeStruct((nb * l, nh * dv), BF16),
            jax.ShapeDtypeStruct((nb, 2, nh, dk, dv), F32),
        ],
        scratch_shapes=[
            pltpu.VMEM((l, hb * dk), BF16),
            pltpu.VMEM((hb, l // RET_CHUNK, 2 * dk, dv), F32),
        ],
        compiler_params=_params(("parallel", "parallel")),
        name=f"retention_{l}",
    )(*args)


ROW_ALIGN = 16


def _sorted_rows(tm):
    worst = TOP_K * tm + N_EXPERTS * (ROW_ALIGN - 1) + ROW_ALIGN
    return -(-worst // LANES) * LANES


def _mix_kernel(zap_ref, zas_ref, zbp_ref, zbs_ref, gap_ref, gas_ref, gbp_ref, gbs_ref, xp_ref, xs_ref,
                mod_ref, wa_ref, wb_ref, wo_ref, nw_ref, rw_ref, rb_ref, x1_ref, xl_ref, route_ref, seg_ref,
                h_scr, pr_scr, sg_scr, *, d, tm, rmax, n_tiles, n_prompt_tiles):
    i = pl.program_id(0)

    @pl.when(i == 0)
    def _():
        h_scr[...] = jnp.zeros_like(h_scr)
        pr_scr[...] = jnp.zeros_like(pr_scr)
        sg_scr[...] = jnp.zeros_like(sg_scr)

    @pl.when(i > n_tiles)
    def _():
        xl_ref[...] = jnp.zeros_like(xl_ref)

    @pl.when(i <= n_tiles)
    def _():
        _mix_step(zap_ref, zas_ref, zbp_ref, zbs_ref, gap_ref, gas_ref, gbp_ref, gbs_ref, xp_ref, xs_ref,
                  mod_ref, wa_ref, wb_ref, wo_ref, nw_ref, rw_ref, rb_ref, x1_ref, xl_ref, route_ref, seg_ref,
                  h_scr, pr_scr, sg_scr, d=d, tm=tm, rmax=rmax, n_tiles=n_tiles, n_prompt_tiles=n_prompt_tiles)


def _mix_step(zap_ref, zas_ref, zbp_ref, zbs_ref, gap_ref, gas_ref, gbp_ref, gbs_ref, xp_ref, xs_ref,
              mod_ref, wa_ref, wb_ref, wo_ref, nw_ref, rw_ref, rb_ref, x1_ref, xl_ref, route_ref, seg_ref,
              h_scr, pr_scr, sg_scr, *, d, tm, rmax, n_tiles, n_prompt_tiles):
    i = pl.program_id(0)
    r_col = lax.broadcasted_iota(jnp.int32, (rmax, 1), 0).astype(F32)
    in_seg = jnp.logical_and(r_col >= sg_scr[0:1], r_col < sg_scr[0:1] + sg_scr[1:2])
    seg_hot = jnp.where(in_seg, 1.0, 0.0).astype(BF16)
    row_of = _dot(jnp.concatenate([seg_hot, seg_hot], axis=1), pr_scr[...])
    sel = jnp.where(row_of == r_col + 1.0, 1.0, 0.0).astype(BF16)
    xl_ref[...] = _dot(sel, h_scr[...]).astype(BF16)

    is_prompt = jnp.minimum(i, n_tiles - 1) < n_prompt_tiles
    pick = lambda p_ref, s_ref: jnp.where(is_prompt, p_ref[...], s_ref[...])
    za, zb = pick(zap_ref, zas_ref), pick(zbp_ref, zbs_ref)
    ga, gb = pick(gap_ref, gas_ref), pick(gbp_ref, gbs_ref)
    merged = (jax.nn.sigmoid(ga.astype(F32)) * _dot(za, wa_ref[...])
              + jax.nn.sigmoid(gb.astype(F32)) * _dot(zb, wb_ref[...]))
    gate1 = mod_ref[0, :, 2 * d:3 * d]
    shift2 = mod_ref[0, :, 3 * d:4 * d]
    scale2 = mod_ref[0, :, 4 * d:5 * d]
    x1 = pick(xp_ref, xs_ref) + gate1 * _dot(merged.astype(BF16), wo_ref[...])
    x1_ref[...] = x1
    hn = x1 * lax.rsqrt(jnp.mean(x1 * x1, axis=-1, keepdims=True) + EPS) * nw_ref[...]
    hmo = hn * (1.0 + scale2) + shift2

    logits = _dot3(hmo, rw_ref[...]) + rb_ref[...]
    lane = lax.broadcasted_iota(jnp.int32, (1, LANES), 1)
    idxs, vals = [], []
    for _ in range(TOP_K):
        m = jnp.max(logits, axis=1, keepdims=True)
        ik = jnp.min(jnp.where(logits == m, lane, LANES), axis=1, keepdims=True)
        idxs.append(ik)
        vals.append(m)
        logits = jnp.where(lane == ik, -jnp.inf, logits)
    exps = [jnp.exp(v - vals[0]) for v in vals]
    denom = exps[0] + exps[1] + exps[2] + exps[3]

    hot = jnp.zeros((tm, LANES), F32)
    for ik in idxs:
        hot = hot + jnp.where(lane == ik, 1.0, 0.0)
    r_i = lax.broadcasted_iota(jnp.int32, (tm, tm), 0)
    c_i = lax.broadcasted_iota(jnp.int32, (tm, tm), 1)
    below = jnp.where(c_i < r_i, 1.0, 0.0).astype(BF16)
    before = _dot(below, hot.astype(BF16))
    seg = jnp.ceil(jnp.sum(hot, axis=0, keepdims=True) * (1.0 / ROW_ALIGN)) * ROW_ALIGN
    seg_ref[0] = seg
    e_r = lax.broadcasted_iota(jnp.int32, (LANES, LANES), 0)
    e_c = lax.broadcasted_iota(jnp.int32, (LANES, LANES), 1)
    seg_start = _dot3(jnp.broadcast_to(seg, (8, LANES)), jnp.where(e_r < e_c, 1.0, 0.0))[0:1]
    base = seg_start + before

    route = jnp.zeros((tm, LANES), F32)
    for k in range(TOP_K):
        srow = jnp.sum(jnp.where(lane == idxs[k], base, 0.0), axis=1, keepdims=True)
        route = route + jnp.where(lane == k, srow, 0.0)
        route = route + jnp.where(lane == TOP_K + k, idxs[k].astype(F32), 0.0)
        route = route + jnp.where(lane == 2 * TOP_K + k, exps[k] / denom, 0.0)
    route_ref[...] = route

    h_scr[...] = hmo.astype(BF16)
    pair_row = jnp.where(hot > 0.0, base + 1.0, 0.0).T
    pr_scr[...] = jnp.concatenate(_split_bf16(pair_row), axis=0)
    sg_scr[0:1] = seg_start
    sg_scr[1:2] = seg


def _mix(za, zb, proj, x, mod3, mod_row, wa, wb, wo, norm_w, rw, rb, tm, spare_rows):
    d = x[0].shape[1]
    n_p = x[0].shape[0] // tm
    n_s = x[1].shape[0] // tm
    nt = n_p + n_s
    t = nt * tm
    rmax = _sorted_rows(tm)
    spare_blocks = -(-spare_rows // rmax)
    vw = N_RET_HEADS * RET_DV
    cur = lambda i: jnp.minimum(i, nt - 1)
    prev = lambda i: jnp.maximum(i - 1, 0)
    pair = lambda w, col: [pl.BlockSpec((tm, w), lambda i: (jnp.minimum(cur(i), n_p - 1), col)),
                           pl.BlockSpec((tm, w), lambda i: (jnp.maximum(cur(i) - n_p, 0), col))]
    const = lambda i: (0, 0)
    return pl.pallas_call(
        functools.partial(_mix_kernel, d=d, tm=tm, rmax=rmax, n_tiles=nt, n_prompt_tiles=n_p),
        grid=(nt + 1 + spare_blocks,),
        in_specs=[
            *pair(HY_W, 0), *pair(vw, 0), *pair(d, COL_GA // d), *pair(d, COL_GB // d), *pair(d, 0),
            pl.BlockSpec((1, 1, 6 * d), lambda i: (mod_row(cur(i)), 0, 0)),
            _resident((HY_W, d), const), _resident((vw, d), const), _resident((d, d), const),
            pl.BlockSpec((1, d), const),
            _resident((d, LANES), const), pl.BlockSpec((1, LANES), const),
        ],
        out_specs=[
            pl.BlockSpec((tm, d), lambda i: (cur(i), 0)),
            pl.BlockSpec((rmax, d), lambda i: (prev(i), 0)),
            pl.BlockSpec((tm, LANES), lambda i: (cur(i), 0)),
            pl.BlockSpec((1, 1, LANES), lambda i: (cur(i), 0, 0)),
        ],
        out_shape=[
            jax.ShapeDtypeStruct((t, d), F32),
            jax.ShapeDtypeStruct(((nt + spare_blocks) * rmax, d), BF16),
            jax.ShapeDtypeStruct((t, LANES), F32),
            jax.ShapeDtypeStruct((nt, 1, LANES), F32),
        ],
        scratch_shapes=[pltpu.VMEM((tm, d), BF16), pltpu.VMEM((2 * LANES, tm), BF16),
                        pltpu.VMEM((2, LANES), F32)],
        compiler_params=_params(("arbitrary",)),
        name="merge_router",
    )(*za, *zb, *proj, *proj, *x, mod3, wa, wb, wo, norm_w.reshape(1, d), rw, rb)


def _ffn_kernel(te_ref, nu_ref, src_ref, dst_ref, xl_ref, wgu_ref, bgu_ref, wd_ref, bd_ref, yl_ref,
                xbuf, ybuf, gsem, ssem, wgu_scr, wd_scr, *, dff, te):
    j = pl.program_id(0)
    nu = nu_ref[0]
    cpt = te // ROW_ALIGN
    slot = j % 2

    def hbm_chunk(ci):
        return pl.ds(pl.multiple_of(ci * ROW_ALIGN, ROW_ALIGN), ROW_ALIGN)

    def start_gather(tile, buf):
        for ci in range(cpt):
            pltpu.make_async_copy(xl_ref.at[hbm_chunk(src_ref[tile * cpt + ci])],
                                  xbuf.at[buf, pl.ds(ci * ROW_ALIGN, ROW_ALIGN)], gsem.at[buf]).start()

    def wait_gather(buf):
        pltpu.make_async_copy(xl_ref.at[pl.ds(0, te)], xbuf.at[buf], gsem.at[buf]).wait()

    def start_scatter(tile, buf):
        for ci in range(cpt):
            pltpu.make_async_copy(ybuf.at[buf, pl.ds(ci * ROW_ALIGN, ROW_ALIGN)],
                                  yl_ref.at[hbm_chunk(dst_ref[tile * cpt + ci])], ssem.at[buf]).start()

    def wait_scatter(buf):
        pltpu.make_async_copy(ybuf.at[buf], yl_ref.at[pl.ds(0, te)], ssem.at[buf]).wait()

    @pl.when(j < nu)
    def _():
        @pl.when(j == 0)
        def _():
            start_gather(0, 0)

        start_gather(j + 1, 1 - slot)
        wait_gather(slot)

        @pl.when(j >= 2)
        def _():
            wait_scatter(slot)

        @pl.when(jnp.logical_or(j == 0, te_ref[j] != te_ref[jnp.maximum(j - 1, 0)]))
        def _():
            wgu_scr[...] = wgu_ref[0].astype(BF16)
            wd_scr[...] = wd_ref[0].astype(BF16)

        hgu = _dot(xbuf[slot], wgu_scr[...]) + bgu_ref[0]
        gate = jnp.minimum(hgu[:, :dff], SWIGLU_LIMIT)
        up = jnp.clip(hgu[:, dff:], -SWIGLU_LIMIT, SWIGLU_LIMIT)
        act = gate * jax.nn.sigmoid(SWIGLU_ALPHA * gate) * (up + 1.0)
        ybuf[slot] = (_dot(act.astype(BF16), wd_scr[...]) + bd_ref[0]).astype(BF16)
        start_scatter(j, slot)

        @pl.when(j == nu - 1)
        def _():
            wait_gather(1 - slot)

            @pl.when(j >= 1)
            def _():
                wait_scatter(1 - slot)
            wait_scatter(slot)


def _expert_ffn(xl, tile_expert, n_used, src_chunk, dst_chunk, w_gate_up, b_gate_up, w_down, b_down, layer, te):
    _, d = xl.shape
    depth, n_exp, _, two_ff = w_gate_up.shape
    ne = depth * n_exp
    dff = two_ff // 2
    wsel = lambda j, tex, nu, src, dst: (layer * n_exp + tex[j], 0, 0)
    return pl.pallas_call(
        functools.partial(_ffn_kernel, dff=dff, te=te),
        grid_spec=pltpu.PrefetchScalarGridSpec(
            num_scalar_prefetch=4,
            grid=(tile_expert.shape[0],),
            in_specs=[
                pl.BlockSpec(memory_space=pl.ANY),
                pl.BlockSpec((1, d, two_ff), wsel),
                pl.BlockSpec((1, 1, two_ff), wsel),
                pl.BlockSpec((1, dff, d), wsel),
                pl.BlockSpec((1, 1, d), wsel),
            ],
            out_specs=pl.BlockSpec(memory_space=pl.ANY),
            scratch_shapes=[
                pltpu.VMEM((2, te, d), BF16), pltpu.VMEM((2, te, d), BF16),
                pltpu.SemaphoreType.DMA((2,)), pltpu.SemaphoreType.DMA((2,)),
                pltpu.VMEM((d, two_ff), BF16), pltpu.VMEM((dff, d), BF16),
            ],
        ),
        out_shape=jax.ShapeDtypeStruct(xl.shape, BF16),
        input_output_aliases={4: 0},
        compiler_params=_params(("arbitrary",)),
        name="moe_ffn",
    )(tile_expert, n_used, src_chunk, dst_chunk, xl, w_gate_up.reshape(ne, d, two_ff),
      b_gate_up.reshape(ne, 1, two_ff), w_down.reshape(ne, dff, d), b_down.reshape(ne, 1, d))


def _comb_kernel(yl_ref, route_ref, x1_ref, mod_ref, fw_ref, o_ref, *, d, tm, rmax, final_norm):
    route = route_ref[...]
    r_iota = lax.broadcasted_iota(jnp.int32, (1, rmax), 1).astype(F32)
    wm = jnp.zeros((tm, rmax), F32)
    for k in range(TOP_K):
        wm = wm + jnp.where(r_iota == route[:, k:k + 1], route[:, 2 * TOP_K + k:2 * TOP_K + k + 1], 0.0)
    out = _dot(wm.astype(BF16), yl_ref[...])
    x2 = x1_ref[...] + mod_ref[0, :, 5 * d:6 * d] * out
    if final_norm:
        x2 = x2 * lax.rsqrt(jnp.mean(x2 * x2, axis=-1, keepdims=True) + EPS) * fw_ref[...]
    o_ref[...] = x2


def _combine(yl, route, x1, mod3, mod_row, final_w, tile0, n_tiles, tm, final_norm):
    d = x1.shape[1]
    rmax = _sorted_rows(tm)
    return pl.pallas_call(
        functools.partial(_comb_kernel, d=d, tm=tm, rmax=rmax, final_norm=final_norm),
        grid=(n_tiles,),
        in_specs=[
            pl.BlockSpec((rmax, d), lambda i: (tile0 + i, 0)),
            pl.BlockSpec((tm, LANES), lambda i: (tile0 + i, 0)),
            pl.BlockSpec((tm, d), lambda i: (tile0 + i, 0)),
            pl.BlockSpec((1, 1, 6 * d), lambda i: (mod_row(tile0 + i), 0, 0)),
            pl.BlockSpec((1, d), lambda i: (0, 0)),
        ],
        out_specs=pl.BlockSpec((tm, d), lambda i: (i, 0)),
        out_shape=jax.ShapeDtypeStruct((n_tiles * tm, d), F32),
        compiler_params=_params(("parallel",)),
        name="moe_combine",
    )(yl, route, x1, mod3, final_w.reshape(1, d))


@functools.lru_cache(maxsize=None)
def _dft_mats(l):
    f = np.arange(l, dtype=np.int64)
    ang = ((f[:, None] * f[None, :]) % (2 * l)).astype(np.float64) * (math.pi / l)
    return np.cos(ang).astype(np.float32), np.sin(ang).astype(np.float32)


@functools.lru_cache(maxsize=None)
def _filter_features(l):
    f32 = np.float32
    t = np.linspace(0.0, 1.0, l, dtype=f32)[:, None]
    bands = (HY_EMB - 1) // 2
    w = f32(2.0 * math.pi) * np.arange(l, dtype=f32)[:, None] / f32(l)
    f = np.linspace(1e-4, bands - 1, bands, dtype=f32)[None, :]
    z = np.concatenate([t, np.cos(f * w), -np.sin(f * w)], axis=-1).astype(f32)
    return np.pad(z, ((0, 0), (0, LANES - HY_EMB))), t


@functools.lru_cache(maxsize=None)
def _rope_tables(rows):
    f32 = np.float32
    row = np.repeat(np.arange(rows, dtype=f32), GRID_W)
    col = np.tile(np.arange(GRID_W, dtype=f32), rows)
    half = RET_DK // 4
    inv = (f32(ROPE_BASE) ** (-np.arange(half, dtype=f32) / f32(half))).astype(f32)
    ang = np.concatenate([row[:, None] * inv, col[:, None] * inv], axis=-1)
    ang = np.concatenate([ang, ang], axis=-1).astype(f32)
    sign = np.where(np.arange(RET_DK) < RET_DK // 2, -1.0, 1.0).astype(f32)
    return np.cos(ang), np.sin(ang) * sign


@functools.lru_cache(maxsize=None)
def _decay_rates():
    return np.linspace(math.log(HY_TARGET) / HY_SLOW_PCT, math.log(HY_TARGET) / HY_FAST_PCT,
                       HY_W, dtype=np.float32)[None, :]


def _pad2(a, rows, cols):
    return jnp.pad(a, ((0, rows - a.shape[0]), (0, cols - a.shape[1])))


def _ffn_plan(seg, rmax, te, n_ffn_tiles):
    nt, ne = seg.shape
    cum_t = jnp.cumsum(seg, axis=0)
    tot = cum_t[-1]
    padded = ((tot + te - 1) // te) * te
    ends = jnp.cumsum(padded)
    offs = ends - padded
    n_used = ends[-1] // te
    expert_of = lambda rows: jnp.minimum(jnp.sum(ends[None, :] <= rows[:, None], axis=1), ne - 1)
    tiles = jnp.arange(n_ffn_tiles, dtype=jnp.int32)
    tile_expert = expert_of(jnp.minimum(tiles, n_used - 1) * te).astype(jnp.int32)

    rows = jnp.arange(n_ffn_tiles * te // ROW_ALIGN, dtype=jnp.int32) * ROW_ALIGN
    hot_e = expert_of(rows)[:, None] == jnp.arange(ne)[None, :]
    pick_e = lambda tab: jnp.sum(jnp.where(hot_e[:, None, :], tab[None], 0), axis=2)
    r_s = rows - pick_e(offs[None, :])[:, 0]
    cum_sel = pick_e(cum_t)
    i_s = jnp.minimum(jnp.sum(cum_sel <= r_s[:, None], axis=1), nt - 1)
    hot_i = i_s[:, None] == jnp.arange(nt)[None, :]
    seg_base = (jnp.arange(nt)[:, None] * rmax + jnp.cumsum(seg, axis=1) - seg) - (cum_t - seg)
    src_row = r_s + jnp.sum(jnp.where(hot_i, pick_e(seg_base), 0), axis=1)
    valid = (r_s < pick_e(tot[None, :])[:, 0]) & (rows < ends[-1])
    src_chunk = jnp.where(valid, src_row // ROW_ALIGN, rmax // ROW_ALIGN - 1).astype(jnp.int32)
    pad_rank = jnp.cumsum(jnp.where(valid, 0, 1)) - 1
    spare = nt * rmax // ROW_ALIGN + jnp.minimum(pad_rank, _spare_chunks(te) - 1)
    dst_chunk = jnp.where(valid, src_row // ROW_ALIGN, spare).astype(jnp.int32)
    return tile_expert, n_used.reshape(1).astype(jnp.int32), src_chunk, dst_chunk


def _spare_chunks(te):
    return N_EXPERTS * (te // ROW_ALIGN - 1)


def kernel(x_prompt, x_sample, state_ret, c, c_ctx, ada_w, ada_b, norm_mix_w, w_in, conv_w, conv_b,
           f_w1, f_b1, f_w2, f_b2, f_w3, f_b3, f_freq, f_wout, hy_skip, ret_decay_logit,
           w_branch_a, w_branch_b, w_out, norm_ffn_w, router_w, router_b, w_gate_up, b_gate_up,
           w_down, b_down, final_norm_w):
    bp, lp, d = x_prompt.shape
    bs, ls, _ = x_sample.shape
    depth = ada_w.shape[0]
    tp, ts = bp * lp, bs * ls
    t = tp + ts
    assert ls % lp == 0 and tp % ls == 0 and lp % RET_CHUNK == 0 and ls % GRID_W == 0
    tm = 512 if (tp % 512 == 0 and ls % 512 == 0) else lp
    te = 512
    nt = t // tm
    rmax = _sorted_rows(tm)
    n_ffn_tiles = -(-(t * TOP_K + nt * N_EXPERTS * (ROW_ALIGN - 1)) // te) + N_EXPERTS

    dft = {l: tuple(jnp.asarray(a).astype(BF16) for a in _dft_mats(l)) for l in (lp, ls)}
    feats = {l: tuple(jnp.asarray(a) for a in _filter_features(l)) for l in (lp, ls)}
    rope_tabs = tuple(jnp.asarray(a) for a in _rope_tables(ls // GRID_W))
    deltas = jnp.asarray(_decay_rates())

    cvec = jnp.zeros((8, d), F32).at[:bs].set(c).at[bs].set(c_ctx)
    mod3 = _modulation(cvec, ada_w, ada_b).reshape(depth * 8, 1, 6 * d)
    n_p, n_s = tp // tm, ts // tm

    x = (x_prompt.reshape(tp, d), x_sample.reshape(ts, d))
    states = []
    for l in range(depth):
        mod_row = lambda i, l=l: l * 8 + jnp.where(i < n_p, bs, jnp.maximum(i - n_p, 0) // (ls // tm))
        w_in_bf = w_in[l].astype(BF16)
        proj = (_in_proj(x[0], norm_mix_w[l], mod3, mod_row, w_in_bf, tm),
                _in_proj(x[1], norm_mix_w[l], mod3, lambda i, f=mod_row: f(i + n_p), w_in_bf, tm))

        w1 = _pad2(f_w1[l], LANES, LANES)
        w2 = _pad2(f_w2[l], LANES, LANES)
        w3 = _pad2(f_w3[l], LANES, LANES)
        wo = _pad2(f_wout[l], LANES, f_wout.shape[2])
        b1, b2, b3, fr = (_pad2(a[l][None, :], 1, LANES) for a in (f_b1, f_b2, f_b3, f_freq))
        za, zb = [], []
        for grp, (nb, ln) in enumerate(((bp, lp), (bs, ls))):
            cm, sm = dft[ln]
            taps, sums = _hyena_taps(ln, feats[ln][0], feats[ln][1], deltas, w1, b1, w2, b2, w3, b3, fr, wo)
            kc, ki, kn = _filter_spectra(ln, cm, sm, taps, sums)
            za.append(_hyena_branch(proj[grp], 0, nb, ln, cm, sm, conv_w[l], conv_b[l], kc, ki, kn, hy_skip[l]))
            zb_g, st = _retention_branch(proj[grp], 0, nb, ln, ret_decay_logit[l],
                                         rope_tabs if grp == 1 else None,
                                         state_ret if grp == 1 else None, l)
            zb.append(zb_g)
            if grp == 0:
                states.append(st)

        rw = _pad2(router_w[l], d, LANES)
        rb = jnp.full((1, LANES), -1e30, F32).at[0, :N_EXPERTS].set(router_b[l])
        x1, xl, route, seg = _mix(za, zb, proj, x, mod3, mod_row,
                                  w_branch_a[l].astype(BF16), w_branch_b[l].astype(BF16),
                                  w_out[l].astype(BF16), norm_ffn_w[l], rw, rb, tm, _spare_chunks(te) * ROW_ALIGN)
        tile_expert, n_used, src_chunk, dst_chunk = _ffn_plan(seg[:, 0, :N_EXPERTS].astype(jnp.int32), rmax, te,
                                                              n_ffn_tiles)
        yl = _expert_ffn(xl, tile_expert, n_used, src_chunk, dst_chunk, w_gate_up, b_gate_up, w_down, b_down,
                         l, te)
        last = l == depth - 1
        x = (_combine(yl, route, x1, mod3, mod_row, final_norm_w, 0, n_p, tm, last),
             _combine(yl, route, x1, mod3, mod_row, final_norm_w, n_p, n_s, tm, last))

    state_ret_new = jnp.stack(states, axis=1)
    return (x[0].reshape(bp, lp, d), x[1].reshape(bs, ls, d), state_ret_new)
```

```python
import functools
import math

import jax
import jax.numpy as jnp
import numpy as np
from jax import lax
from jax.experimental import pallas as pl
from jax.experimental.pallas import tpu as pltpu

F32 = jnp.float32
BF16 = jnp.bfloat16

GRID_W = 64
HY_W = 512
HY_ORDER = 2
HY_EMB = 33
HY_FAST_PCT = 0.3
HY_SLOW_PCT = 1.5
HY_TARGET = 1e-2
N_RET_HEADS = 4
RET_DK = 128
RET_DV = 256
RET_CHUNK = 256
ROPE_BASE = 10000.0
N_EXPERTS = 32
TOP_K = 4
SWIGLU_LIMIT = 7.0
SWIGLU_ALPHA = 1.702
EPS = 1e-6

LANES = 128
VMEM_LIMIT = 56 * 1024 * 1024

COL_V, COL_G, COL_GA, COL_GB = 0, 1024, 2048, 3072
COL_HV, COL_HX1, COL_HX2, COL_Q, COL_K = 4096, 4608, 5120, 5632, 6144
IN_W = 6656
IN_COL_BLOCKS = (5, 6, 7, 8, 9, 10, 11, 12, 0, 1, 2, 3, 4)


def _dot(a, b):
    return jnp.dot(a, b, preferred_element_type=F32)


def _split_bf16(a):
    hi = a.astype(BF16)
    lo = (a - hi.astype(F32)).astype(BF16)
    return hi, lo


def _dot3(a, b):
    ah, al = _split_bf16(a)
    bh, bl = _split_bf16(b)
    return _dot(ah, bh) + (_dot(ah, bl) + _dot(al, bh))


def _sigmoid(x):
    return 0.5 * jnp.tanh(0.5 * x) + 0.5


def _silu(x):
    return x * _sigmoid(x)


def _params(sem, vmem=VMEM_LIMIT):
    return pltpu.CompilerParams(dimension_semantics=sem, vmem_limit_bytes=vmem)


def _resident(shape, index_map):
    return pl.BlockSpec(shape, index_map, pipeline_mode=pl.Buffered(1))


def _mod_kernel(c_ref, w_ref, b_ref, o_ref):
    o_ref[0] = _dot3(_silu(c_ref[...]), w_ref[0]) + b_ref[0]


def _modulation(cvec, ada_w, ada_b):
    depth, d, six_d = ada_w.shape
    tn = six_d // 4
    return pl.pallas_call(
        _mod_kernel,
        grid=(depth, six_d // tn),
        in_specs=[
            pl.BlockSpec((8, d), lambda l, j: (0, 0)),
            pl.BlockSpec((1, d, tn), lambda l, j: (l, 0, j)),
            pl.BlockSpec((1, 1, tn), lambda l, j: (l, 0, j)),
        ],
        out_specs=pl.BlockSpec((1, 8, tn), lambda l, j: (l, 0, j)),
        out_shape=jax.ShapeDtypeStruct((depth, 8, six_d), F32),
        compiler_params=_params(("parallel", "parallel")),
        name="adaln_mod",
    )(cvec, ada_w, ada_b.reshape(depth, 1, six_d))


def _in_kernel(x_ref, nw_ref, mod_ref, w_ref, o_ref, *, d, cw):
    x = x_ref[...]
    y = x * lax.rsqrt(jnp.mean(x * x, axis=-1, keepdims=True) + EPS) * nw_ref[...]
    shift = mod_ref[0, :, 0:d]
    scale = mod_ref[0, :, d:2 * d]
    u = (y * (1.0 + scale) + shift).astype(BF16)
    for j, src in enumerate(IN_COL_BLOCKS):
        o_ref[:, j * cw:(j + 1) * cw] = _dot(u, w_ref[:, src * cw:(src + 1) * cw]).astype(BF16)


def _in_proj(x, norm_w, mod3, mod_row, w_in_bf, tm):
    t, d = x.shape
    return pl.pallas_call(
        functools.partial(_in_kernel, d=d, cw=IN_W // len(IN_COL_BLOCKS)),
        grid=(t // tm,),
        in_specs=[
            pl.BlockSpec((tm, d), lambda i: (i, 0)),
            pl.BlockSpec((1, d), lambda i: (0, 0)),
            pl.BlockSpec((1, 1, 2 * d), lambda i: (mod_row(i), 0, 0)),
            _resident((d, IN_W), lambda i: (0, 0)),
        ],
        out_specs=pl.BlockSpec((tm, IN_W), lambda i: (i, 0)),
        out_shape=jax.ShapeDtypeStruct((t, IN_W), BF16),
        compiler_params=_params(("parallel",)),
        name="in_proj",
    )(x, norm_w.reshape(1, d), mod3, w_in_bf)


def _filt_kernel(z_ref, t_ref, w1, b1, w2, b2, w3, b3, fr, wo, dl_ref, h_ref, s_ref, *, tl):
    i = pl.program_id(0)
    freq = fr[...]
    h = jnp.sin(freq * (_dot3(z_ref[...], w1[...]) + b1[...]))
    h = jnp.sin(freq * (_dot3(h, w2[...]) + b2[...]))
    h = jnp.sin(freq * (_dot3(h, w3[...]) + b3[...]))
    h = _dot3(h, wo[...])
    decay = jnp.exp(-t_ref[...] * jnp.abs(dl_ref[...]))
    h = h * jnp.concatenate([decay] * (2 * HY_ORDER), axis=1)
    h_ref[...] = h
    row = i * tl + lax.broadcasted_iota(jnp.int32, (tl, 1), 0)
    col = lax.broadcasted_iota(jnp.int32, (1, 2 * HY_ORDER * HY_W), 1)
    is_bwd = ((col // HY_W) % 2) == 1
    part = jnp.sum(jnp.where((row == 0) & is_bwd, 0.0, jnp.abs(h)), axis=0, keepdims=True)

    @pl.when(i == 0)
    def _():
        s_ref[...] = jnp.zeros_like(s_ref)

    s_ref[...] += part


def _hyena_taps(l, zfeat, tcol, deltas, w1, b1, w2, b2, w3, b3, freq, wout):
    tl = min(l, 256)
    nw = 2 * HY_ORDER * HY_W
    full = lambda shape: pl.BlockSpec(shape, lambda i: (0,) * len(shape))
    return pl.pallas_call(
        functools.partial(_filt_kernel, tl=tl),
        grid=(l // tl,),
        in_specs=[
            pl.BlockSpec((tl, LANES), lambda i: (i, 0)),
            pl.BlockSpec((tl, 1), lambda i: (i, 0)),
            full((LANES, LANES)), full((1, LANES)),
            full((LANES, LANES)), full((1, LANES)),
            full((LANES, LANES)), full((1, LANES)),
            full((1, LANES)),
            full((LANES, nw)),
            full((1, HY_W)),
        ],
        out_specs=[pl.BlockSpec((tl, nw), lambda i: (i, 0)), pl.BlockSpec((1, nw), lambda i: (0, 0))],
        out_shape=[jax.ShapeDtypeStruct((l, nw), F32), jax.ShapeDtypeStruct((1, nw), F32)],
        compiler_params=_params(("arbitrary",)),
        name=f"hyena_taps_{l}",
    )(zfeat, tcol, w1, b1, w2, b2, w3, b3, freq, wout, deltas)


def _fdft_kernel(c_ref, s_ref, hf_ref, hb_ref, sf_ref, sb_ref, kc_ref, ki_ref, kn_ref, *, l):
    row = lax.broadcasted_iota(jnp.int32, (l, 1), 0)
    hf = hf_ref[...]
    hb = jnp.where(row == 0, 0.0, hb_ref[...])
    inv = 1.0 / (sf_ref[...] + sb_ref[...])
    a = hf + hb
    b = hb - hf
    wf = jnp.where(row == 0, 0.5 / l, 1.0 / l)
    kc_ref[...] = _dot(c_ref[...], a.astype(BF16)) * wf * inv
    ki_ref[...] = _dot(s_ref[...], b.astype(BF16)) * wf * inv
    sgn = jnp.where(row % 2 == 0, 1.0, -1.0)
    kn_ref[...] = jnp.sum(a * sgn, axis=0, keepdims=True) * inv * (0.5 / l)


def _filter_spectra(l, cmat, smat, taps, sums):
    wt = 256
    nb = HY_W // wt
    two_w = 2 * HY_W
    return pl.pallas_call(
        functools.partial(_fdft_kernel, l=l),
        grid=(HY_ORDER, nb),
        in_specs=[
            _resident((l, l), lambda o, j: (0, 0)),
            _resident((l, l), lambda o, j: (0, 0)),
            pl.BlockSpec((l, wt), lambda o, j: (0, o * 2 * nb + j)),
            pl.BlockSpec((l, wt), lambda o, j: (0, o * 2 * nb + nb + j)),
            pl.BlockSpec((1, wt), lambda o, j: (0, o * 2 * nb + j)),
            pl.BlockSpec((1, wt), lambda o, j: (0, o * 2 * nb + nb + j)),
        ],
        out_specs=[
            pl.BlockSpec((l, wt), lambda o, j: (0, o * nb + j)),
            pl.BlockSpec((l, wt), lambda o, j: (0, o * nb + j)),
            pl.BlockSpec((1, wt), lambda o, j: (0, o * nb + j)),
        ],
        out_shape=[
            jax.ShapeDtypeStruct((l, HY_ORDER * HY_W), F32),
            jax.ShapeDtypeStruct((l, HY_ORDER * HY_W), F32),
            jax.ShapeDtypeStruct((1, HY_ORDER * HY_W), F32),
        ],
        compiler_params=_params(("parallel", "parallel")),
        name=f"hyena_spectra_{l}",
    )(cmat, smat, taps, taps, sums, sums)


def _hy_kernel(c_ref, s_ref, hv_ref, hx1_ref, hx2_ref, cw_ref, cb_ref,
               kc0, ki0, kc1, ki1, kn_ref, sk_ref, o_ref, z_scr, zb_scr, p_scr, q_scr, *, l, rc, sb):
    row = lax.broadcasted_iota(jnp.int32, (l, 1), 0)
    first = row == 0
    last = row == l - 1
    sgn = jnp.where(row % 2 == 0, 1.0, -1.0)
    cw = cw_ref[...]
    cb = cb_ref[...]

    def short_conv(ref, rows, part):
        x = ref[rows, :].astype(F32)
        xm = jnp.where(first, 0.0, pltpu.roll(x, 1, 0))
        xp = jnp.where(last, 0.0, pltpu.roll(x, l - 1, 0))
        return (xm * cw[0, part:part + 1] + x * cw[1, part:part + 1] + xp * cw[2, part:part + 1]
                + cb[part:part + 1])

    for s in range(sb):
        rows = slice(s * l, (s + 1) * l)
        z_scr[s] = short_conv(hv_ref, rows, 0)
        for o, (gate_ref, kc, ki) in enumerate(((hx1_ref, kc0, ki0), (hx2_ref, kc1, ki1))):
            z = z_scr[s]
            zb_scr[s] = z.astype(BF16)
            nyq = jnp.sum(z * sgn, axis=0, keepdims=True) * kn_ref[o:o + 1]
            for r in range(l // rc):
                sl = slice(r * rc, (r + 1) * rc)
                zc = _dot(c_ref[sl, :], zb_scr[s])
                zs = _dot(s_ref[sl, :], zb_scr[s])
                p_scr[s, sl, :] = (zc * kc[sl, :] + zs * ki[sl, :]).astype(BF16)
                q_scr[s, sl, :] = (zs * kc[sl, :] - zc * ki[sl, :]).astype(BF16)
            gate = short_conv(gate_ref, rows, o + 1)
            for r in range(l // rc):
                sl = slice(r * rc, (r + 1) * rc)
                y = _dot(c_ref[sl, :], p_scr[s]) + _dot(s_ref[sl, :], q_scr[s])
                y = y + sgn[sl] * nyq + z_scr[s, sl, :] * sk_ref[o:o + 1]
                z_scr[s, sl, :] = gate[sl] * y
        o_ref[rows, :] = z_scr[s].astype(BF16)


def _hyena_branch(proj, row_blk0, nb, l, cmat, smat, conv_w, conv_b, kc, ki, kn, skip):
    wt = 256
    nw = HY_W // wt
    rc = min(l, 512)
    sb = math.gcd(nb, max(1, 2048 // l))
    assert row_blk0 % sb == 0
    hv0, hx10, hx20 = COL_HV // wt, COL_HX1 // wt, COL_HX2 // wt
    seq = lambda c0: pl.BlockSpec((sb * l, wt), lambda j, b: (row_blk0 // sb + b, c0 + j))
    tab = lambda o: pl.BlockSpec((l, wt), lambda j, b: (0, o * nw + j))
    return pl.pallas_call(
        functools.partial(_hy_kernel, l=l, rc=rc, sb=sb),
        grid=(nw, nb // sb),
        in_specs=[
            _resident((l, l), lambda j, b: (0, 0)),
            _resident((l, l), lambda j, b: (0, 0)),
            seq(hv0), seq(hx10), seq(hx20),
            pl.BlockSpec((3, 3, wt), lambda j, b: (0, 0, j)),
            pl.BlockSpec((3, wt), lambda j, b: (0, j)),
            tab(0), tab(0), tab(1), tab(1),
            pl.BlockSpec((HY_ORDER, wt), lambda j, b: (0, j)),
            pl.BlockSpec((HY_ORDER, wt), lambda j, b: (0, j)),
        ],
        out_specs=pl.BlockSpec((sb * l, wt), lambda j, b: (b, j)),
        out_shape=jax.ShapeDtypeStruct((nb * l, HY_W), BF16),
        scratch_shapes=[
            pltpu.VMEM((sb, l, wt), F32), pltpu.VMEM((sb, l, wt), BF16),
            pltpu.VMEM((sb, l, wt), BF16), pltpu.VMEM((sb, l, wt), BF16),
        ],
        compiler_params=_params(("parallel", "parallel")),
        name=f"hyena_{l}",
    )(cmat, smat, proj, proj, proj, conv_w.reshape(3, 3, HY_W), conv_b.reshape(3, HY_W),
      kc, ki, kc, ki, kn.reshape(HY_ORDER, HY_W), skip)


def _log_sigmoid(x):
    return jnp.minimum(x, 0.0) - jnp.log1p(jnp.exp(-jnp.abs(x)))


def _ret_kernel(*refs, l, hb, use_rope, has_s0):
    it = iter(refs)
    dl_ref, q_ref, k_ref, v_ref, g_ref = (next(it) for _ in range(5))
    cos_ref, sin_ref = (next(it), next(it)) if use_rope else (None, None)
    s0_ref = next(it) if has_s0 else None
    o_ref, st_ref, kr_scr, kv_scr = (next(it) for _ in range(4))

    c, dk, dv = RET_CHUNK, RET_DK, RET_DV
    nc = l // c
    unroll = min(nc, 4)
    pos = lax.broadcasted_iota(jnp.int32, (c, 1), 0).astype(F32)
    diff = pos - lax.broadcasted_iota(jnp.int32, (1, c), 1).astype(F32)

    def rope(x, r0):
        if not use_rope:
            return x
        return x * cos_ref[pl.ds(r0, c), :] + pltpu.roll(x, dk // 2, 1) * sin_ref[pl.ds(r0, c), :]

    for h in range(hb):
        qs = slice(h * dk, (h + 1) * dk)
        vs = slice(h * dv, (h + 1) * dv)
        lgf = _log_sigmoid(dl_ref[0, h])
        lgb = _log_sigmoid(dl_ref[1, h])
        dm = jnp.where(diff >= 0, jnp.exp(lgf * diff), jnp.exp(-lgb * diff))
        zeta_f = jnp.exp(lgf * (c - 1.0 - pos))
        zeta_b = jnp.exp(lgb * pos)
        xi_f = jnp.exp(lgf * (pos + 1.0))
        xi_b = jnp.exp(lgb * (c - pos))
        gc_f = jnp.exp(lgf * c)
        gc_b = jnp.exp(lgb * c)

        def kv_step(ci, carry):
            r0 = pl.multiple_of(ci * c, c)
            k = rope(k_ref[pl.ds(r0, c), qs].astype(F32), r0) * (dk ** -0.5)
            kr_scr[pl.ds(r0, c), qs] = k.astype(BF16)
            kk = jnp.concatenate([k * zeta_f, k * zeta_b], axis=1).astype(BF16)
            kv_scr[h, ci] = lax.dot_general(kk, v_ref[pl.ds(r0, c), vs], (((0,), (0,)), ((), ())),
                                            preferred_element_type=F32)
            return carry

        lax.fori_loop(0, nc, kv_step, 0, unroll=unroll)

        def fwd_scan(ci, s):
            upd = kv_scr[h, ci, 0:dk, :]
            kv_scr[h, ci, 0:dk, :] = s
            return gc_f * s + upd

        def bwd_scan(i, r):
            ci = nc - 1 - i
            upd = kv_scr[h, ci, dk:2 * dk, :]
            kv_scr[h, ci, dk:2 * dk, :] = r
            return gc_b * r + upd

        zero = jnp.zeros((dk, dv), F32)
        st_ref[0, 0, h] = lax.fori_loop(0, nc, fwd_scan, s0_ref[0, 0, 0, h] if has_s0 else zero)
        st_ref[0, 1, h] = lax.fori_loop(0, nc, bwd_scan, s0_ref[0, 0, 1, h] if has_s0 else zero)

        def out_step(ci, carry):
            r0 = pl.multiple_of(ci * c, c)
            q = rope(q_ref[pl.ds(r0, c), qs].astype(F32), r0)
            sc = lax.dot_general(q.astype(BF16), kr_scr[pl.ds(r0, c), qs], (((1,), (1,)), ((), ())),
                                 preferred_element_type=F32) * dm
            qq = jnp.concatenate([q * xi_f, q * xi_b], axis=1).astype(BF16)
            o = _dot(sc.astype(BF16), v_ref[pl.ds(r0, c), vs]) + _dot(qq, kv_scr[h, ci].astype(BF16))
            o = o * lax.rsqrt(jnp.mean(o * o, axis=-1, keepdims=True) + EPS)
            o_ref[pl.ds(r0, c), vs] = (_silu(g_ref[pl.ds(r0, c), vs].astype(F32)) * o).astype(BF16)
            return carry

        lax.fori_loop(0, nc, out_step, 0, unroll=unroll)


def _retention_branch(proj, row_blk0, nb, l, decay_logit, rope_tabs, state, layer):
    nh, dk, dv = N_RET_HEADS, RET_DK, RET_DV
    use_rope = rope_tabs is not None
    has_s0 = state is not None
    hb = nh if l <= 512 else 1
    ng = nh // hb
    seq = lambda col, w: pl.BlockSpec((l, hb * w), lambda b, g: (row_blk0 + b, col // (hb * w) + g))
    in_specs = [
        pl.BlockSpec((2, hb, 1, 1), lambda b, g: (0, g, 0, 0)),
        seq(COL_Q, dk), seq(COL_K, dk), seq(COL_V, dv), seq(COL_G, dv),
    ]
    args = [decay_logit.reshape(2, nh, 1, 1), proj, proj, proj, proj]
    if use_rope:
        in_specs += [pl.BlockSpec((l, dk), lambda b, g: (0, 0))] * 2
        args += list(rope_tabs)
    if has_s0:
        in_specs.append(pl.BlockSpec((1, 1, 2, hb, dk, dv), lambda b, g: (b, layer, 0, g, 0, 0)))
        args.append(state)
    return pl.pallas_call(
        functools.partial(_ret_kernel, l=l, hb=hb, use_rope=use_rope, has_s0=has_s0),
        grid=(nb, ng),
        in_specs=in_specs,
        out_specs=[
            pl.BlockSpec((l, hb * dv), lambda b, g: (b, g)),
            pl.BlockSpec((1, 2, hb, dk, dv), lambda b, g: (b, 0, g, 0, 0)),
        ],
        out_shape=[
            jax.ShapeDtypeStruct((nb * l, nh * dv), BF16),
            jax.ShapeDtypeStruct((nb, 2, nh, dk, dv), F32),
        ],
        scratch_shapes=[
            pltpu.VMEM((l, hb * dk), BF16),
            pltpu.VMEM((hb, l // RET_CHUNK, 2 * dk, dv), F32),
        ],
        compiler_params=_params(("parallel", "parallel")),
        name=f"retention_{l}",
    )(*args)


ROW_ALIGN = 16


def _sorted_rows(tm):
    worst = TOP_K * tm + N_EXPERTS * (ROW_ALIGN - 1) + ROW_ALIGN
    return -(-worst // LANES) * LANES


def _mix_kernel(zap_ref, zas_ref, zbp_ref, zbs_ref, gap_ref, gas_ref, gbp_ref, gbs_ref, xp_ref, xs_ref,
                mod_ref, wa_ref, wb_ref, wo_ref, nw_ref, rw_ref, rb_ref, x1_ref, xl_ref, route_ref, seg_ref,
                h_scr, pr_scr, sg_scr, *, d, tm, rmax, n_tiles, n_prompt_tiles):
    i = pl.program_id(0)

    @pl.when(i == 0)
    def _():
        h_scr[...] = jnp.zeros_like(h_scr)
        pr_scr[...] = jnp.zeros_like(pr_scr)
        sg_scr[...] = jnp.zeros_like(sg_scr)

    @pl.when(i > n_tiles)
    def _():
        xl_ref[...] = jnp.zeros_like(xl_ref)

    @pl.when(i <= n_tiles)
    def _():
        _mix_step(zap_ref, zas_ref, zbp_ref, zbs_ref, gap_ref, gas_ref, gbp_ref, gbs_ref, xp_ref, xs_ref,
                  mod_ref, wa_ref, wb_ref, wo_ref, nw_ref, rw_ref, rb_ref, x1_ref, xl_ref, route_ref, seg_ref,
                  h_scr, pr_scr, sg_scr, d=d, tm=tm, rmax=rmax, n_tiles=n_tiles, n_prompt_tiles=n_prompt_tiles)


def _mix_step(zap_ref, zas_ref, zbp_ref, zbs_ref, gap_ref, gas_ref, gbp_ref, gbs_ref, xp_ref, xs_ref,
              mod_ref, wa_ref, wb_ref, wo_ref, nw_ref, rw_ref, rb_ref, x1_ref, xl_ref, route_ref, seg_ref,
              h_scr, pr_scr, sg_scr, *, d, tm, rmax, n_tiles, n_prompt_tiles):
    i = pl.program_id(0)
    r_col = lax.broadcasted_iota(jnp.int32, (rmax, 1), 0).astype(F32)
    in_seg = jnp.logical_and(r_col >= sg_scr[0:1], r_col < sg_scr[0:1] + sg_scr[1:2])
    seg_hot = jnp.where(in_seg, 1.0, 0.0).astype(BF16)
    row_of = _dot(jnp.concatenate([seg_hot, seg_hot], axis=1), pr_scr[...])
    sel = jnp.where(row_of == r_col + 1.0, 1.0, 0.0).astype(BF16)
    xl_ref[...] = _dot(sel, h_scr[...]).astype(BF16)

    is_prompt = jnp.minimum(i, n_tiles - 1) < n_prompt_tiles
    pick = lambda p_ref, s_ref: jnp.where(is_prompt, p_ref[...], s_ref[...])
    za, zb = pick(zap_ref, zas_ref), pick(zbp_ref, zbs_ref)
    ga, gb = pick(gap_ref, gas_ref), pick(gbp_ref, gbs_ref)
    merged = (_sigmoid(ga.astype(F32)) * _dot(za, wa_ref[...])
              + _sigmoid(gb.astype(F32)) * _dot(zb, wb_ref[...]))
    gate1 = mod_ref[0, :, 2 * d:3 * d]
    shift2 = mod_ref[0, :, 3 * d:4 * d]
    scale2 = mod_ref[0, :, 4 * d:5 * d]
    x1 = pick(xp_ref, xs_ref) + gate1 * _dot(merged.astype(BF16), wo_ref[...])
    x1_ref[...] = x1
    hn = x1 * lax.rsqrt(jnp.mean(x1 * x1, axis=-1, keepdims=True) + EPS) * nw_ref[...]
    hmo = hn * (1.0 + scale2) + shift2

    logits = _dot3(hmo, rw_ref[...]) + rb_ref[...]
    lane = lax.broadcasted_iota(jnp.int32, (1, LANES), 1)
    idxs, vals = [], []
    for _ in range(TOP_K):
        m = jnp.max(logits, axis=1, keepdims=True)
        ik = jnp.min(jnp.where(logits == m, lane, LANES), axis=1, keepdims=True)
        idxs.append(ik)
        vals.append(m)
        logits = jnp.where(lane == ik, -jnp.inf, logits)
    exps = [jnp.exp(v - vals[0]) for v in vals]
    denom = exps[0] + exps[1] + exps[2] + exps[3]

    hot = jnp.zeros((tm, LANES), F32)
    for ik in idxs:
        hot = hot + jnp.where(lane == ik, 1.0, 0.0)
    r_i = lax.broadcasted_iota(jnp.int32, (tm, tm), 0)
    c_i = lax.broadcasted_iota(jnp.int32, (tm, tm), 1)
    below = jnp.where(c_i < r_i, 1.0, 0.0).astype(BF16)
    before = _dot(below, hot.astype(BF16))
    seg = jnp.ceil(jnp.sum(hot, axis=0, keepdims=True) * (1.0 / ROW_ALIGN)) * ROW_ALIGN
    seg_ref[0] = seg
    e_r = lax.broadcasted_iota(jnp.int32, (LANES, LANES), 0)
    e_c = lax.broadcasted_iota(jnp.int32, (LANES, LANES), 1)
    seg_start = _dot3(jnp.broadcast_to(seg, (8, LANES)), jnp.where(e_r < e_c, 1.0, 0.0))[0:1]
    base = seg_start + before

    route = jnp.zeros((tm, LANES), F32)
    for k in range(TOP_K):
        srow = jnp.sum(jnp.where(lane == idxs[k], base, 0.0), axis=1, keepdims=True)
        route = route + jnp.where(lane == k, srow, 0.0)
        route = route + jnp.where(lane == TOP_K + k, idxs[k].astype(F32), 0.0)
        route = route + jnp.where(lane == 2 * TOP_K + k, exps[k] / denom, 0.0)
    route_ref[...] = route

    h_scr[...] = hmo.astype(BF16)
    pair_row = jnp.where(hot > 0.0, base + 1.0, 0.0).T
    pr_scr[...] = jnp.concatenate(_split_bf16(pair_row), axis=0)
    sg_scr[0:1] = seg_start
    sg_scr[1:2] = seg


def _mix(za, zb, proj, x, mod3, mod_row, wa, wb, wo, norm_w, rw, rb, tm, spare_rows):
    d = x[0].shape[1]
    n_p = x[0].shape[0] // tm
    n_s = x[1].shape[0] // tm
    nt = n_p + n_s
    t = nt * tm
    rmax = _sorted_rows(tm)
    spare_blocks = -(-spare_rows // rmax)
    vw = N_RET_HEADS * RET_DV
    cur = lambda i: jnp.minimum(i, nt - 1)
    prev = lambda i: jnp.maximum(i - 1, 0)
    pair = lambda w, col: [pl.BlockSpec((tm, w), lambda i: (jnp.minimum(cur(i), n_p - 1), col)),
                           pl.BlockSpec((tm, w), lambda i: (jnp.maximum(cur(i) - n_p, 0), col))]
    const = lambda i: (0, 0)
    return pl.pallas_call(
        functools.partial(_mix_kernel, d=d, tm=tm, rmax=rmax, n_tiles=nt, n_prompt_tiles=n_p),
        grid=(nt + 1 + spare_blocks,),
        in_specs=[
            *pair(HY_W, 0), *pair(vw, 0), *pair(d, COL_GA // d), *pair(d, COL_GB // d), *pair(d, 0),
            pl.BlockSpec((1, 1, 6 * d), lambda i: (mod_row(cur(i)), 0, 0)),
            _resident((HY_W, d), const), _resident((vw, d), const), _resident((d, d), const),
            pl.BlockSpec((1, d), const),
            _resident((d, LANES), const), pl.BlockSpec((1, LANES), const),
        ],
        out_specs=[
            pl.BlockSpec((tm, d), lambda i: (cur(i), 0)),
            pl.BlockSpec((rmax, d), lambda i: (prev(i), 0)),
            pl.BlockSpec((tm, LANES), lambda i: (cur(i), 0)),
            pl.BlockSpec((1, 1, LANES), lambda i: (cur(i), 0, 0)),
        ],
        out_shape=[
            jax.ShapeDtypeStruct((t, d), F32),
            jax.ShapeDtypeStruct(((nt + spare_blocks) * rmax, d), BF16),
            jax.ShapeDtypeStruct((t, LANES), F32),
            jax.ShapeDtypeStruct((nt, 1, LANES), F32),
        ],
        scratch_shapes=[pltpu.VMEM((tm, d), BF16), pltpu.VMEM((2 * LANES, tm), BF16),
                        pltpu.VMEM((2, LANES), F32)],
        compiler_params=_params(("arbitrary",)),
        name="merge_router",
    )(*za, *zb, *proj, *proj, *x, mod3, wa, wb, wo, norm_w.reshape(1, d), rw, rb)


def _ffn_kernel(te_ref, nu_ref, src_ref, dst_ref, xl_ref, wgu_ref, bgu_ref, wd_ref, bd_ref, yl_ref,
                xbuf, ybuf, gsem, ssem, wgu_scr, wd_scr, *, dff, te):
    j = pl.program_id(0)
    nu = nu_ref[0]
    cpt = te // ROW_ALIGN
    slot = j % 2

    def hbm_chunk(ci):
        return pl.ds(pl.multiple_of(ci * ROW_ALIGN, ROW_ALIGN), ROW_ALIGN)

    def start_gather(tile, buf, part=0, parts=1):
        for ci in range(part * cpt // parts, (part + 1) * cpt // parts):
            pltpu.make_async_copy(xl_ref.at[hbm_chunk(src_ref[tile * cpt + ci])],
                                  xbuf.at[buf, pl.ds(ci * ROW_ALIGN, ROW_ALIGN)], gsem.at[buf]).start()

    def wait_gather(buf):
        pltpu.make_async_copy(xl_ref.at[pl.ds(0, te)], xbuf.at[buf], gsem.at[buf]).wait()

    def start_scatter(tile, buf):
        for ci in range(cpt):
            pltpu.make_async_copy(ybuf.at[buf, pl.ds(ci * ROW_ALIGN, ROW_ALIGN)],
                                  yl_ref.at[hbm_chunk(dst_ref[tile * cpt + ci])], ssem.at[buf]).start()

    def wait_scatter(buf):
        pltpu.make_async_copy(ybuf.at[buf], yl_ref.at[pl.ds(0, te)], ssem.at[buf]).wait()

    @pl.when(j < nu)
    def _():
        @pl.when(j == 0)
        def _():
            start_gather(0, 0)

        @pl.when(j >= 2)
        def _():
            wait_scatter(slot)

        @pl.when(jnp.logical_or(j == 0, te_ref[j] != te_ref[jnp.maximum(j - 1, 0)]))
        def _():
            wgu_scr[...] = wgu_ref[0].astype(BF16)
            wd_scr[...] = wd_ref[0].astype(BF16)

        wait_gather(slot)
        nblk = 4
        bw = 2 * dff // nblk
        blocks = []
        for b in range(nblk):
            blocks.append(_dot(xbuf[slot], wgu_scr[:, b * bw:(b + 1) * bw]) + bgu_ref[0, :, b * bw:(b + 1) * bw])
            start_gather(j + 1, 1 - slot, b, nblk)
        hgu = jnp.concatenate(blocks, axis=1)
        gate = jnp.minimum(hgu[:, :dff], SWIGLU_LIMIT)
        up = jnp.clip(hgu[:, dff:], -SWIGLU_LIMIT, SWIGLU_LIMIT)
        act = gate * _sigmoid(SWIGLU_ALPHA * gate) * (up + 1.0)
        ybuf[slot] = (_dot(act.astype(BF16), wd_scr[...]) + bd_ref[0]).astype(BF16)
        start_scatter(j, slot)

        @pl.when(j == nu - 1)
        def _():
            wait_gather(1 - slot)

            @pl.when(j >= 1)
            def _():
                wait_scatter(1 - slot)
            wait_scatter(slot)


def _expert_ffn(xl, tile_expert, n_used, src_chunk, dst_chunk, w_gate_up, b_gate_up, w_down, b_down, layer, te):
    _, d = xl.shape
    depth, n_exp, _, two_ff = w_gate_up.shape
    ne = depth * n_exp
    dff = two_ff // 2
    wsel = lambda j, tex, nu, src, dst: (layer * n_exp + tex[j], 0, 0)
    return pl.pallas_call(
        functools.partial(_ffn_kernel, dff=dff, te=te),
        grid_spec=pltpu.PrefetchScalarGridSpec(
            num_scalar_prefetch=4,
            grid=(tile_expert.shape[0],),
            in_specs=[
                pl.BlockSpec(memory_space=pl.ANY),
                pl.BlockSpec((1, d, two_ff), wsel),
                pl.BlockSpec((1, 1, two_ff), wsel),
                pl.BlockSpec((1, dff, d), wsel),
                pl.BlockSpec((1, 1, d), wsel),
            ],
            out_specs=pl.BlockSpec(memory_space=pl.ANY),
            scratch_shapes=[
                pltpu.VMEM((2, te, d), BF16), pltpu.VMEM((2, te, d), BF16),
                pltpu.SemaphoreType.DMA((2,)), pltpu.SemaphoreType.DMA((2,)),
                pltpu.VMEM((d, two_ff), BF16), pltpu.VMEM((dff, d), BF16),
            ],
        ),
        out_shape=jax.ShapeDtypeStruct(xl.shape, BF16),
        input_output_aliases={4: 0},
        compiler_params=_params(("arbitrary",)),
        name="moe_ffn",
    )(tile_expert, n_used, src_chunk, dst_chunk, xl, w_gate_up.reshape(ne, d, two_ff),
      b_gate_up.reshape(ne, 1, two_ff), w_down.reshape(ne, dff, d), b_down.reshape(ne, 1, d))


def _comb_kernel(yl_ref, route_ref, x1_ref, mod_ref, fw_ref, o_ref, *, d, tm, rmax, final_norm):
    route = route_ref[...]
    r_iota = lax.broadcasted_iota(jnp.int32, (1, rmax), 1).astype(F32)
    wm = jnp.zeros((tm, rmax), F32)
    for k in range(TOP_K):
        wm = wm + jnp.where(r_iota == route[:, k:k + 1], route[:, 2 * TOP_K + k:2 * TOP_K + k + 1], 0.0)
    out = _dot(wm.astype(BF16), yl_ref[...])
    x2 = x1_ref[...] + mod_ref[0, :, 5 * d:6 * d] * out
    if final_norm:
        x2 = x2 * lax.rsqrt(jnp.mean(x2 * x2, axis=-1, keepdims=True) + EPS) * fw_ref[...]
    o_ref[...] = x2


def _combine(yl, route, x1, mod3, mod_row, final_w, tile0, n_tiles, tm, final_norm):
    d = x1.shape[1]
    rmax = _sorted_rows(tm)
    return pl.pallas_call(
        functools.partial(_comb_kernel, d=d, tm=tm, rmax=rmax, final_norm=final_norm),
        grid=(n_tiles,),
        in_specs=[
            pl.BlockSpec((rmax, d), lambda i: (tile0 + i, 0)),
            pl.BlockSpec((tm, LANES), lambda i: (tile0 + i, 0)),
            pl.BlockSpec((tm, d), lambda i: (tile0 + i, 0)),
            pl.BlockSpec((1, 1, 6 * d), lambda i: (mod_row(tile0 + i), 0, 0)),
            pl.BlockSpec((1, d), lambda i: (0, 0)),
        ],
        out_specs=pl.BlockSpec((tm, d), lambda i: (i, 0)),
        out_shape=jax.ShapeDtypeStruct((n_tiles * tm, d), F32),
        compiler_params=_params(("parallel",)),
        name="moe_combine",
    )(yl, route, x1, mod3, final_w.reshape(1, d))


@functools.lru_cache(maxsize=None)
def _dft_mats(l):
    f = np.arange(l, dtype=np.int64)
    ang = ((f[:, None] * f[None, :]) % (2 * l)).astype(np.float64) * (math.pi / l)
    return np.cos(ang).astype(np.float32), np.sin(ang).astype(np.float32)


@functools.lru_cache(maxsize=None)
def _filter_features(l):
    f32 = np.float32
    t = np.linspace(0.0, 1.0, l, dtype=f32)[:, None]
    bands = (HY_EMB - 1) // 2
    w = f32(2.0 * math.pi) * np.arange(l, dtype=f32)[:, None] / f32(l)
    f = np.linspace(1e-4, bands - 1, bands, dtype=f32)[None, :]
    z = np.concatenate([t, np.cos(f * w), -np.sin(f * w)], axis=-1).astype(f32)
    return np.pad(z, ((0, 0), (0, LANES - HY_EMB))), t


@functools.lru_cache(maxsize=None)
def _rope_tables(rows):
    f32 = np.float32
    row = np.repeat(np.arange(rows, dtype=f32), GRID_W)
    col = np.tile(np.arange(GRID_W, dtype=f32), rows)
    half = RET_DK // 4
    inv = (f32(ROPE_BASE) ** (-np.arange(half, dtype=f32) / f32(half))).astype(f32)
    ang = np.concatenate([row[:, None] * inv, col[:, None] * inv], axis=-1)
    ang = np.concatenate([ang, ang], axis=-1).astype(f32)
    sign = np.where(np.arange(RET_DK) < RET_DK // 2, -1.0, 1.0).astype(f32)
    return np.cos(ang), np.sin(ang) * sign


@functools.lru_cache(maxsize=None)
def _decay_rates():
    return np.linspace(math.log(HY_TARGET) / HY_SLOW_PCT, math.log(HY_TARGET) / HY_FAST_PCT,
                       HY_W, dtype=np.float32)[None, :]


def _pad2(a, rows, cols):
    return jnp.pad(a, ((0, rows - a.shape[0]), (0, cols - a.shape[1])))


def _ffn_plan(seg, rmax, te, n_ffn_tiles):
    nt, ne = seg.shape
    cum_t = jnp.cumsum(seg, axis=0)
    tot = cum_t[-1]
    padded = ((tot + te - 1) // te) * te
    ends = jnp.cumsum(padded)
    offs = ends - padded
    n_used = ends[-1] // te
    expert_of = lambda rows: jnp.minimum(jnp.sum(ends[None, :] <= rows[:, None], axis=1), ne - 1)
    tiles = jnp.arange(n_ffn_tiles, dtype=jnp.int32)
    tile_expert = expert_of(jnp.minimum(tiles, n_used - 1) * te).astype(jnp.int32)

    rows = jnp.arange(n_ffn_tiles * te // ROW_ALIGN, dtype=jnp.int32) * ROW_ALIGN
    hot_e = expert_of(rows)[:, None] == jnp.arange(ne)[None, :]
    pick_e = lambda tab: jnp.sum(jnp.where(hot_e[:, None, :], tab[None], 0), axis=2)
    r_s = rows - pick_e(offs[None, :])[:, 0]
    cum_sel = pick_e(cum_t)
    i_s = jnp.minimum(jnp.sum(cum_sel <= r_s[:, None], axis=1), nt - 1)
    hot_i = i_s[:, None] == jnp.arange(nt)[None, :]
    seg_base = (jnp.arange(nt)[:, None] * rmax + jnp.cumsum(seg, axis=1) - seg) - (cum_t - seg)
    src_row = r_s + jnp.sum(jnp.where(hot_i, pick_e(seg_base), 0), axis=1)
    valid = (r_s < pick_e(tot[None, :])[:, 0]) & (rows < ends[-1])
    src_chunk = jnp.where(valid, src_row // ROW_ALIGN, rmax // ROW_ALIGN - 1).astype(jnp.int32)
    pad_rank = jnp.cumsum(jnp.where(valid, 0, 1)) - 1
    spare = nt * rmax // ROW_ALIGN + jnp.minimum(pad_rank, _spare_chunks(te) - 1)
    dst_chunk = jnp.where(valid, src_row // ROW_ALIGN, spare).astype(jnp.int32)
    return tile_expert, n_used.reshape(1).astype(jnp.int32), src_chunk, dst_chunk


def _spare_chunks(te):
    return N_EXPERTS * (te // ROW_ALIGN - 1)


def kernel(x_prompt, x_sample, state_ret, c, c_ctx, ada_w, ada_b, norm_mix_w, w_in, conv_w, conv_b,
           f_w1, f_b1, f_w2, f_b2, f_w3, f_b3, f_freq, f_wout, hy_skip, ret_decay_logit,
           w_branch_a, w_branch_b, w_out, norm_ffn_w, router_w, router_b, w_gate_up, b_gate_up,
           w_down, b_down, final_norm_w):
    bp, lp, d = x_prompt.shape
    bs, ls, _ = x_sample.shape
    depth = ada_w.shape[0]
    tp, ts = bp * lp, bs * ls
    t = tp + ts
    assert ls % lp == 0 and tp % ls == 0 and lp % RET_CHUNK == 0 and ls % GRID_W == 0
    tm = 512 if (tp % 512 == 0 and ls % 512 == 0) else lp
    te = 512
    nt = t // tm
    rmax = _sorted_rows(tm)
    n_ffn_tiles = -(-(t * TOP_K + nt * N_EXPERTS * (ROW_ALIGN - 1)) // te) + N_EXPERTS

    dft = {l: tuple(jnp.asarray(a).astype(BF16) for a in _dft_mats(l)) for l in (lp, ls)}
    feats = {l: tuple(jnp.asarray(a) for a in _filter_features(l)) for l in (lp, ls)}
    rope_tabs = tuple(jnp.asarray(a) for a in _rope_tables(ls // GRID_W))
    deltas = jnp.asarray(_decay_rates())

    cvec = jnp.zeros((8, d), F32).at[:bs].set(c).at[bs].set(c_ctx)
    mod3 = _modulation(cvec, ada_w, ada_b).reshape(depth * 8, 1, 6 * d)
    n_p, n_s = tp // tm, ts // tm

    x = (x_prompt.reshape(tp, d), x_sample.reshape(ts, d))
    states = []
    for l in range(depth):
        mod_row = lambda i, l=l: l * 8 + jnp.where(i < n_p, bs, jnp.maximum(i - n_p, 0) // (ls // tm))
        w_in_bf = w_in[l].astype(BF16)
        proj = (_in_proj(x[0], norm_mix_w[l], mod3, mod_row, w_in_bf, tm),
                _in_proj(x[1], norm_mix_w[l], mod3, lambda i, f=mod_row: f(i + n_p), w_in_bf, tm))

        w1 = _pad2(f_w1[l], LANES, LANES)
        w2 = _pad2(f_w2[l], LANES, LANES)
        w3 = _pad2(f_w3[l], LANES, LANES)
        wo = _pad2(f_wout[l], LANES, f_wout.shape[2])
        b1, b2, b3, fr = (_pad2(a[l][None, :], 1, LANES) for a in (f_b1, f_b2, f_b3, f_freq))
        za, zb = [], []
        for grp, (nb, ln) in enumerate(((bp, lp), (bs, ls))):
            cm, sm = dft[ln]
            taps, sums = _hyena_taps(ln, feats[ln][0], feats[ln][1], deltas, w1, b1, w2, b2, w3, b3, fr, wo)
            kc, ki, kn = _filter_spectra(ln, cm, sm, taps, sums)
            za.append(_hyena_branch(proj[grp], 0, nb, ln, cm, sm, conv_w[l], conv_b[l], kc, ki, kn, hy_skip[l]))
            zb_g, st = _retention_branch(proj[grp], 0, nb, ln, ret_decay_logit[l],
                                         rope_tabs if grp == 1 else None,
                                         state_ret if grp == 1 else None, l)
            zb.append(zb_g)
            if grp == 0:
                states.append(st)

        rw = _pad2(router_w[l], d, LANES)
        rb = jnp.full((1, LANES), -1e30, F32).at[0, :N_EXPERTS].set(router_b[l])
        x1, xl, route, seg = _mix(za, zb, proj, x, mod3, mod_row,
                                  w_branch_a[l].astype(BF16), w_branch_b[l].astype(BF16),
                                  w_out[l].astype(BF16), norm_ffn_w[l], rw, rb, tm, _spare_chunks(te) * ROW_ALIGN)
        tile_expert, n_used, src_chunk, dst_chunk = _ffn_plan(seg[:, 0, :N_EXPERTS].astype(jnp.int32), rmax, te,
                                                              n_ffn_tiles)
        yl = _expert_ffn(xl, tile_expert, n_used, src_chunk, dst_chunk, w_gate_up, b_gate_up, w_down, b_down,
                         l, te)
        last = l == depth - 1
        x = (_combine(yl, route, x1, mod3, mod_row, final_norm_w, 0, n_p, tm, last),
             _combine(yl, route, x1, mod3, mod_row, final_norm_w, n_p, n_s, tm, last))

    state_ret_new = jnp.stack(states, axis=1)
    return (x[0].reshape(bp, lp, d), x[1].reshape(bs, ls, d), state_ret_new)
```

```python
import functools
import math

import jax
import jax.numpy as jnp
import numpy as np
from jax import lax
from jax.experimental import pallas as pl
from jax.experimental.pallas import tpu as pltpu

F32 = jnp.float32
BF16 = jnp.bfloat16

GRID_W = 64
HY_W = 512
HY_ORDER = 2
HY_EMB = 33
HY_FAST_PCT = 0.3
HY_SLOW_PCT = 1.5
HY_TARGET = 1e-2
N_RET_HEADS = 4
RET_DK = 128
RET_DV = 256
RET_CHUNK = 256
ROPE_BASE = 10000.0
N_EXPERTS = 32
TOP_K = 4
SWIGLU_LIMIT = 7.0
SWIGLU_ALPHA = 1.702
EPS = 1e-6

LANES = 128
VMEM_LIMIT = 56 * 1024 * 1024

COL_V, COL_G, COL_GA, COL_GB = 0, 1024, 2048, 3072
COL_HV, COL_HX1, COL_HX2, COL_Q, COL_K = 4096, 4608, 5120, 5632, 6144
IN_W = 6656
IN_COL_BLOCKS = (5, 6, 7, 8, 9, 10, 11, 12, 0, 1, 2, 3, 4)


def _dot(a, b):
    return jnp.dot(a, b, preferred_element_type=F32)


def _split_bf16(a):
    hi = a.astype(BF16)
    lo = (a - hi.astype(F32)).astype(BF16)
    return hi, lo


def _dot3(a, b):
    ah, al = _split_bf16(a)
    bh, bl = _split_bf16(b)
    return _dot(ah, bh) + (_dot(ah, bl) + _dot(al, bh))


def _sigmoid(x):
    return 0.5 * jnp.tanh(0.5 * x) + 0.5


def _silu(x):
    return x * _sigmoid(x)


def _params(sem, vmem=VMEM_LIMIT):
    return pltpu.CompilerParams(dimension_semantics=sem, vmem_limit_bytes=vmem)


def _resident(shape, index_map):
    return pl.BlockSpec(shape, index_map, pipeline_mode=pl.Buffered(1))


def _mod_kernel(c_ref, w_ref, b_ref, o_ref):
    o_ref[0] = _dot3(_silu(c_ref[...]), w_ref[0]) + b_ref[0]


def _modulation(cvec, ada_w, ada_b):
    depth, d, six_d = ada_w.shape
    tn = six_d // 4
    return pl.pallas_call(
        _mod_kernel,
        grid=(depth, six_d // tn),
        in_specs=[
            pl.BlockSpec((8, d), lambda l, j: (0, 0)),
            pl.BlockSpec((1, d, tn), lambda l, j: (l, 0, j)),
            pl.BlockSpec((1, 1, tn), lambda l, j: (l, 0, j)),
        ],
        out_specs=pl.BlockSpec((1, 8, tn), lambda l, j: (l, 0, j)),
        out_shape=jax.ShapeDtypeStruct((depth, 8, six_d), F32),
        compiler_params=_params(("parallel", "parallel")),
        name="adaln_mod",
    )(cvec, ada_w, ada_b.reshape(depth, 1, six_d))


def _in_kernel(x_ref, nw_ref, mod_ref, w_ref, o_ref, *, d, cw):
    x = x_ref[...]
    y = x * lax.rsqrt(jnp.mean(x * x, axis=-1, keepdims=True) + EPS) * nw_ref[...]
    shift = mod_ref[0, :, 0:d]
    scale = mod_ref[0, :, d:2 * d]
    u = (y * (1.0 + scale) + shift).astype(BF16)
    for j, src in enumerate(IN_COL_BLOCKS):
        o_ref[:, j * cw:(j + 1) * cw] = _dot(u, w_ref[:, src * cw:(src + 1) * cw]).astype(BF16)


def _in_proj(x, norm_w, mod3, mod_row, w_in_bf, tm):
    t, d = x.shape
    return pl.pallas_call(
        functools.partial(_in_kernel, d=d, cw=IN_W // len(IN_COL_BLOCKS)),
        grid=(t // tm,),
        in_specs=[
            pl.BlockSpec((tm, d), lambda i: (i, 0)),
            pl.BlockSpec((1, d), lambda i: (0, 0)),
            pl.BlockSpec((1, 1, 2 * d), lambda i: (mod_row(i), 0, 0)),
            _resident((d, IN_W), lambda i: (0, 0)),
        ],
        out_specs=pl.BlockSpec((tm, IN_W), lambda i: (i, 0)),
        out_shape=jax.ShapeDtypeStruct((t, IN_W), BF16),
        compiler_params=_params(("parallel",)),
        name="in_proj",
    )(x, norm_w.reshape(1, d), mod3, w_in_bf)


def _filt_kernel(z_ref, t_ref, w1, b1, w2, b2, w3, b3, fr, wo, dl_ref, h_ref, s_ref, *, tl):
    i = pl.program_id(0)
    freq = fr[...]
    h = jnp.sin(freq * (_dot3(z_ref[...], w1[...]) + b1[...]))
    h = jnp.sin(freq * (_dot3(h, w2[...]) + b2[...]))
    h = jnp.sin(freq * (_dot3(h, w3[...]) + b3[...]))
    h = _dot3(h, wo[...])
    decay = jnp.exp(-t_ref[...] * jnp.abs(dl_ref[...]))
    h = h * jnp.concatenate([decay] * (2 * HY_ORDER), axis=1)
    h_ref[...] = h
    row = i * tl + lax.broadcasted_iota(jnp.int32, (tl, 1), 0)
    col = lax.broadcasted_iota(jnp.int32, (1, 2 * HY_ORDER * HY_W), 1)
    is_bwd = ((col // HY_W) % 2) == 1
    part = jnp.sum(jnp.where((row == 0) & is_bwd, 0.0, jnp.abs(h)), axis=0, keepdims=True)

    @pl.when(i == 0)
    def _():
        s_ref[...] = jnp.zeros_like(s_ref)

    s_ref[...] += part


def _hyena_taps(l, zfeat, tcol, deltas, w1, b1, w2, b2, w3, b3, freq, wout):
    tl = min(l, 256)
    nw = 2 * HY_ORDER * HY_W
    full = lambda shape: pl.BlockSpec(shape, lambda i: (0,) * len(shape))
    return pl.pallas_call(
        functools.partial(_filt_kernel, tl=tl),
        grid=(l // tl,),
        in_specs=[
            pl.BlockSpec((tl, LANES), lambda i: (i, 0)),
            pl.BlockSpec((tl, 1), lambda i: (i, 0)),
            full((LANES, LANES)), full((1, LANES)),
            full((LANES, LANES)), full((1, LANES)),
            full((LANES, LANES)), full((1, LANES)),
            full((1, LANES)),
            full((LANES, nw)),
            full((1, HY_W)),
        ],
        out_specs=[pl.BlockSpec((tl, nw), lambda i: (i, 0)), pl.BlockSpec((1, nw), lambda i: (0, 0))],
        out_shape=[jax.ShapeDtypeStruct((l, nw), F32), jax.ShapeDtypeStruct((1, nw), F32)],
        compiler_params=_params(("arbitrary",)),
        name=f"hyena_taps_{l}",
    )(zfeat, tcol, w1, b1, w2, b2, w3, b3, freq, wout, deltas)


def _fdft_kernel(c_ref, s_ref, hf_ref, hb_ref, sf_ref, sb_ref, kc_ref, ki_ref, kn_ref, *, l):
    row = lax.broadcasted_iota(jnp.int32, (l, 1), 0)
    hf = hf_ref[...]
    hb = jnp.where(row == 0, 0.0, hb_ref[...])
    inv = 1.0 / (sf_ref[...] + sb_ref[...])
    a = hf + hb
    b = hb - hf
    wf = jnp.where(row == 0, 0.5 / l, 1.0 / l)
    kc_ref[...] = _dot(c_ref[...], a.astype(BF16)) * wf * inv
    ki_ref[...] = _dot(s_ref[...], b.astype(BF16)) * wf * inv
    sgn = jnp.where(row % 2 == 0, 1.0, -1.0)
    kn_ref[...] = jnp.sum(a * sgn, axis=0, keepdims=True) * inv * (0.5 / l)


def _filter_spectra(l, cmat, smat, taps, sums):
    wt = 256
    nb = HY_W // wt
    two_w = 2 * HY_W
    return pl.pallas_call(
        functools.partial(_fdft_kernel, l=l),
        grid=(HY_ORDER, nb),
        in_specs=[
            _resident((l, l), lambda o, j: (0, 0)),
            _resident((l, l), lambda o, j: (0, 0)),
            pl.BlockSpec((l, wt), lambda o, j: (0, o * 2 * nb + j)),
            pl.BlockSpec((l, wt), lambda o, j: (0, o * 2 * nb + nb + j)),
            pl.BlockSpec((1, wt), lambda o, j: (0, o * 2 * nb + j)),
            pl.BlockSpec((1, wt), lambda o, j: (0, o * 2 * nb + nb + j)),
        ],
        out_specs=[
            pl.BlockSpec((l, wt), lambda o, j: (0, o * nb + j)),
            pl.BlockSpec((l, wt), lambda o, j: (0, o * nb + j)),
            pl.BlockSpec((1, wt), lambda o, j: (0, o * nb + j)),
        ],
        out_shape=[
            jax.ShapeDtypeStruct((l, HY_ORDER * HY_W), F32),
            jax.ShapeDtypeStruct((l, HY_ORDER * HY_W), F32),
            jax.ShapeDtypeStruct((1, HY_ORDER * HY_W), F32),
        ],
        compiler_params=_params(("parallel", "parallel")),
        name=f"hyena_spectra_{l}",
    )(cmat, smat, taps, taps, sums, sums)


def _hy_kernel(c_ref, s_ref, hv_ref, hx1_ref, hx2_ref, cw_ref, cb_ref,
               kc0, ki0, kc1, ki1, kn_ref, sk_ref, o_ref, z_scr, zb_scr, p_scr, q_scr, *, l, rc, sb):
    row = lax.broadcasted_iota(jnp.int32, (l, 1), 0)
    first = row == 0
    last = row == l - 1
    sgn = jnp.where(row % 2 == 0, 1.0, -1.0)
    cw = cw_ref[...]
    cb = cb_ref[...]

    def short_conv(ref, rows, part):
        x = ref[rows, :].astype(F32)
        xm = jnp.where(first, 0.0, pltpu.roll(x, 1, 0))
        xp = jnp.where(last, 0.0, pltpu.roll(x, l - 1, 0))
        return (xm * cw[0, part:part + 1] + x * cw[1, part:part + 1] + xp * cw[2, part:part + 1]
                + cb[part:part + 1])

    for s in range(sb):
        rows = slice(s * l, (s + 1) * l)
        z_scr[s] = short_conv(hv_ref, rows, 0)
        for o, (gate_ref, kc, ki) in enumerate(((hx1_ref, kc0, ki0), (hx2_ref, kc1, ki1))):
            z = z_scr[s]
            zb_scr[s] = z.astype(BF16)
            nyq = jnp.sum(z * sgn, axis=0, keepdims=True) * kn_ref[o:o + 1]
            for r in range(l // rc):
                sl = slice(r * rc, (r + 1) * rc)
                zc = _dot(c_ref[sl, :], zb_scr[s])
                zs = _dot(s_ref[sl, :], zb_scr[s])
                p_scr[s, sl, :] = (zc * kc[sl, :] + zs * ki[sl, :]).astype(BF16)
                q_scr[s, sl, :] = (zs * kc[sl, :] - zc * ki[sl, :]).astype(BF16)
            gate = short_conv(gate_ref, rows, o + 1)
            for r in range(l // rc):
                sl = slice(r * rc, (r + 1) * rc)
                y = _dot(c_ref[sl, :], p_scr[s]) + _dot(s_ref[sl, :], q_scr[s])
                y = y + sgn[sl] * nyq + z_scr[s, sl, :] * sk_ref[o:o + 1]
                z_scr[s, sl, :] = gate[sl] * y
        o_ref[rows, :] = z_scr[s].astype(BF16)


def _hyena_branch(proj, row_blk0, nb, l, cmat, smat, conv_w, conv_b, kc, ki, kn, skip):
    wt = 256
    nw = HY_W // wt
    rc = min(l, 512)
    sb = math.gcd(nb, max(1, 2048 // l))
    assert row_blk0 % sb == 0
    hv0, hx10, hx20 = COL_HV // wt, COL_HX1 // wt, COL_HX2 // wt
    seq = lambda c0: pl.BlockSpec((sb * l, wt), lambda j, b: (row_blk0 // sb + b, c0 + j))
    tab = lambda o: pl.BlockSpec((l, wt), lambda j, b: (0, o * nw + j))
    return pl.pallas_call(
        functools.partial(_hy_kernel, l=l, rc=rc, sb=sb),
        grid=(nw, nb // sb),
        in_specs=[
            _resident((l, l), lambda j, b: (0, 0)),
            _resident((l, l), lambda j, b: (0, 0)),
            seq(hv0), seq(hx10), seq(hx20),
            pl.BlockSpec((3, 3, wt), lambda j, b: (0, 0, j)),
            pl.BlockSpec((3, wt), lambda j, b: (0, j)),
            tab(0), tab(0), tab(1), tab(1),
            pl.BlockSpec((HY_ORDER, wt), lambda j, b: (0, j)),
            pl.BlockSpec((HY_ORDER, wt), lambda j, b: (0, j)),
        ],
        out_specs=pl.BlockSpec((sb * l, wt), lambda j, b: (b, j)),
        out_shape=jax.ShapeDtypeStruct((nb * l, HY_W), BF16),
        scratch_shapes=[
            pltpu.VMEM((sb, l, wt), F32), pltpu.VMEM((sb, l, wt), BF16),
            pltpu.VMEM((sb, l, wt), BF16), pltpu.VMEM((sb, l, wt), BF16),
        ],
        compiler_params=_params(("parallel", "parallel")),
        name=f"hyena_{l}",
    )(cmat, smat, proj, proj, proj, conv_w.reshape(3, 3, HY_W), conv_b.reshape(3, HY_W),
      kc, ki, kc, ki, kn.reshape(HY_ORDER, HY_W), skip)


def _log_sigmoid(x):
    return jnp.minimum(x, 0.0) - jnp.log1p(jnp.exp(-jnp.abs(x)))


def _ret_kernel(*refs, l, hb, use_rope, has_s0, has_acc, st_layers, st_own):
    it = iter(refs)
    dl_ref, q_ref, k_ref, v_ref, g_ref = (next(it) for _ in range(5))
    cos_ref, sin_ref = (next(it), next(it)) if use_rope else (None, None)
    s0_ref = next(it) if has_s0 else None
    if has_acc:
        next(it)
    o_ref, st_ref, kr_scr, kv_scr = (next(it) for _ in range(4))
    for other in range(st_layers):
        if other != st_own:
            st_ref[0, other] = jnp.zeros(st_ref.shape[2:], F32)

    c, dk, dv = RET_CHUNK, RET_DK, RET_DV
    nc = l // c
    unroll = min(nc, 4)
    pos = lax.broadcasted_iota(jnp.int32, (c, 1), 0).astype(F32)
    diff = pos - lax.broadcasted_iota(jnp.int32, (1, c), 1).astype(F32)

    def rope(x, r0):
        if not use_rope:
            return x
        return x * cos_ref[pl.ds(r0, c), :] + pltpu.roll(x, dk // 2, 1) * sin_ref[pl.ds(r0, c), :]

    for h in range(hb):
        qs = slice(h * dk, (h + 1) * dk)
        vs = slice(h * dv, (h + 1) * dv)
        lgf = _log_sigmoid(dl_ref[0, h])
        lgb = _log_sigmoid(dl_ref[1, h])
        dm = jnp.where(diff >= 0, jnp.exp(lgf * diff), jnp.exp(-lgb * diff))
        zeta_f = jnp.exp(lgf * (c - 1.0 - pos))
        zeta_b = jnp.exp(lgb * pos)
        xi_f = jnp.exp(lgf * (pos + 1.0))
        xi_b = jnp.exp(lgb * (c - pos))
        gc_f = jnp.exp(lgf * c)
        gc_b = jnp.exp(lgb * c)

        def kv_step(ci, carry):
            r0 = pl.multiple_of(ci * c, c)
            k = rope(k_ref[pl.ds(r0, c), qs].astype(F32), r0) * (dk ** -0.5)
            kr_scr[pl.ds(r0, c), qs] = k.astype(BF16)
            kk = jnp.concatenate([k * zeta_f, k * zeta_b], axis=1).astype(BF16)
            kv_scr[h, ci] = lax.dot_general(kk, v_ref[pl.ds(r0, c), vs], (((0,), (0,)), ((), ())),
                                            preferred_element_type=F32)
            return carry

        lax.fori_loop(0, nc, kv_step, 0, unroll=unroll)

        def fwd_scan(ci, s):
            upd = kv_scr[h, ci, 0:dk, :]
            kv_scr[h, ci, 0:dk, :] = s
            return gc_f * s + upd

        def bwd_scan(i, r):
            ci = nc - 1 - i
            upd = kv_scr[h, ci, dk:2 * dk, :]
            kv_scr[h, ci, dk:2 * dk, :] = r
            return gc_b * r + upd

        zero = jnp.zeros((dk, dv), F32)
        st_ref[0, st_own, 0, h] = lax.fori_loop(0, nc, fwd_scan, s0_ref[0, 0, 0, h] if has_s0 else zero)
        st_ref[0, st_own, 1, h] = lax.fori_loop(0, nc, bwd_scan, s0_ref[0, 0, 1, h] if has_s0 else zero)

        def out_step(ci, carry):
            r0 = pl.multiple_of(ci * c, c)
            q = rope(q_ref[pl.ds(r0, c), qs].astype(F32), r0)
            sc = lax.dot_general(q.astype(BF16), kr_scr[pl.ds(r0, c), qs], (((1,), (1,)), ((), ())),
                                 preferred_element_type=F32) * dm
            o = _dot(sc.astype(BF16), v_ref[pl.ds(r0, c), vs])
            if has_s0 or nc > 1:
                qq = jnp.concatenate([q * xi_f, q * xi_b], axis=1).astype(BF16)
                o = o + _dot(qq, kv_scr[h, ci].astype(BF16))
            o = o * lax.rsqrt(jnp.mean(o * o, axis=-1, keepdims=True) + EPS)
            o_ref[pl.ds(r0, c), vs] = (_silu(g_ref[pl.ds(r0, c), vs].astype(F32)) * o).astype(BF16)
            return carry

        lax.fori_loop(0, nc, out_step, 0, unroll=unroll)


def _retention_branch(proj, row_blk0, nb, l, decay_logit, rope_tabs, state, layer, n_state_layers=1,
                      states_acc=None):
    nh, dk, dv = N_RET_HEADS, RET_DK, RET_DV
    use_rope = rope_tabs is not None
    has_s0 = state is not None
    has_acc = states_acc is not None
    own_layer = layer if n_state_layers > 1 else 0
    st_layers = 1 if has_acc else n_state_layers
    hb = nh if l <= 512 else 1
    ng = nh // hb
    seq = lambda col, w: pl.BlockSpec((l, hb * w), lambda b, g: (row_blk0 + b, col // (hb * w) + g))
    in_specs = [
        pl.BlockSpec((2, hb, 1, 1), lambda b, g: (0, g, 0, 0)),
        seq(COL_Q, dk), seq(COL_K, dk), seq(COL_V, dv), seq(COL_G, dv),
    ]
    args = [decay_logit.reshape(2, nh, 1, 1), proj, proj, proj, proj]
    if use_rope:
        in_specs += [pl.BlockSpec((l, dk), lambda b, g: (0, 0))] * 2
        args += list(rope_tabs)
    if has_s0:
        in_specs.append(pl.BlockSpec((1, 1, 2, hb, dk, dv), lambda b, g: (b, layer, 0, g, 0, 0)))
        args.append(state)
    if has_acc:
        in_specs.append(pl.BlockSpec(memory_space=pl.ANY))
        args.append(states_acc)
    return pl.pallas_call(
        functools.partial(_ret_kernel, l=l, hb=hb, use_rope=use_rope, has_s0=has_s0, has_acc=has_acc,
                          st_layers=st_layers, st_own=0 if has_acc else own_layer),
        grid=(nb, ng),
        in_specs=in_specs,
        out_specs=[
            pl.BlockSpec((l, hb * dv), lambda b, g: (b, g)),
            pl.BlockSpec((1, st_layers, 2, hb, dk, dv),
                         lambda b, g: (b, own_layer if has_acc else 0, 0, g, 0, 0)),
        ],
        out_shape=[
            jax.ShapeDtypeStruct((nb * l, nh * dv), BF16),
            jax.ShapeDtypeStruct((nb, n_state_layers, 2, nh, dk, dv), F32),
        ],
        input_output_aliases={len(args) - 1: 1} if has_acc else {},
        scratch_shapes=[
            pltpu.VMEM((l, hb * dk), BF16),
            pltpu.VMEM((hb, l // RET_CHUNK, 2 * dk, dv), F32),
        ],
        compiler_params=_params(("parallel", "parallel")),
        name=f"retention_{l}",
    )(*args)


ROW_ALIGN = 16


def _sorted_rows(tm):
    worst = TOP_K * tm + N_EXPERTS * (ROW_ALIGN - 1) + ROW_ALIGN
    return -(-worst // LANES) * LANES


def _mix_kernel(zap_ref, zas_ref, zbp_ref, zbs_ref, gap_ref, gas_ref, gbp_ref, gbs_ref, xp_ref, xs_ref,
                mod_ref, wa_ref, wb_ref, wo_ref, nw_ref, rw_ref, rb_ref, x1_ref, xl_ref, route_ref, seg_ref,
                h_scr, pr_scr, sg_scr, *, d, tm, rmax, n_tiles, n_prompt_tiles):
    i = pl.program_id(0)

    @pl.when(i == 0)
    def _():
        h_scr[...] = jnp.zeros_like(h_scr)
        pr_scr[...] = jnp.zeros_like(pr_scr)
        sg_scr[...] = jnp.zeros_like(sg_scr)

    @pl.when(i > n_tiles)
    def _():
        xl_ref[...] = jnp.zeros_like(xl_ref)

    @pl.when(i <= n_tiles)
    def _():
        _mix_step(zap_ref, zas_ref, zbp_ref, zbs_ref, gap_ref, gas_ref, gbp_ref, gbs_ref, xp_ref, xs_ref,
                  mod_ref, wa_ref, wb_ref, wo_ref, nw_ref, rw_ref, rb_ref, x1_ref, xl_ref, route_ref, seg_ref,
                  h_scr, pr_scr, sg_scr, d=d, tm=tm, rmax=rmax, n_tiles=n_tiles, n_prompt_tiles=n_prompt_tiles)


def _mix_step(zap_ref, zas_ref, zbp_ref, zbs_ref, gap_ref, gas_ref, gbp_ref, gbs_ref, xp_ref, xs_ref,
              mod_ref, wa_ref, wb_ref, wo_ref, nw_ref, rw_ref, rb_ref, x1_ref, xl_ref, route_ref, seg_ref,
              h_scr, pr_scr, sg_scr, *, d, tm, rmax, n_tiles, n_prompt_tiles):
    i = pl.program_id(0)
    r_col = lax.broadcasted_iota(jnp.int32, (rmax, 1), 0).astype(F32)
    in_seg = jnp.logical_and(r_col >= sg_scr[0:1], r_col < sg_scr[0:1] + sg_scr[1:2])
    seg_hot = jnp.where(in_seg, 1.0, 0.0).astype(BF16)
    row_of = _dot(jnp.concatenate([seg_hot, seg_hot], axis=1), pr_scr[...])
    sel = jnp.where(row_of == r_col + 1.0, 1.0, 0.0).astype(BF16)
    xl_ref[...] = _dot(sel, h_scr[...]).astype(BF16)

    is_prompt = jnp.minimum(i, n_tiles - 1) < n_prompt_tiles
    pick = lambda p_ref, s_ref: jnp.where(is_prompt, p_ref[...], s_ref[...])
    za, zb = pick(zap_ref, zas_ref), pick(zbp_ref, zbs_ref)
    ga, gb = pick(gap_ref, gas_ref), pick(gbp_ref, gbs_ref)
    merged = (_sigmoid(ga.astype(F32)) * _dot(za, wa_ref[...])
              + _sigmoid(gb.astype(F32)) * _dot(zb, wb_ref[...]))
    gate1 = mod_ref[0, :, 2 * d:3 * d]
    shift2 = mod_ref[0, :, 3 * d:4 * d]
    scale2 = mod_ref[0, :, 4 * d:5 * d]
    x1 = pick(xp_ref, xs_ref) + gate1 * _dot(merged.astype(BF16), wo_ref[...])
    x1_ref[...] = x1
    hn = x1 * lax.rsqrt(jnp.mean(x1 * x1, axis=-1, keepdims=True) + EPS) * nw_ref[...]
    hmo = hn * (1.0 + scale2) + shift2

    logits = _dot3(hmo, rw_ref[...]) + rb_ref[...]
    lane = lax.broadcasted_iota(jnp.int32, (1, LANES), 1)
    idxs, vals = [], []
    for _ in range(TOP_K):
        m = jnp.max(logits, axis=1, keepdims=True)
        ik = jnp.min(jnp.where(logits == m, lane, LANES), axis=1, keepdims=True)
        idxs.append(ik)
        vals.append(m)
        logits = jnp.where(lane == ik, -jnp.inf, logits)
    exps = [jnp.exp(v - vals[0]) for v in vals]
    denom = exps[0] + exps[1] + exps[2] + exps[3]

    hot = jnp.zeros((tm, LANES), F32)
    for ik in idxs:
        hot = hot + jnp.where(lane == ik, 1.0, 0.0)
    r_i = lax.broadcasted_iota(jnp.int32, (tm, tm), 0)
    c_i = lax.broadcasted_iota(jnp.int32, (tm, tm), 1)
    below = jnp.where(c_i < r_i, 1.0, 0.0).astype(BF16)
    before = _dot(below, hot.astype(BF16))
    seg = jnp.ceil(jnp.sum(hot, axis=0, keepdims=True) * (1.0 / ROW_ALIGN)) * ROW_ALIGN
    seg_ref[0] = seg
    e_r = lax.broadcasted_iota(jnp.int32, (LANES, LANES), 0)
    e_c = lax.broadcasted_iota(jnp.int32, (LANES, LANES), 1)
    seg_start = _dot3(jnp.broadcast_to(seg, (8, LANES)), jnp.where(e_r < e_c, 1.0, 0.0))[0:1]
    base = seg_start + before

    route = jnp.zeros((tm, LANES), F32)
    for k in range(TOP_K):
        srow = jnp.sum(jnp.where(lane == idxs[k], base, 0.0), axis=1, keepdims=True)
        route = route + jnp.where(lane == k, srow, 0.0)
        route = route + jnp.where(lane == TOP_K + k, idxs[k].astype(F32), 0.0)
        route = route + jnp.where(lane == 2 * TOP_K + k, exps[k] / denom, 0.0)
    route_ref[...] = route

    h_scr[...] = hmo.astype(BF16)
    pair_row = jnp.where(hot > 0.0, base + 1.0, 0.0).T
    pr_scr[...] = jnp.concatenate(_split_bf16(pair_row), axis=0)
    sg_scr[0:1] = seg_start
    sg_scr[1:2] = seg


def _mix(za, zb, proj, x, mod3, mod_row, wa, wb, wo, norm_w, rw, rb, tm, spare_rows):
    d = x[0].shape[1]
    n_p = x[0].shape[0] // tm
    n_s = x[1].shape[0] // tm
    nt = n_p + n_s
    t = nt * tm
    rmax = _sorted_rows(tm)
    spare_blocks = -(-spare_rows // rmax)
    vw = N_RET_HEADS * RET_DV
    cur = lambda i: jnp.minimum(i, nt - 1)
    prev = lambda i: jnp.maximum(i - 1, 0)
    pair = lambda w, col: [pl.BlockSpec((tm, w), lambda i: (jnp.minimum(cur(i), n_p - 1), col)),
                           pl.BlockSpec((tm, w), lambda i: (jnp.maximum(cur(i) - n_p, 0), col))]
    const = lambda i: (0, 0)
    return pl.pallas_call(
        functools.partial(_mix_kernel, d=d, tm=tm, rmax=rmax, n_tiles=nt, n_prompt_tiles=n_p),
        grid=(nt + 1 + spare_blocks,),
        in_specs=[
            *pair(HY_W, 0), *pair(vw, 0), *pair(d, COL_GA // d), *pair(d, COL_GB // d), *pair(d, 0),
            pl.BlockSpec((1, 1, 6 * d), lambda i: (mod_row(cur(i)), 0, 0)),
            _resident((HY_W, d), const), _resident((vw, d), const), _resident((d, d), const),
            pl.BlockSpec((1, d), const),
            _resident((d, LANES), const), pl.BlockSpec((1, LANES), const),
        ],
        out_specs=[
            pl.BlockSpec((tm, d), lambda i: (cur(i), 0)),
            pl.BlockSpec((rmax, d), lambda i: (prev(i), 0)),
            pl.BlockSpec((tm, LANES), lambda i: (cur(i), 0)),
            pl.BlockSpec((1, 1, LANES), lambda i: (cur(i), 0, 0)),
        ],
        out_shape=[
            jax.ShapeDtypeStruct((t, d), F32),
            jax.ShapeDtypeStruct(((nt + spare_blocks) * rmax, d), BF16),
            jax.ShapeDtypeStruct((t, LANES), F32),
            jax.ShapeDtypeStruct((nt, 1, LANES), F32),
        ],
        scratch_shapes=[pltpu.VMEM((tm, d), BF16), pltpu.VMEM((2 * LANES, tm), BF16),
                        pltpu.VMEM((2, LANES), F32)],
        compiler_params=_params(("arbitrary",)),
        name="merge_router",
    )(*za, *zb, *proj, *proj, *x, mod3, wa, wb, wo, norm_w.reshape(1, d), rw, rb)


def _ffn_kernel(te_ref, nu_ref, src_ref, dst_ref, xl_ref, wgu_ref, bgu_ref, wd_ref, bd_ref, yl_ref,
                xbuf, ybuf, gsem, ssem, wgu_scr, wd_scr, *, dff, te):
    j = pl.program_id(0)
    nu = nu_ref[0]
    cpt = te // ROW_ALIGN
    slot = j % 2

    def hbm_chunk(ci):
        return pl.ds(pl.multiple_of(ci * ROW_ALIGN, ROW_ALIGN), ROW_ALIGN)

    def start_gather(tile, buf):
        for ci in range(cpt):
            pltpu.make_async_copy(xl_ref.at[hbm_chunk(src_ref[tile * cpt + ci])],
                                  xbuf.at[buf, pl.ds(ci * ROW_ALIGN, ROW_ALIGN)], gsem.at[buf]).start()

    def wait_gather(buf):
        pltpu.make_async_copy(xl_ref.at[pl.ds(0, te)], xbuf.at[buf], gsem.at[buf]).wait()

    def start_scatter(tile, buf):
        for ci in range(cpt):
            pltpu.make_async_copy(ybuf.at[buf, pl.ds(ci * ROW_ALIGN, ROW_ALIGN)],
                                  yl_ref.at[hbm_chunk(dst_ref[tile * cpt + ci])], ssem.at[buf]).start()

    def wait_scatter(buf):
        pltpu.make_async_copy(ybuf.at[buf], yl_ref.at[pl.ds(0, te)], ssem.at[buf]).wait()

    @pl.when(j < nu)
    def _():
        @pl.when(j == 0)
        def _():
            start_gather(0, 0)

        start_gather(j + 1, 1 - slot)
        wait_gather(slot)

        @pl.when(j >= 2)
        def _():
            wait_scatter(slot)

        @pl.when(jnp.logical_or(j == 0, te_ref[j] != te_ref[jnp.maximum(j - 1, 0)]))
        def _():
            wgu_scr[...] = wgu_ref[0].astype(BF16)
            wd_scr[...] = wd_ref[0].astype(BF16)

        hgu = _dot(xbuf[slot], wgu_scr[...]) + bgu_ref[0]
        gate = jnp.minimum(hgu[:, :dff], SWIGLU_LIMIT)
        up = jnp.clip(hgu[:, dff:], -SWIGLU_LIMIT, SWIGLU_LIMIT)
        act = gate * _sigmoid(SWIGLU_ALPHA * gate) * (up + 1.0)
        ybuf[slot] = (_dot(act.astype(BF16), wd_scr[...]) + bd_ref[0]).astype(BF16)
        start_scatter(j, slot)

        @pl.when(j == nu - 1)
        def _():
            wait_gather(1 - slot)

            @pl.when(j >= 1)
            def _():
                wait_scatter(1 - slot)
            wait_scatter(slot)


def _expert_ffn(xl, tile_expert, n_used, src_chunk, dst_chunk, w_gate_up, b_gate_up, w_down, b_down, layer, te):
    _, d = xl.shape
    depth, n_exp, _, two_ff = w_gate_up.shape
    ne = depth * n_exp
    dff = two_ff // 2
    wsel = lambda j, tex, nu, src, dst: (layer * n_exp + tex[j], 0, 0)
    return pl.pallas_call(
        functools.partial(_ffn_kernel, dff=dff, te=te),
        grid_spec=pltpu.PrefetchScalarGridSpec(
            num_scalar_prefetch=4,
            grid=(tile_expert.shape[0],),
            in_specs=[
                pl.BlockSpec(memory_space=pl.ANY),
                pl.BlockSpec((1, d, two_ff), wsel),
                pl.BlockSpec((1, 1, two_ff), wsel),
                pl.BlockSpec((1, dff, d), wsel),
                pl.BlockSpec((1, 1, d), wsel),
            ],
            out_specs=pl.BlockSpec(memory_space=pl.ANY),
            scratch_shapes=[
                pltpu.VMEM((2, te, d), BF16), pltpu.VMEM((2, te, d), BF16),
                pltpu.SemaphoreType.DMA((2,)), pltpu.SemaphoreType.DMA((2,)),
                pltpu.VMEM((d, two_ff), BF16), pltpu.VMEM((dff, d), BF16),
            ],
        ),
        out_shape=jax.ShapeDtypeStruct(xl.shape, BF16),
        input_output_aliases={4: 0},
        compiler_params=_params(("arbitrary",)),
        name="moe_ffn",
    )(tile_expert, n_used, src_chunk, dst_chunk, xl, w_gate_up.reshape(ne, d, two_ff),
      b_gate_up.reshape(ne, 1, two_ff), w_down.reshape(ne, dff, d), b_down.reshape(ne, 1, d))


def _comb_kernel(yl_ref, route_ref, x1_ref, mod_ref, fw_ref, o_ref, *, d, tm, rmax, final_norm):
    route = route_ref[...]
    r_iota = lax.broadcasted_iota(jnp.int32, (1, rmax), 1).astype(F32)
    wm = jnp.zeros((tm, rmax), F32)
    for k in range(TOP_K):
        wm = wm + jnp.where(r_iota == route[:, k:k + 1], route[:, 2 * TOP_K + k:2 * TOP_K + k + 1], 0.0)
    out = _dot(wm.astype(BF16), yl_ref[...])
    x2 = x1_ref[...] + mod_ref[0, :, 5 * d:6 * d] * out
    if final_norm:
        x2 = x2 * lax.rsqrt(jnp.mean(x2 * x2, axis=-1, keepdims=True) + EPS) * fw_ref[...]
    o_ref[...] = x2


def _combine(yl, route, x1, mod3, mod_row, final_w, tile0, n_tiles, tm, final_norm):
    d = x1.shape[1]
    rmax = _sorted_rows(tm)
    return pl.pallas_call(
        functools.partial(_comb_kernel, d=d, tm=tm, rmax=rmax, final_norm=final_norm),
        grid=(n_tiles,),
        in_specs=[
            pl.BlockSpec((rmax, d), lambda i: (tile0 + i, 0)),
            pl.BlockSpec((tm, LANES), lambda i: (tile0 + i, 0)),
            pl.BlockSpec((tm, d), lambda i: (tile0 + i, 0)),
            pl.BlockSpec((1, 1, 6 * d), lambda i: (mod_row(tile0 + i), 0, 0)),
            pl.BlockSpec((1, d), lambda i: (0, 0)),
        ],
        out_specs=pl.BlockSpec((tm, d), lambda i: (i, 0)),
        out_shape=jax.ShapeDtypeStruct((n_tiles * tm, d), F32),
        compiler_params=_params(("parallel",)),
        name="moe_combine",
    )(yl, route, x1, mod3, final_w.reshape(1, d))


@functools.lru_cache(maxsize=None)
def _dft_mats(l):
    f = np.arange(l, dtype=np.int64)
    ang = ((f[:, None] * f[None, :]) % (2 * l)).astype(np.float64) * (math.pi / l)
    return np.cos(ang).astype(np.float32), np.sin(ang).astype(np.float32)


@functools.lru_cache(maxsize=None)
def _filter_features(l):
    f32 = np.float32
    t = np.linspace(0.0, 1.0, l, dtype=f32)[:, None]
    bands = (HY_EMB - 1) // 2
    w = f32(2.0 * math.pi) * np.arange(l, dtype=f32)[:, None] / f32(l)
    f = np.linspace(1e-4, bands - 1, bands, dtype=f32)[None, :]
    z = np.concatenate([t, np.cos(f * w), -np.sin(f * w)], axis=-1).astype(f32)
    return np.pad(z, ((0, 0), (0, LANES - HY_EMB))), t


@functools.lru_cache(maxsize=None)
def _rope_tables(rows):
    f32 = np.float32
    row = np.repeat(np.arange(rows, dtype=f32), GRID_W)
    col = np.tile(np.arange(GRID_W, dtype=f32), rows)
    half = RET_DK // 4
    inv = (f32(ROPE_BASE) ** (-np.arange(half, dtype=f32) / f32(half))).astype(f32)
    ang = np.concatenate([row[:, None] * inv, col[:, None] * inv], axis=-1)
    ang = np.concatenate([ang, ang], axis=-1).astype(f32)
    sign = np.where(np.arange(RET_DK) < RET_DK // 2, -1.0, 1.0).astype(f32)
    return np.cos(ang), np.sin(ang) * sign


@functools.lru_cache(maxsize=None)
def _decay_rates():
    return np.linspace(math.log(HY_TARGET) / HY_SLOW_PCT, math.log(HY_TARGET) / HY_FAST_PCT,
                       HY_W, dtype=np.float32)[None, :]


def _pad2(a, rows, cols):
    return jnp.pad(a, ((0, rows - a.shape[0]), (0, cols - a.shape[1])))


def _ffn_plan(seg, rmax, te, n_ffn_tiles):
    nt, ne = seg.shape
    cum_t = jnp.cumsum(seg, axis=0)
    tot = cum_t[-1]
    padded = ((tot + te - 1) // te) * te
    ends = jnp.cumsum(padded)
    offs = ends - padded
    n_used = ends[-1] // te
    expert_of = lambda rows: jnp.minimum(jnp.sum(ends[None, :] <= rows[:, None], axis=1), ne - 1)
    tiles = jnp.arange(n_ffn_tiles, dtype=jnp.int32)
    tile_expert = expert_of(jnp.minimum(tiles, n_used - 1) * te).astype(jnp.int32)

    rows = jnp.arange(n_ffn_tiles * te // ROW_ALIGN, dtype=jnp.int32) * ROW_ALIGN
    hot_e = expert_of(rows)[:, None] == jnp.arange(ne)[None, :]
    pick_e = lambda tab: jnp.sum(jnp.where(hot_e[:, None, :], tab[None], 0), axis=2)
    r_s = rows - pick_e(offs[None, :])[:, 0]
    cum_sel = pick_e(cum_t)
    i_s = jnp.minimum(jnp.sum(cum_sel <= r_s[:, None], axis=1), nt - 1)
    hot_i = i_s[:, None] == jnp.arange(nt)[None, :]
    seg_base = (jnp.arange(nt)[:, None] * rmax + jnp.cumsum(seg, axis=1) - seg) - (cum_t - seg)
    src_row = r_s + jnp.sum(jnp.where(hot_i, pick_e(seg_base), 0), axis=1)
    valid = (r_s < pick_e(tot[None, :])[:, 0]) & (rows < ends[-1])
    src_chunk = jnp.where(valid, src_row // ROW_ALIGN, rmax // ROW_ALIGN - 1).astype(jnp.int32)
    pad_rank = jnp.cumsum(jnp.where(valid, 0, 1)) - 1
    spare = nt * rmax // ROW_ALIGN + jnp.minimum(pad_rank, _spare_chunks(te) - 1)
    dst_chunk = jnp.where(valid, src_row // ROW_ALIGN, spare).astype(jnp.int32)
    return tile_expert, n_used.reshape(1).astype(jnp.int32), src_chunk, dst_chunk


def _spare_chunks(te):
    return N_EXPERTS * (te // ROW_ALIGN - 1)


def kernel(x_prompt, x_sample, state_ret, c, c_ctx, ada_w, ada_b, norm_mix_w, w_in, conv_w, conv_b,
           f_w1, f_b1, f_w2, f_b2, f_w3, f_b3, f_freq, f_wout, hy_skip, ret_decay_logit,
           w_branch_a, w_branch_b, w_out, norm_ffn_w, router_w, router_b, w_gate_up, b_gate_up,
           w_down, b_down, final_norm_w):
    bp, lp, d = x_prompt.shape
    bs, ls, _ = x_sample.shape
    depth = ada_w.shape[0]
    tp, ts = bp * lp, bs * ls
    t = tp + ts
    assert ls % lp == 0 and tp % ls == 0 and lp % RET_CHUNK == 0 and ls % GRID_W == 0
    tm = 512 if (tp % 512 == 0 and ls % 512 == 0) else lp
    te = 512
    nt = t // tm
    rmax = _sorted_rows(tm)
    n_ffn_tiles = -(-(t * TOP_K + nt * N_EXPERTS * (ROW_ALIGN - 1)) // te) + N_EXPERTS

    dft = {l: tuple(jnp.asarray(a).astype(BF16) for a in _dft_mats(l)) for l in (lp, ls)}
    feats = {l: tuple(jnp.asarray(a) for a in _filter_features(l)) for l in (lp, ls)}
    rope_tabs = tuple(jnp.asarray(a) for a in _rope_tables(ls // GRID_W))
    deltas = jnp.asarray(_decay_rates())

    cvec = jnp.zeros((8, d), F32).at[:bs].set(c).at[bs].set(c_ctx)
    mod3 = _modulation(cvec, ada_w, ada_b).reshape(depth * 8, 1, 6 * d)
    n_p, n_s = tp // tm, ts // tm

    x = (x_prompt.reshape(tp, d), x_sample.reshape(ts, d))
    states = None
    for l in range(depth):
        mod_row = lambda i, l=l: l * 8 + jnp.where(i < n_p, bs, jnp.maximum(i - n_p, 0) // (ls // tm))
        w_in_bf = w_in[l].astype(BF16)
        proj = (_in_proj(x[0], norm_mix_w[l], mod3, mod_row, w_in_bf, tm),
                _in_proj(x[1], norm_mix_w[l], mod3, lambda i, f=mod_row: f(i + n_p), w_in_bf, tm))

        w1 = _pad2(f_w1[l], LANES, LANES)
        w2 = _pad2(f_w2[l], LANES, LANES)
        w3 = _pad2(f_w3[l], LANES, LANES)
        wo = _pad2(f_wout[l], LANES, f_wout.shape[2])
        b1, b2, b3, fr = (_pad2(a[l][None, :], 1, LANES) for a in (f_b1, f_b2, f_b3, f_freq))
        za, zb = [], []
        for grp, (nb, ln) in enumerate(((bp, lp), (bs, ls))):
            cm, sm = dft[ln]
            taps, sums = _hyena_taps(ln, feats[ln][0], feats[ln][1], deltas, w1, b1, w2, b2, w3, b3, fr, wo)
            kc, ki, kn = _filter_spectra(ln, cm, sm, taps, sums)
            za.append(_hyena_branch(proj[grp], 0, nb, ln, cm, sm, conv_w[l], conv_b[l], kc, ki, kn, hy_skip[l]))
            if grp == 0:
                zb_g, states = _retention_branch(proj[grp], 0, nb, ln, ret_decay_logit[l], None, None, l,
                                                 depth, states)
            else:
                zb_g, _ = _retention_branch(proj[grp], 0, nb, ln, ret_decay_logit[l], rope_tabs, state_ret, l)
            zb.append(zb_g)

        rw = _pad2(router_w[l], d, LANES)
        rb = jnp.full((1, LANES), -1e30, F32).at[0, :N_EXPERTS].set(router_b[l])
        x1, xl, route, seg = _mix(za, zb, proj, x, mod3, mod_row,
                                  w_branch_a[l].astype(BF16), w_branch_b[l].astype(BF16),
                                  w_out[l].astype(BF16), norm_ffn_w[l], rw, rb, tm, _spare_chunks(te) * ROW_ALIGN)
        tile_expert, n_used, src_chunk, dst_chunk = _ffn_plan(seg[:, 0, :N_EXPERTS].astype(jnp.int32), rmax, te,
                                                              n_ffn_tiles)
        yl = _expert_ffn(xl, tile_expert, n_used, src_chunk, dst_chunk, w_gate_up, b_gate_up, w_down, b_down,
                         l, te)
        last = l == depth - 1
        x = (_combine(yl, route, x1, mod3, mod_row, final_norm_w, 0, n_p, tm, last),
             _combine(yl, route, x1, mod3, mod_row, final_norm_w, n_p, n_s, tm, last))

    return (x[0].reshape(bp, lp, d), x[1].reshape(bs, ls, d), states)
```

```python
import functools
import math

import jax
import jax.numpy as jnp
import numpy as np
from jax import lax
from jax.experimental import pallas as pl
from jax.experimental.pallas import tpu as pltpu

F32 = jnp.float32
BF16 = jnp.bfloat16

GRID_W = 64
HY_W = 512
HY_ORDER = 2
HY_EMB = 33
HY_FAST_PCT = 0.3
HY_SLOW_PCT = 1.5
HY_TARGET = 1e-2
N_RET_HEADS = 4
RET_DK = 128
RET_DV = 256
RET_CHUNK = 256
ROPE_BASE = 10000.0
N_EXPERTS = 32
TOP_K = 4
SWIGLU_LIMIT = 7.0
SWIGLU_ALPHA = 1.702
EPS = 1e-6

LANES = 128
VMEM_LIMIT = 56 * 1024 * 1024

COL_V, COL_G, COL_GA, COL_GB = 0, 1024, 2048, 3072
COL_HV, COL_HX1, COL_HX2, COL_Q, COL_K = 4096, 4608, 5120, 5632, 6144
IN_W = 6656
IN_COL_BLOCKS = (5, 6, 7, 8, 9, 10, 11, 12, 0, 1, 2, 3, 4)


def _dot(a, b):
    return jnp.dot(a, b, preferred_element_type=F32)


def _split_bf16(a):
    hi = a.astype(BF16)
    lo = (a - hi.astype(F32)).astype(BF16)
    return hi, lo


def _dot3(a, b):
    ah, al = _split_bf16(a)
    bh, bl = _split_bf16(b)
    return _dot(ah, bh) + (_dot(ah, bl) + _dot(al, bh))


def _sigmoid(x):
    return 0.5 * jnp.tanh(0.5 * x) + 0.5


def _silu(x):
    return x * _sigmoid(x)


def _params(sem, vmem=VMEM_LIMIT):
    return pltpu.CompilerParams(dimension_semantics=sem, vmem_limit_bytes=vmem)


def _resident(shape, index_map):
    return pl.BlockSpec(shape, index_map, pipeline_mode=pl.Buffered(1))


def _mod_kernel(c_ref, w_ref, b_ref, o_ref):
    o_ref[0] = _dot3(_silu(c_ref[...]), w_ref[0]) + b_ref[0]


def _modulation(cvec, ada_w, ada_b):
    depth, d, six_d = ada_w.shape
    tn = six_d // 4
    return pl.pallas_call(
        _mod_kernel,
        grid=(depth, six_d // tn),
        in_specs=[
            pl.BlockSpec((8, d), lambda l, j: (0, 0)),
            pl.BlockSpec((1, d, tn), lambda l, j: (l, 0, j)),
            pl.BlockSpec((1, 1, tn), lambda l, j: (l, 0, j)),
        ],
        out_specs=pl.BlockSpec((1, 8, tn), lambda l, j: (l, 0, j)),
        out_shape=jax.ShapeDtypeStruct((depth, 8, six_d), F32),
        compiler_params=_params(("parallel", "parallel")),
        name="adaln_mod",
    )(cvec, ada_w, ada_b.reshape(depth, 1, six_d))


def _in_kernel(x_ref, nw_ref, mod_ref, w_ref, o_ref, *, d, cw):
    x = x_ref[...]
    y = x * lax.rsqrt(jnp.mean(x * x, axis=-1, keepdims=True) + EPS) * nw_ref[...]
    shift = mod_ref[0, :, 0:d]
    scale = mod_ref[0, :, d:2 * d]
    u = (y * (1.0 + scale) + shift).astype(BF16)
    for j, src in enumerate(IN_COL_BLOCKS):
        o_ref[:, j * cw:(j + 1) * cw] = _dot(u, w_ref[:, src * cw:(src + 1) * cw]).astype(BF16)


def _in_proj(x, norm_w, mod3, mod_row, w_in_bf, tm):
    t, d = x.shape
    return pl.pallas_call(
        functools.partial(_in_kernel, d=d, cw=IN_W // len(IN_COL_BLOCKS)),
        grid=(t // tm,),
        in_specs=[
            pl.BlockSpec((tm, d), lambda i: (i, 0)),
            pl.BlockSpec((1, d), lambda i: (0, 0)),
            pl.BlockSpec((1, 1, 2 * d), lambda i: (mod_row(i), 0, 0)),
            _resident((d, IN_W), lambda i: (0, 0)),
        ],
        out_specs=pl.BlockSpec((tm, IN_W), lambda i: (i, 0)),
        out_shape=jax.ShapeDtypeStruct((t, IN_W), BF16),
        compiler_params=_params(("parallel",)),
        name="in_proj",
    )(x, norm_w.reshape(1, d), mod3, w_in_bf)


def _filt_kernel(z_ref, t_ref, w1, b1, w2, b2, w3, b3, fr, wo, dl_ref, h_ref, s_ref, *, tl):
    i = pl.program_id(0)
    freq = fr[...]
    h = jnp.sin(freq * (_dot3(z_ref[...], w1[...]) + b1[...]))
    h = jnp.sin(freq * (_dot3(h, w2[...]) + b2[...]))
    h = jnp.sin(freq * (_dot3(h, w3[...]) + b3[...]))
    h = _dot3(h, wo[...])
    decay = jnp.exp(-t_ref[...] * jnp.abs(dl_ref[...]))
    h = h * jnp.concatenate([decay] * (2 * HY_ORDER), axis=1)
    h_ref[...] = h
    row = i * tl + lax.broadcasted_iota(jnp.int32, (tl, 1), 0)
    col = lax.broadcasted_iota(jnp.int32, (1, 2 * HY_ORDER * HY_W), 1)
    is_bwd = ((col // HY_W) % 2) == 1
    part = jnp.sum(jnp.where((row == 0) & is_bwd, 0.0, jnp.abs(h)), axis=0, keepdims=True)

    @pl.when(i == 0)
    def _():
        s_ref[...] = jnp.zeros_like(s_ref)

    s_ref[...] += part


def _hyena_taps(l, zfeat, tcol, deltas, w1, b1, w2, b2, w3, b3, freq, wout):
    tl = min(l, 256)
    nw = 2 * HY_ORDER * HY_W
    full = lambda shape: pl.BlockSpec(shape, lambda i: (0,) * len(shape))
    return pl.pallas_call(
        functools.partial(_filt_kernel, tl=tl),
        grid=(l // tl,),
        in_specs=[
            pl.BlockSpec((tl, LANES), lambda i: (i, 0)),
            pl.BlockSpec((tl, 1), lambda i: (i, 0)),
            full((LANES, LANES)), full((1, LANES)),
            full((LANES, LANES)), full((1, LANES)),
            full((LANES, LANES)), full((1, LANES)),
            full((1, LANES)),
            full((LANES, nw)),
            full((1, HY_W)),
        ],
        out_specs=[pl.BlockSpec((tl, nw), lambda i: (i, 0)), pl.BlockSpec((1, nw), lambda i: (0, 0))],
        out_shape=[jax.ShapeDtypeStruct((l, nw), F32), jax.ShapeDtypeStruct((1, nw), F32)],
        compiler_params=_params(("arbitrary",)),
        name=f"hyena_taps_{l}",
    )(zfeat, tcol, w1, b1, w2, b2, w3, b3, freq, wout, deltas)


def _fdft_kernel(c_ref, s_ref, hf_ref, hb_ref, sf_ref, sb_ref, kc_ref, ki_ref, kn_ref, *, l):
    row = lax.broadcasted_iota(jnp.int32, (l, 1), 0)
    hf = hf_ref[...]
    hb = jnp.where(row == 0, 0.0, hb_ref[...])
    inv = 1.0 / (sf_ref[...] + sb_ref[...])
    a = hf + hb
    b = hb - hf
    wf = jnp.where(row == 0, 0.5 / l, 1.0 / l)
    kc_ref[...] = _dot(c_ref[...], a.astype(BF16)) * wf * inv
    ki_ref[...] = _dot(s_ref[...], b.astype(BF16)) * wf * inv
    sgn = jnp.where(row % 2 == 0, 1.0, -1.0)
    kn_ref[...] = jnp.sum(a * sgn, axis=0, keepdims=True) * inv * (0.5 / l)


def _filter_spectra(l, cmat, smat, taps, sums):
    wt = 256
    nb = HY_W // wt
    two_w = 2 * HY_W
    return pl.pallas_call(
        functools.partial(_fdft_kernel, l=l),
        grid=(HY_ORDER, nb),
        in_specs=[
            _resident((l, l), lambda o, j: (0, 0)),
            _resident((l, l), lambda o, j: (0, 0)),
            pl.BlockSpec((l, wt), lambda o, j: (0, o * 2 * nb + j)),
            pl.BlockSpec((l, wt), lambda o, j: (0, o * 2 * nb + nb + j)),
            pl.BlockSpec((1, wt), lambda o, j: (0, o * 2 * nb + j)),
            pl.BlockSpec((1, wt), lambda o, j: (0, o * 2 * nb + nb + j)),
        ],
        out_specs=[
            pl.BlockSpec((l, wt), lambda o, j: (0, o * nb + j)),
            pl.BlockSpec((l, wt), lambda o, j: (0, o * nb + j)),
            pl.BlockSpec((1, wt), lambda o, j: (0, o * nb + j)),
        ],
        out_shape=[
            jax.ShapeDtypeStruct((l, HY_ORDER * HY_W), F32),
            jax.ShapeDtypeStruct((l, HY_ORDER * HY_W), F32),
            jax.ShapeDtypeStruct((1, HY_ORDER * HY_W), F32),
        ],
        compiler_params=_params(("parallel", "parallel")),
        name=f"hyena_spectra_{l}",
    )(cmat, smat, taps, taps, sums, sums)


def _hy_direct_kernel(c_ref, s_ref, hv_ref, hx1_ref, hx2_ref, cw_ref, cb_ref,
                      kc0, ki0, kc1, ki1, kn_ref, sk_ref, o_ref, z_scr, zb_scr, p_scr, q_scr, *, l, rc, sb):
    row = lax.broadcasted_iota(jnp.int32, (l, 1), 0)
    first = row == 0
    last = row == l - 1
    sgn = jnp.where(row % 2 == 0, 1.0, -1.0)
    cw = cw_ref[...]
    cb = cb_ref[...]

    def short_conv(ref, rows, part):
        x = ref[rows, :].astype(F32)
        xm = jnp.where(first, 0.0, pltpu.roll(x, 1, 0))
        xp = jnp.where(last, 0.0, pltpu.roll(x, l - 1, 0))
        return (xm * cw[0, part:part + 1] + x * cw[1, part:part + 1] + xp * cw[2, part:part + 1]
                + cb[part:part + 1])

    for s in range(sb):
        rows = slice(s * l, (s + 1) * l)
        z_scr[s] = short_conv(hv_ref, rows, 0)
        for o, (gate_ref, kc, ki) in enumerate(((hx1_ref, kc0, ki0), (hx2_ref, kc1, ki1))):
            z = z_scr[s]
            zb_scr[s] = z.astype(BF16)
            nyq = jnp.sum(z * sgn, axis=0, keepdims=True) * kn_ref[o:o + 1]
            for r in range(l // rc):
                sl = slice(r * rc, (r + 1) * rc)
                zc = _dot(c_ref[sl, :], zb_scr[s])
                zs = _dot(s_ref[sl, :], zb_scr[s])
                p_scr[s, sl, :] = (zc * kc[sl, :] + zs * ki[sl, :]).astype(BF16)
                q_scr[s, sl, :] = (zs * kc[sl, :] - zc * ki[sl, :]).astype(BF16)
            gate = short_conv(gate_ref, rows, o + 1)
            for r in range(l // rc):
                sl = slice(r * rc, (r + 1) * rc)
                y = _dot(c_ref[sl, :], p_scr[s]) + _dot(s_ref[sl, :], q_scr[s])
                y = y + sgn[sl] * nyq + z_scr[s, sl, :] * sk_ref[o:o + 1]
                z_scr[s, sl, :] = gate[sl] * y
        o_ref[rows, :] = z_scr[s].astype(BF16)


def _hy_kernel(ce_ref, se_ref, co_ref, so_ref, cot_ref, sot_ref, j_ref, hv_ref, hx1_ref, hx2_ref, cw_ref, cb_ref,
               kc0, ki0, kc1, ki1, kn_ref, sk_ref, o_ref, z_scr, zp_scr, zm_scr, p_scr, q_scr, yr_scr,
               *, l, rc, sb):
    h = l // 2
    row = lax.broadcasted_iota(jnp.int32, (l, 1), 0)
    first = row == 0
    last = row == l - 1
    sgn = jnp.where(row % 2 == 0, 1.0, -1.0)
    half_row = lax.broadcasted_iota(jnp.int32, (h, 1), 0)
    sm = jnp.where(half_row % 2 == 0, 1.0, -1.0)
    cw = cw_ref[...]
    cb = cb_ref[...]

    def short_conv(ref, rows, part):
        x = ref[rows, :].astype(F32)
        xm = jnp.where(first, 0.0, pltpu.roll(x, 1, 0))
        xp = jnp.where(last, 0.0, pltpu.roll(x, l - 1, 0))
        return (xm * cw[0, part:part + 1] + x * cw[1, part:part + 1] + xp * cw[2, part:part + 1]
                + cb[part:part + 1])

    for s in range(sb):
        rows = slice(s * l, (s + 1) * l)
        z_scr[s] = short_conv(hv_ref, rows, 0)
        for o, (gate_ref, kc, ki) in enumerate(((hx1_ref, kc0, ki0), (hx2_ref, kc1, ki1))):
            z = z_scr[s]
            nyq = jnp.sum(z * sgn, axis=0, keepdims=True) * kn_ref[o:o + 1]
            zmid = z[h:h + 1]
            zrev = _dot(j_ref[...], z[h:].astype(BF16))
            zp_scr[s] = (z[:h] + zrev).astype(BF16)
            zm_scr[s] = (z[:h] - zrev).astype(BF16)
            ymid = jnp.zeros_like(zmid)
            for r in range(h // rc):
                sl = slice(r * rc, (r + 1) * rc)
                so_ = slice(h + r * rc, h + (r + 1) * rc)
                zc_e = _dot(ce_ref[sl, :], zp_scr[s]) + sm[sl] * zmid
                zs_e = _dot(se_ref[sl, :], zm_scr[s])
                zc_o = _dot(co_ref[sl, :], zm_scr[s])
                zs_o = _dot(so_ref[sl, :], zp_scr[s]) + sm[sl] * zmid
                pe = zc_e * kc[sl, :] + zs_e * ki[sl, :]
                qe = zs_e * kc[sl, :] - zc_e * ki[sl, :]
                po = zc_o * kc[so_, :] + zs_o * ki[so_, :]
                qo = zs_o * kc[so_, :] - zc_o * ki[so_, :]
                ymid = ymid + jnp.sum(sm[sl] * (pe + qo), axis=0, keepdims=True)
                p_scr[s, sl, :] = pe.astype(BF16)
                q_scr[s, sl, :] = qe.astype(BF16)
                p_scr[s, so_, :] = po.astype(BF16)
                q_scr[s, so_, :] = qo.astype(BF16)
            gate = short_conv(gate_ref, rows, o + 1)
            skip = sk_ref[o:o + 1]
            for r in range(h // rc):
                sl = slice(r * rc, (r + 1) * rc)
                a1 = _dot(ce_ref[sl, :], p_scr[s, :h, :])
                a2 = _dot(se_ref[sl, :], q_scr[s, :h, :])
                a3 = _dot(cot_ref[sl, :], p_scr[s, h:, :])
                a4 = _dot(sot_ref[sl, :], q_scr[s, h:, :])
                yr_scr[s, sl, :] = ((a1 + a4) - (a2 + a3)).astype(BF16)
                y = (a1 + a4) + (a2 + a3) + sgn[sl] * nyq + z[sl] * skip
                z_scr[s, sl, :] = gate[sl] * y
            y_hi = jnp.where(half_row == 0, ymid, _dot(j_ref[...], yr_scr[s]))
            y_hi = y_hi + sgn[h:] * nyq + z[h:] * skip
            z_scr[s, h:, :] = gate[h:] * y_hi
        o_ref[rows, :] = z_scr[s].astype(BF16)


def _fold_dft(l):
    return l >= 1024


def _hyena_branch(proj, row_blk0, nb, l, mats, conv_w, conv_b, kc, ki, kn, skip):
    wt = 256
    nw = HY_W // wt
    h = l // 2
    folded = _fold_dft(l)
    rc = min(h if folded else l, 512)
    sb = math.gcd(nb, max(1, 2048 // l))
    assert row_blk0 % sb == 0
    hv0, hx10, hx20 = COL_HV // wt, COL_HX1 // wt, COL_HX2 // wt
    seq = lambda c0: pl.BlockSpec((sb * l, wt), lambda j, b: (row_blk0 // sb + b, c0 + j))
    tab = lambda o: pl.BlockSpec((l, wt), lambda j, b: (0, o * nw + j))
    if folded:
        cmat, smat, jmat = mats
        quarter = lambda blk: _resident((h, h), lambda j, b: (blk, 0))
        mat_specs = [quarter(0), quarter(0), quarter(1), quarter(1),
                     quarter(0), quarter(0), quarter(0)]
        mat_args = [cmat, smat, cmat, smat, cmat[h:, :h].T, smat[h:, :h].T, jmat]
        body = _hy_kernel
        scratch = [pltpu.VMEM((sb, l, wt), F32),
                   pltpu.VMEM((sb, h, wt), BF16), pltpu.VMEM((sb, h, wt), BF16),
                   pltpu.VMEM((sb, l, wt), BF16), pltpu.VMEM((sb, l, wt), BF16),
                   pltpu.VMEM((sb, h, wt), BF16)]
    else:
        mat_specs = [_resident((l, l), lambda j, b: (0, 0))] * 2
        mat_args = list(mats)
        body = _hy_direct_kernel
        scratch = [pltpu.VMEM((sb, l, wt), F32), pltpu.VMEM((sb, l, wt), BF16),
                   pltpu.VMEM((sb, l, wt), BF16), pltpu.VMEM((sb, l, wt), BF16)]
    return pl.pallas_call(
        functools.partial(body, l=l, rc=rc, sb=sb),
        grid=(nw, nb // sb),
        in_specs=[
            *mat_specs,
            seq(hv0), seq(hx10), seq(hx20),
            pl.BlockSpec((3, 3, wt), lambda j, b: (0, 0, j)),
            pl.BlockSpec((3, wt), lambda j, b: (0, j)),
            tab(0), tab(0), tab(1), tab(1),
            pl.BlockSpec((HY_ORDER, wt), lambda j, b: (0, j)),
            pl.BlockSpec((HY_ORDER, wt), lambda j, b: (0, j)),
        ],
        out_specs=pl.BlockSpec((sb * l, wt), lambda j, b: (b, j)),
        out_shape=jax.ShapeDtypeStruct((nb * l, HY_W), BF16),
        scratch_shapes=scratch,
        compiler_params=_params(("parallel", "parallel")),
        name=f"hyena_{l}",
    )(*mat_args, proj, proj, proj, conv_w.reshape(3, 3, HY_W),
      conv_b.reshape(3, HY_W), kc, ki, kc, ki, kn.reshape(HY_ORDER, HY_W), skip)


def _log_sigmoid(x):
    return jnp.minimum(x, 0.0) - jnp.log1p(jnp.exp(-jnp.abs(x)))


def _ret_kernel(*refs, l, hb, use_rope, has_s0, has_acc, st_layers, st_own):
    it = iter(refs)
    dl_ref, q_ref, k_ref, v_ref, g_ref = (next(it) for _ in range(5))
    cos_ref, sin_ref = (next(it), next(it)) if use_rope else (None, None)
    s0_ref = next(it) if has_s0 else None
    if has_acc:
        next(it)
    o_ref, st_ref, kr_scr, kv_scr = (next(it) for _ in range(4))
    for other in range(st_layers):
        if other != st_own:
            st_ref[0, other] = jnp.zeros(st_ref.shape[2:], F32)

    c, dk, dv = RET_CHUNK, RET_DK, RET_DV
    nc = l // c
    unroll = min(nc, 4)
    pos = lax.broadcasted_iota(jnp.int32, (c, 1), 0).astype(F32)
    diff = pos - lax.broadcasted_iota(jnp.int32, (1, c), 1).astype(F32)

    def rope(x, r0):
        if not use_rope:
            return x
        return x * cos_ref[pl.ds(r0, c), :] + pltpu.roll(x, dk // 2, 1) * sin_ref[pl.ds(r0, c), :]

    for h in range(hb):
        qs = slice(h * dk, (h + 1) * dk)
        vs = slice(h * dv, (h + 1) * dv)
        lgf = _log_sigmoid(dl_ref[0, h])
        lgb = _log_sigmoid(dl_ref[1, h])
        dm = jnp.where(diff >= 0, jnp.exp(lgf * diff), jnp.exp(-lgb * diff))
        zeta_f = jnp.exp(lgf * (c - 1.0 - pos))
        zeta_b = jnp.exp(lgb * pos)
        xi_f = jnp.exp(lgf * (pos + 1.0))
        xi_b = jnp.exp(lgb * (c - pos))
        gc_f = jnp.exp(lgf * c)
        gc_b = jnp.exp(lgb * c)

        def kv_step(ci, carry):
            r0 = pl.multiple_of(ci * c, c)
            k = rope(k_ref[pl.ds(r0, c), qs].astype(F32), r0) * (dk ** -0.5)
            kr_scr[pl.ds(r0, c), qs] = k.astype(BF16)
            kk = jnp.concatenate([k * zeta_f, k * zeta_b], axis=1).astype(BF16)
            kv_scr[h, ci] = lax.dot_general(kk, v_ref[pl.ds(r0, c), vs], (((0,), (0,)), ((), ())),
                                            preferred_element_type=F32)
            return carry

        lax.fori_loop(0, nc, kv_step, 0, unroll=unroll)

        def fwd_scan(ci, s):
            upd = kv_scr[h, ci, 0:dk, :]
            kv_scr[h, ci, 0:dk, :] = s
            return gc_f * s + upd

        def bwd_scan(i, r):
            ci = nc - 1 - i
            upd = kv_scr[h, ci, dk:2 * dk, :]
            kv_scr[h, ci, dk:2 * dk, :] = r
            return gc_b * r + upd

        zero = jnp.zeros((dk, dv), F32)
        st_ref[0, st_own, 0, h] = lax.fori_loop(0, nc, fwd_scan, s0_ref[0, 0, 0, h] if has_s0 else zero)
        st_ref[0, st_own, 1, h] = lax.fori_loop(0, nc, bwd_scan, s0_ref[0, 0, 1, h] if has_s0 else zero)

        def out_step(ci, carry):
            r0 = pl.multiple_of(ci * c, c)
            q = rope(q_ref[pl.ds(r0, c), qs].astype(F32), r0)
            sc = lax.dot_general(q.astype(BF16), kr_scr[pl.ds(r0, c), qs], (((1,), (1,)), ((), ())),
                                 preferred_element_type=F32) * dm
            o = _dot(sc.astype(BF16), v_ref[pl.ds(r0, c), vs])
            if has_s0 or nc > 1:
                qq = jnp.concatenate([q * xi_f, q * xi_b], axis=1).astype(BF16)
                o = o + _dot(qq, kv_scr[h, ci].astype(BF16))
            o = o * lax.rsqrt(jnp.mean(o * o, axis=-1, keepdims=True) + EPS)
            o_ref[pl.ds(r0, c), vs] = (_silu(g_ref[pl.ds(r0, c), vs].astype(F32)) * o).astype(BF16)
            return carry

        lax.fori_loop(0, nc, out_step, 0, unroll=unroll)


def _retention_branch(proj, row_blk0, nb, l, decay_logit, rope_tabs, state, layer, n_state_layers=1,
                      states_acc=None):
    nh, dk, dv = N_RET_HEADS, RET_DK, RET_DV
    use_rope = rope_tabs is not None
    has_s0 = state is not None
    has_acc = states_acc is not None
    own_layer = layer if n_state_layers > 1 else 0
    st_layers = 1 if has_acc else n_state_layers
    hb = nh if l <= 512 else 1
    ng = nh // hb
    seq = lambda col, w: pl.BlockSpec((l, hb * w), lambda b, g: (row_blk0 + b, col // (hb * w) + g))
    in_specs = [
        pl.BlockSpec((2, hb, 1, 1), lambda b, g: (0, g, 0, 0)),
        seq(COL_Q, dk), seq(COL_K, dk), seq(COL_V, dv), seq(COL_G, dv),
    ]
    args = [decay_logit.reshape(2, nh, 1, 1), proj, proj, proj, proj]
    if use_rope:
        in_specs += [pl.BlockSpec((l, dk), lambda b, g: (0, 0))] * 2
        args += list(rope_tabs)
    if has_s0:
        in_specs.append(pl.BlockSpec((1, 1, 2, hb, dk, dv), lambda b, g: (b, layer, 0, g, 0, 0)))
        args.append(state)
    if has_acc:
        in_specs.append(pl.BlockSpec(memory_space=pl.ANY))
        args.append(states_acc)
    return pl.pallas_call(
        functools.partial(_ret_kernel, l=l, hb=hb, use_rope=use_rope, has_s0=has_s0, has_acc=has_acc,
                          st_layers=st_layers, st_own=0 if has_acc else own_layer),
        grid=(nb, ng),
        in_specs=in_specs,
        out_specs=[
            pl.BlockSpec((l, hb * dv), lambda b, g: (b, g)),
            pl.BlockSpec((1, st_layers, 2, hb, dk, dv),
                         lambda b, g: (b, own_layer if has_acc else 0, 0, g, 0, 0)),
        ],
        out_shape=[
            jax.ShapeDtypeStruct((nb * l, nh * dv), BF16),
            jax.ShapeDtypeStruct((nb, n_state_layers, 2, nh, dk, dv), F32),
        ],
        input_output_aliases={len(args) - 1: 1} if has_acc else {},
        scratch_shapes=[
            pltpu.VMEM((l, hb * dk), BF16),
            pltpu.VMEM((hb, l // RET_CHUNK, 2 * dk, dv), F32),
        ],
        compiler_params=_params(("parallel", "parallel")),
        name=f"retention_{l}",
    )(*args)


ROW_ALIGN = 16


def _sorted_rows(tm):
    worst = TOP_K * tm + N_EXPERTS * (ROW_ALIGN - 1) + ROW_ALIGN
    return -(-worst // LANES) * LANES


def _mix_kernel(zap_ref, zas_ref, zbp_ref, zbs_ref, gap_ref, gas_ref, gbp_ref, gbs_ref, xp_ref, xs_ref,
                mod_ref, wa_ref, wb_ref, wo_ref, nw_ref, rw_ref, rb_ref, x1_ref, xl_ref, route_ref, seg_ref,
                h_scr, pr_scr, sg_scr, *, d, tm, rmax, n_tiles, n_prompt_tiles):
    i = pl.program_id(0)

    @pl.when(i == 0)
    def _():
        h_scr[...] = jnp.zeros_like(h_scr)
        pr_scr[...] = jnp.zeros_like(pr_scr)
        sg_scr[...] = jnp.zeros_like(sg_scr)

    @pl.when(i > n_tiles)
    def _():
        xl_ref[...] = jnp.zeros_like(xl_ref)

    @pl.when(i <= n_tiles)
    def _():
        _mix_step(zap_ref, zas_ref, zbp_ref, zbs_ref, gap_ref, gas_ref, gbp_ref, gbs_ref, xp_ref, xs_ref,
                  mod_ref, wa_ref, wb_ref, wo_ref, nw_ref, rw_ref, rb_ref, x1_ref, xl_ref, route_ref, seg_ref,
                  h_scr, pr_scr, sg_scr, d=d, tm=tm, rmax=rmax, n_tiles=n_tiles, n_prompt_tiles=n_prompt_tiles)


def _mix_step(zap_ref, zas_ref, zbp_ref, zbs_ref, gap_ref, gas_ref, gbp_ref, gbs_ref, xp_ref, xs_ref,
              mod_ref, wa_ref, wb_ref, wo_ref, nw_ref, rw_ref, rb_ref, x1_ref, xl_ref, route_ref, seg_ref,
              h_scr, pr_scr, sg_scr, *, d, tm, rmax, n_tiles, n_prompt_tiles):
    i = pl.program_id(0)
    r_col = lax.broadcasted_iota(jnp.int32, (rmax, 1), 0).astype(F32)
    in_seg = jnp.logical_and(r_col >= sg_scr[0:1], r_col < sg_scr[0:1] + sg_scr[1:2])
    seg_hot = jnp.where(in_seg, 1.0, 0.0).astype(BF16)
    row_of = _dot(jnp.concatenate([seg_hot, seg_hot], axis=1), pr_scr[...])
    sel = jnp.where(row_of == r_col + 1.0, 1.0, 0.0).astype(BF16)
    xl_ref[...] = _dot(sel, h_scr[...]).astype(BF16)

    is_prompt = jnp.minimum(i, n_tiles - 1) < n_prompt_tiles
    pick = lambda p_ref, s_ref: jnp.where(is_prompt, p_ref[...], s_ref[...])
    za, zb = pick(zap_ref, zas_ref), pick(zbp_ref, zbs_ref)
    ga, gb = pick(gap_ref, gas_ref), pick(gbp_ref, gbs_ref)
    merged = (_sigmoid(ga.astype(F32)) * _dot(za, wa_ref[...])
              + _sigmoid(gb.astype(F32)) * _dot(zb, wb_ref[...]))
    gate1 = mod_ref[0, :, 2 * d:3 * d]
    shift2 = mod_ref[0, :, 3 * d:4 * d]
    scale2 = mod_ref[0, :, 4 * d:5 * d]
    x1 = pick(xp_ref, xs_ref) + gate1 * _dot(merged.astype(BF16), wo_ref[...])
    x1_ref[...] = x1
    hn = x1 * lax.rsqrt(jnp.mean(x1 * x1, axis=-1, keepdims=True) + EPS) * nw_ref[...]
    hmo = hn * (1.0 + scale2) + shift2

    logits = _dot3(hmo, rw_ref[...]) + rb_ref[...]
    lane = lax.broadcasted_iota(jnp.int32, (1, LANES), 1)
    idxs, vals = [], []
    for _ in range(TOP_K):
        m = jnp.max(logits, axis=1, keepdims=True)
        ik = jnp.min(jnp.where(logits == m, lane, LANES), axis=1, keepdims=True)
        idxs.append(ik)
        vals.append(m)
        logits = jnp.where(lane == ik, -jnp.inf, logits)
    exps = [jnp.exp(v - vals[0]) for v in vals]
    denom = exps[0] + exps[1] + exps[2] + exps[3]

    hot = jnp.zeros((tm, LANES), F32)
    for ik in idxs:
        hot = hot + jnp.where(lane == ik, 1.0, 0.0)
    r_i = lax.broadcasted_iota(jnp.int32, (tm, tm), 0)
    c_i = lax.broadcasted_iota(jnp.int32, (tm, tm), 1)
    below = jnp.where(c_i < r_i, 1.0, 0.0).astype(BF16)
    before = _dot(below, hot.astype(BF16))
    seg = jnp.ceil(jnp.sum(hot, axis=0, keepdims=True) * (1.0 / ROW_ALIGN)) * ROW_ALIGN
    seg_ref[0] = seg
    e_r = lax.broadcasted_iota(jnp.int32, (LANES, LANES), 0)
    e_c = lax.broadcasted_iota(jnp.int32, (LANES, LANES), 1)
    seg_start = _dot3(jnp.broadcast_to(seg, (8, LANES)), jnp.where(e_r < e_c, 1.0, 0.0))[0:1]
    base = seg_start + before

    route = jnp.zeros((tm, LANES), F32)
    for k in range(TOP_K):
        srow = jnp.sum(jnp.where(lane == idxs[k], base, 0.0), axis=1, keepdims=True)
        route = route + jnp.where(lane == k, srow, 0.0)
        route = route + jnp.where(lane == TOP_K + k, idxs[k].astype(F32), 0.0)
        route = route + jnp.where(lane == 2 * TOP_K + k, exps[k] / denom, 0.0)
    route_ref[...] = route

    h_scr[...] = hmo.astype(BF16)
    pair_row = jnp.where(hot > 0.0, base + 1.0, 0.0).T
    pr_scr[...] = jnp.concatenate(_split_bf16(pair_row), axis=0)
    sg_scr[0:1] = seg_start
    sg_scr[1:2] = seg


def _mix(za, zb, proj, x, mod3, mod_row, wa, wb, wo, norm_w, rw, rb, tm, spare_rows):
    d = x[0].shape[1]
    n_p = x[0].shape[0] // tm
    n_s = x[1].shape[0] // tm
    nt = n_p + n_s
    t = nt * tm
    rmax = _sorted_rows(tm)
    spare_blocks = -(-spare_rows // rmax)
    vw = N_RET_HEADS * RET_DV
    cur = lambda i: jnp.minimum(i, nt - 1)
    prev = lambda i: jnp.maximum(i - 1, 0)
    pair = lambda w, col: [pl.BlockSpec((tm, w), lambda i: (jnp.minimum(cur(i), n_p - 1), col)),
                           pl.BlockSpec((tm, w), lambda i: (jnp.maximum(cur(i) - n_p, 0), col))]
    const = lambda i: (0, 0)
    return pl.pallas_call(
        functools.partial(_mix_kernel, d=d, tm=tm, rmax=rmax, n_tiles=nt, n_prompt_tiles=n_p),
        grid=(nt + 1 + spare_blocks,),
        in_specs=[
            *pair(HY_W, 0), *pair(vw, 0), *pair(d, COL_GA // d), *pair(d, COL_GB // d), *pair(d, 0),
            pl.BlockSpec((1, 1, 6 * d), lambda i: (mod_row(cur(i)), 0, 0)),
            _resident((HY_W, d), const), _resident((vw, d), const), _resident((d, d), const),
            pl.BlockSpec((1, d), const),
            _resident((d, LANES), const), pl.BlockSpec((1, LANES), const),
        ],
        out_specs=[
            pl.BlockSpec((tm, d), lambda i: (cur(i), 0)),
            pl.BlockSpec((rmax, d), lambda i: (prev(i), 0)),
            pl.BlockSpec((tm, LANES), lambda i: (cur(i), 0)),
            pl.BlockSpec((1, 1, LANES), lambda i: (cur(i), 0, 0)),
        ],
        out_shape=[
            jax.ShapeDtypeStruct((t, d), F32),
            jax.ShapeDtypeStruct(((nt + spare_blocks) * rmax, d), BF16),
            jax.ShapeDtypeStruct((t, LANES), F32),
            jax.ShapeDtypeStruct((nt, 1, LANES), F32),
        ],
        scratch_shapes=[pltpu.VMEM((tm, d), BF16), pltpu.VMEM((2 * LANES, tm), BF16),
                        pltpu.VMEM((2, LANES), F32)],
        compiler_params=_params(("arbitrary",)),
        name="merge_router",
    )(*za, *zb, *proj, *proj, *x, mod3, wa, wb, wo, norm_w.reshape(1, d), rw, rb)


def _ffn_kernel(te_ref, nu_ref, src_ref, dst_ref, xl_ref, wgu_ref, bgu_ref, wd_ref, bd_ref, yl_ref,
                xbuf, ybuf, gsem, ssem, wgu_scr, wd_scr, *, dff, te):
    j = pl.program_id(0)
    nu = nu_ref[0]
    cpt = te // ROW_ALIGN
    slot = j % 2

    def hbm_chunk(ci):
        return pl.ds(pl.multiple_of(ci * ROW_ALIGN, ROW_ALIGN), ROW_ALIGN)

    def start_gather(tile, buf):
        for ci in range(cpt):
            pltpu.make_async_copy(xl_ref.at[hbm_chunk(src_ref[tile * cpt + ci])],
                                  xbuf.at[buf, pl.ds(ci * ROW_ALIGN, ROW_ALIGN)], gsem.at[buf]).start()

    def wait_gather(buf):
        pltpu.make_async_copy(xl_ref.at[pl.ds(0, te)], xbuf.at[buf], gsem.at[buf]).wait()

    def start_scatter(tile, buf):
        for ci in range(cpt):
            pltpu.make_async_copy(ybuf.at[buf, pl.ds(ci * ROW_ALIGN, ROW_ALIGN)],
                                  yl_ref.at[hbm_chunk(dst_ref[tile * cpt + ci])], ssem.at[buf]).start()

    def wait_scatter(buf):
        pltpu.make_async_copy(ybuf.at[buf], yl_ref.at[pl.ds(0, te)], ssem.at[buf]).wait()

    @pl.when(j < nu)
    def _():
        @pl.when(j == 0)
        def _():
            start_gather(0, 0)

        start_gather(j + 1, 1 - slot)
        wait_gather(slot)

        @pl.when(j >= 2)
        def _():
            wait_scatter(slot)

        @pl.when(jnp.logical_or(j == 0, te_ref[j] != te_ref[jnp.maximum(j - 1, 0)]))
        def _():
            wgu_scr[...] = wgu_ref[0].astype(BF16)
            wd_scr[...] = wd_ref[0].astype(BF16)

        hgu = _dot(xbuf[slot], wgu_scr[...]) + bgu_ref[0]
        gate = jnp.minimum(hgu[:, :dff], SWIGLU_LIMIT)
        up = jnp.clip(hgu[:, dff:], -SWIGLU_LIMIT, SWIGLU_LIMIT)
        act = gate * _sigmoid(SWIGLU_ALPHA * gate) * (up + 1.0)
        ybuf[slot] = (_dot(act.astype(BF16), wd_scr[...]) + bd_ref[0]).astype(BF16)
        start_scatter(j, slot)

        @pl.when(j == nu - 1)
        def _():
            wait_gather(1 - slot)

            @pl.when(j >= 1)
            def _():
                wait_scatter(1 - slot)
            wait_scatter(slot)


def _expert_ffn(xl, tile_expert, n_used, src_chunk, dst_chunk, w_gate_up, b_gate_up, w_down, b_down, layer, te):
    _, d = xl.shape
    depth, n_exp, _, two_ff = w_gate_up.shape
    ne = depth * n_exp
    dff = two_ff // 2
    wsel = lambda j, tex, nu, src, dst: (layer * n_exp + tex[j], 0, 0)
    return pl.pallas_call(
        functools.partial(_ffn_kernel, dff=dff, te=te),
        grid_spec=pltpu.PrefetchScalarGridSpec(
            num_scalar_prefetch=4,
            grid=(tile_expert.shape[0],),
            in_specs=[
                pl.BlockSpec(memory_space=pl.ANY),
                pl.BlockSpec((1, d, two_ff), wsel),
                pl.BlockSpec((1, 1, two_ff), wsel),
                pl.BlockSpec((1, dff, d), wsel),
                pl.BlockSpec((1, 1, d), wsel),
            ],
            out_specs=pl.BlockSpec(memory_space=pl.ANY),
            scratch_shapes=[
                pltpu.VMEM((2, te, d), BF16), pltpu.VMEM((2, te, d), BF16),
                pltpu.SemaphoreType.DMA((2,)), pltpu.SemaphoreType.DMA((2,)),
                pltpu.VMEM((d, two_ff), BF16), pltpu.VMEM((dff, d), BF16),
            ],
        ),
        out_shape=jax.ShapeDtypeStruct(xl.shape, BF16),
        input_output_aliases={4: 0},
        compiler_params=_params(("arbitrary",)),
        name="moe_ffn",
    )(tile_expert, n_used, src_chunk, dst_chunk, xl, w_gate_up.reshape(ne, d, two_ff),
      b_gate_up.reshape(ne, 1, two_ff), w_down.reshape(ne, dff, d), b_down.reshape(ne, 1, d))


def _comb_kernel(yl_ref, route_ref, x1_ref, mod_ref, fw_ref, o_ref, *, d, tm, rmax, final_norm):
    route = route_ref[...]
    r_iota = lax.broadcasted_iota(jnp.int32, (1, rmax), 1).astype(F32)
    wm = jnp.zeros((tm, rmax), F32)
    for k in range(TOP_K):
        wm = wm + jnp.where(r_iota == route[:, k:k + 1], route[:, 2 * TOP_K + k:2 * TOP_K + k + 1], 0.0)
    out = _dot(wm.astype(BF16), yl_ref[...])
    x2 = x1_ref[...] + mod_ref[0, :, 5 * d:6 * d] * out
    if final_norm:
        x2 = x2 * lax.rsqrt(jnp.mean(x2 * x2, axis=-1, keepdims=True) + EPS) * fw_ref[...]
    o_ref[...] = x2


def _combine(yl, route, x1, mod3, mod_row, final_w, tile0, n_tiles, tm, final_norm):
    d = x1.shape[1]
    rmax = _sorted_rows(tm)
    return pl.pallas_call(
        functools.partial(_comb_kernel, d=d, tm=tm, rmax=rmax, final_norm=final_norm),
        grid=(n_tiles,),
        in_specs=[
            pl.BlockSpec((rmax, d), lambda i: (tile0 + i, 0)),
            pl.BlockSpec((tm, LANES), lambda i: (tile0 + i, 0)),
            pl.BlockSpec((tm, d), lambda i: (tile0 + i, 0)),
            pl.BlockSpec((1, 1, 6 * d), lambda i: (mod_row(tile0 + i), 0, 0)),
            pl.BlockSpec((1, d), lambda i: (0, 0)),
        ],
        out_specs=pl.BlockSpec((tm, d), lambda i: (i, 0)),
        out_shape=jax.ShapeDtypeStruct((n_tiles * tm, d), F32),
        compiler_params=_params(("parallel",)),
        name="moe_combine",
    )(yl, route, x1, mod3, final_w.reshape(1, d))


@functools.lru_cache(maxsize=None)
def _dft_mats(l):
    s = np.arange(l, dtype=np.int64)
    f = np.concatenate([np.arange(0, l, 2), np.arange(1, l, 2)]).astype(np.int64) if _fold_dft(l) else s
    ang = ((f[:, None] * s[None, :]) % (2 * l)).astype(np.float64) * (math.pi / l)
    mats = (np.cos(ang).astype(np.float32), np.sin(ang).astype(np.float32))
    if _fold_dft(l):
        a = np.arange(l // 2)
        mats += (((a[:, None] + a[None, :] == l // 2) & (a[:, None] >= 1)).astype(np.float32),)
    return mats


@functools.lru_cache(maxsize=None)
def _filter_features(l):
    f32 = np.float32
    t = np.linspace(0.0, 1.0, l, dtype=f32)[:, None]
    bands = (HY_EMB - 1) // 2
    w = f32(2.0 * math.pi) * np.arange(l, dtype=f32)[:, None] / f32(l)
    f = np.linspace(1e-4, bands - 1, bands, dtype=f32)[None, :]
    z = np.concatenate([t, np.cos(f * w), -np.sin(f * w)], axis=-1).astype(f32)
    return np.pad(z, ((0, 0), (0, LANES - HY_EMB))), t


@functools.lru_cache(maxsize=None)
def _rope_tables(rows):
    f32 = np.float32
    row = np.repeat(np.arange(rows, dtype=f32), GRID_W)
    col = np.tile(np.arange(GRID_W, dtype=f32), rows)
    half = RET_DK // 4
    inv = (f32(ROPE_BASE) ** (-np.arange(half, dtype=f32) / f32(half))).astype(f32)
    ang = np.concatenate([row[:, None] * inv, col[:, None] * inv], axis=-1)
    ang = np.concatenate([ang, ang], axis=-1).astype(f32)
    sign = np.where(np.arange(RET_DK) < RET_DK // 2, -1.0, 1.0).astype(f32)
    return np.cos(ang), np.sin(ang) * sign


@functools.lru_cache(maxsize=None)
def _decay_rates():
    return np.linspace(math.log(HY_TARGET) / HY_SLOW_PCT, math.log(HY_TARGET) / HY_FAST_PCT,
                       HY_W, dtype=np.float32)[None, :]


def _pad2(a, rows, cols):
    return jnp.pad(a, ((0, rows - a.shape[0]), (0, cols - a.shape[1])))


def _ffn_plan(seg, rmax, te, n_ffn_tiles):
    nt, ne = seg.shape
    cum_t = jnp.cumsum(seg, axis=0)
    tot = cum_t[-1]
    padded = ((tot + te - 1) // te) * te
    ends = jnp.cumsum(padded)
    offs = ends - padded
    n_used = ends[-1] // te
    expert_of = lambda rows: jnp.minimum(jnp.sum(ends[None, :] <= rows[:, None], axis=1), ne - 1)
    tiles = jnp.arange(n_ffn_tiles, dtype=jnp.int32)
    tile_expert = expert_of(jnp.minimum(tiles, n_used - 1) * te).astype(jnp.int32)

    rows = jnp.arange(n_ffn_tiles * te // ROW_ALIGN, dtype=jnp.int32) * ROW_ALIGN
    hot_e = expert_of(rows)[:, None] == jnp.arange(ne)[None, :]
    pick_e = lambda tab: jnp.sum(jnp.where(hot_e[:, None, :], tab[None], 0), axis=2)
    r_s = rows - pick_e(offs[None, :])[:, 0]
    cum_sel = pick_e(cum_t)
    i_s = jnp.minimum(jnp.sum(cum_sel <= r_s[:, None], axis=1), nt - 1)
    hot_i = i_s[:, None] == jnp.arange(nt)[None, :]
    seg_base = (jnp.arange(nt)[:, None] * rmax + jnp.cumsum(seg, axis=1) - seg) - (cum_t - seg)
    src_row = r_s + jnp.sum(jnp.where(hot_i, pick_e(seg_base), 0), axis=1)
    valid = (r_s < pick_e(tot[None, :])[:, 0]) & (rows < ends[-1])
    src_chunk = jnp.where(valid, src_row // ROW_ALIGN, rmax // ROW_ALIGN - 1).astype(jnp.int32)
    pad_rank = jnp.cumsum(jnp.where(valid, 0, 1)) - 1
    spare = nt * rmax // ROW_ALIGN + jnp.minimum(pad_rank, _spare_chunks(te) - 1)
    dst_chunk = jnp.where(valid, src_row // ROW_ALIGN, spare).astype(jnp.int32)
    return tile_expert, n_used.reshape(1).astype(jnp.int32), src_chunk, dst_chunk


def _spare_chunks(te):
    return N_EXPERTS * (te // ROW_ALIGN - 1)


def kernel(x_prompt, x_sample, state_ret, c, c_ctx, ada_w, ada_b, norm_mix_w, w_in, conv_w, conv_b,
           f_w1, f_b1, f_w2, f_b2, f_w3, f_b3, f_freq, f_wout, hy_skip, ret_decay_logit,
           w_branch_a, w_branch_b, w_out, norm_ffn_w, router_w, router_b, w_gate_up, b_gate_up,
           w_down, b_down, final_norm_w):
    bp, lp, d = x_prompt.shape
    bs, ls, _ = x_sample.shape
    depth = ada_w.shape[0]
    tp, ts = bp * lp, bs * ls
    t = tp + ts
    assert ls % lp == 0 and tp % ls == 0 and lp % RET_CHUNK == 0 and ls % GRID_W == 0
    tm = 512 if (tp % 512 == 0 and ls % 512 == 0) else lp
    te = 512
    nt = t // tm
    rmax = _sorted_rows(tm)
    n_ffn_tiles = -(-(t * TOP_K + nt * N_EXPERTS * (ROW_ALIGN - 1)) // te) + N_EXPERTS

    dft = {l: tuple(jnp.asarray(a).astype(BF16) for a in _dft_mats(l)) for l in (lp, ls)}
    feats = {l: tuple(jnp.asarray(a) for a in _filter_features(l)) for l in (lp, ls)}
    rope_tabs = tuple(jnp.asarray(a) for a in _rope_tables(ls // GRID_W))
    deltas = jnp.asarray(_decay_rates())

    cvec = jnp.zeros((8, d), F32).at[:bs].set(c).at[bs].set(c_ctx)
    mod3 = _modulation(cvec, ada_w, ada_b).reshape(depth * 8, 1, 6 * d)
    n_p, n_s = tp // tm, ts // tm

    x = (x_prompt.reshape(tp, d), x_sample.reshape(ts, d))
    states = None
    for l in range(depth):
        mod_row = lambda i, l=l: l * 8 + jnp.where(i < n_p, bs, jnp.maximum(i - n_p, 0) // (ls // tm))
        w_in_bf = w_in[l].astype(BF16)
        proj = (_in_proj(x[0], norm_mix_w[l], mod3, mod_row, w_in_bf, tm),
                _in_proj(x[1], norm_mix_w[l], mod3, lambda i, f=mod_row: f(i + n_p), w_in_bf, tm))

        w1 = _pad2(f_w1[l], LANES, LANES)
        w2 = _pad2(f_w2[l], LANES, LANES)
        w3 = _pad2(f_w3[l], LANES, LANES)
        wo = _pad2(f_wout[l], LANES, f_wout.shape[2])
        b1, b2, b3, fr = (_pad2(a[l][None, :], 1, LANES) for a in (f_b1, f_b2, f_b3, f_freq))
        za, zb = [], []
        for grp, (nb, ln) in enumerate(((bp, lp), (bs, ls))):
            taps, sums = _hyena_taps(ln, feats[ln][0], feats[ln][1], deltas, w1, b1, w2, b2, w3, b3, fr, wo)
            kc, ki, kn = _filter_spectra(ln, dft[ln][0], dft[ln][1], taps, sums)
            za.append(_hyena_branch(proj[grp], 0, nb, ln, dft[ln], conv_w[l], conv_b[l], kc, ki, kn, hy_skip[l]))
            if grp == 0:
                zb_g, states = _retention_branch(proj[grp], 0, nb, ln, ret_decay_logit[l], None, None, l,
                                                 depth, states)
            else:
                zb_g, _ = _retention_branch(proj[grp], 0, nb, ln, ret_decay_logit[l], rope_tabs, state_ret, l)
            zb.append(zb_g)

        rw = _pad2(router_w[l], d, LANES)
        rb = jnp.full((1, LANES), -jnp.inf, F32).at[0, :N_EXPERTS].set(router_b[l])
        x1, xl, route, seg = _mix(za, zb, proj, x, mod3, mod_row,
                                  w_branch_a[l].astype(BF16), w_branch_b[l].astype(BF16),
                                  w_out[l].astype(BF16), norm_ffn_w[l], rw, rb, tm, _spare_chunks(te) * ROW_ALIGN)
        tile_expert, n_used, src_chunk, dst_chunk = _ffn_plan(seg[:, 0, :N_EXPERTS].astype(jnp.int32), rmax, te,
                                                              n_ffn_tiles)
        yl = _expert_ffn(xl, tile_expert, n_used, src_chunk, dst_chunk, w_gate_up, b_gate_up, w_down, b_down,
                         l, te)
        last = l == depth - 1
        x = (_combine(yl, route, x1, mod3, mod_row, final_norm_w, 0, n_p, tm, last),
             _combine(yl, route, x1, mod3, mod_row, final_norm_w, n_p, n_s, tm, last))

    return (x[0].reshape(bp, lp, d), x[1].reshape(bs, ls, d), states)
```

```python
import functools
import math

import jax
import jax.numpy as jnp
import numpy as np
from jax import lax
from jax.experimental import pallas as pl
from jax.experimental.pallas import tpu as pltpu

F32 = jnp.float32
BF16 = jnp.bfloat16

GRID_W = 64
HY_W = 512
HY_ORDER = 2
HY_EMB = 33
HY_FAST_PCT = 0.3
HY_SLOW_PCT = 1.5
HY_TARGET = 1e-2
N_RET_HEADS = 4
RET_DK = 128
RET_DV = 256
RET_CHUNK = 256
ROPE_BASE = 10000.0
N_EXPERTS = 32
TOP_K = 4
SWIGLU_LIMIT = 7.0
SWIGLU_ALPHA = 1.702
EPS = 1e-6

LANES = 128
VMEM_LIMIT = 56 * 1024 * 1024

COL_V, COL_G, COL_GA, COL_GB = 0, 1024, 2048, 3072
COL_HV, COL_HX1, COL_HX2, COL_Q, COL_K = 4096, 4608, 5120, 5632, 6144
IN_W = 6656
IN_COL_BLOCKS = (5, 6, 7, 8, 9, 10, 11, 12, 0, 1, 2, 3, 4)


def _dot(a, b):
    return jnp.dot(a, b, preferred_element_type=F32)


def _split_bf16(a):
    hi = a.astype(BF16)
    lo = (a - hi.astype(F32)).astype(BF16)
    return hi, lo


def _dot3(a, b):
    ah, al = _split_bf16(a)
    bh, bl = _split_bf16(b)
    return _dot(ah, bh) + (_dot(ah, bl) + _dot(al, bh))


def _sigmoid(x):
    return 0.5 * jnp.tanh(0.5 * x) + 0.5


def _silu(x):
    return x * _sigmoid(x)


def _params(sem, vmem=VMEM_LIMIT):
    return pltpu.CompilerParams(dimension_semantics=sem, vmem_limit_bytes=vmem)


def _resident(shape, index_map):
    return pl.BlockSpec(shape, index_map, pipeline_mode=pl.Buffered(1))


def _mod_kernel(c_ref, w_ref, b_ref, o_ref):
    o_ref[0] = _dot3(_silu(c_ref[...]), w_ref[0]) + b_ref[0]


def _modulation(cvec, ada_w, ada_b):
    depth, d, six_d = ada_w.shape
    tn = six_d // 4
    return pl.pallas_call(
        _mod_kernel,
        grid=(depth, six_d // tn),
        in_specs=[
            pl.BlockSpec((8, d), lambda l, j: (0, 0)),
            pl.BlockSpec((1, d, tn), lambda l, j: (l, 0, j)),
            pl.BlockSpec((1, 1, tn), lambda l, j: (l, 0, j)),
        ],
        out_specs=pl.BlockSpec((1, 8, tn), lambda l, j: (l, 0, j)),
        out_shape=jax.ShapeDtypeStruct((depth, 8, six_d), F32),
        compiler_params=_params(("parallel", "parallel")),
        name="adaln_mod",
    )(cvec, ada_w, ada_b.reshape(depth, 1, six_d))


def _in_kernel(x_ref, nw_ref, mod_ref, w_ref, o_ref, *, d, cw):
    x = x_ref[...]
    y = x * lax.rsqrt(jnp.mean(x * x, axis=-1, keepdims=True) + EPS) * nw_ref[...]
    shift = mod_ref[0, :, 0:d]
    scale = mod_ref[0, :, d:2 * d]
    u = (y * (1.0 + scale) + shift).astype(BF16)
    for j, src in enumerate(IN_COL_BLOCKS):
        o_ref[:, j * cw:(j + 1) * cw] = _dot(u, w_ref[:, src * cw:(src + 1) * cw]).astype(BF16)


def _in_proj(x, norm_w, mod3, mod_row, w_in_bf, tm):
    t, d = x.shape
    return pl.pallas_call(
        functools.partial(_in_kernel, d=d, cw=IN_W // len(IN_COL_BLOCKS)),
        grid=(t // tm,),
        in_specs=[
            pl.BlockSpec((tm, d), lambda i: (i, 0)),
            pl.BlockSpec((1, d), lambda i: (0, 0)),
            pl.BlockSpec((1, 1, 2 * d), lambda i: (mod_row(i), 0, 0)),
            _resident((d, IN_W), lambda i: (0, 0)),
        ],
        out_specs=pl.BlockSpec((tm, IN_W), lambda i: (i, 0)),
        out_shape=jax.ShapeDtypeStruct((t, IN_W), BF16),
        compiler_params=_params(("parallel",)),
        name="in_proj",
    )(x, norm_w.reshape(1, d), mod3, w_in_bf)


def _filt_kernel(z_ref, t_ref, w1, b1, w2, b2, w3, b3, fr, wo, dl_ref, h_ref, s_ref, *, tl):
    i = pl.program_id(0)
    freq = fr[...]
    h = jnp.sin(freq * (_dot3(z_ref[...], w1[...]) + b1[...]))
    h = jnp.sin(freq * (_dot3(h, w2[...]) + b2[...]))
    h = jnp.sin(freq * (_dot3(h, w3[...]) + b3[...]))
    h = _dot3(h, wo[...])
    decay = jnp.exp(-t_ref[...] * jnp.abs(dl_ref[...]))
    h = h * jnp.concatenate([decay] * (2 * HY_ORDER), axis=1)
    h_ref[...] = h
    row = i * tl + lax.broadcasted_iota(jnp.int32, (tl, 1), 0)
    col = lax.broadcasted_iota(jnp.int32, (1, 2 * HY_ORDER * HY_W), 1)
    is_bwd = ((col // HY_W) % 2) == 1
    part = jnp.sum(jnp.where((row == 0) & is_bwd, 0.0, jnp.abs(h)), axis=0, keepdims=True)

    @pl.when(i == 0)
    def _():
        s_ref[...] = jnp.zeros_like(s_ref)

    s_ref[...] += part


def _hyena_taps(l, zfeat, tcol, deltas, w1, b1, w2, b2, w3, b3, freq, wout):
    tl = min(l, 256)
    nw = 2 * HY_ORDER * HY_W
    full = lambda shape: pl.BlockSpec(shape, lambda i: (0,) * len(shape))
    return pl.pallas_call(
        functools.partial(_filt_kernel, tl=tl),
        grid=(l // tl,),
        in_specs=[
            pl.BlockSpec((tl, LANES), lambda i: (i, 0)),
            pl.BlockSpec((tl, 1), lambda i: (i, 0)),
            full((LANES, LANES)), full((1, LANES)),
            full((LANES, LANES)), full((1, LANES)),
            full((LANES, LANES)), full((1, LANES)),
            full((1, LANES)),
            full((LANES, nw)),
            full((1, HY_W)),
        ],
        out_specs=[pl.BlockSpec((tl, nw), lambda i: (i, 0)), pl.BlockSpec((1, nw), lambda i: (0, 0))],
        out_shape=[jax.ShapeDtypeStruct((l, nw), F32), jax.ShapeDtypeStruct((1, nw), F32)],
        compiler_params=_params(("arbitrary",)),
        name=f"hyena_taps_{l}",
    )(zfeat, tcol, w1, b1, w2, b2, w3, b3, freq, wout, deltas)


def _fdft_kernel(c_ref, s_ref, hf_ref, hb_ref, sf_ref, sb_ref, kc_ref, ki_ref, kn_ref, *, l):
    row = lax.broadcasted_iota(jnp.int32, (l, 1), 0)
    hf = hf_ref[...]
    hb = jnp.where(row == 0, 0.0, hb_ref[...])
    inv = 1.0 / (sf_ref[...] + sb_ref[...])
    a = hf + hb
    b = hb - hf
    wf = jnp.where(row == 0, 0.5 / l, 1.0 / l)
    kc_ref[...] = _dot(c_ref[...], a.astype(BF16)) * wf * inv
    ki_ref[...] = _dot(s_ref[...], b.astype(BF16)) * wf * inv
    sgn = jnp.where(row % 2 == 0, 1.0, -1.0)
    kn_ref[...] = jnp.sum(a * sgn, axis=0, keepdims=True) * inv * (0.5 / l)


def _filter_spectra(l, cmat, smat, taps, sums):
    wt = 256
    nb = HY_W // wt
    two_w = 2 * HY_W
    return pl.pallas_call(
        functools.partial(_fdft_kernel, l=l),
        grid=(HY_ORDER, nb),
        in_specs=[
            _resident((l, l), lambda o, j: (0, 0)),
            _resident((l, l), lambda o, j: (0, 0)),
            pl.BlockSpec((l, wt), lambda o, j: (0, o * 2 * nb + j)),
            pl.BlockSpec((l, wt), lambda o, j: (0, o * 2 * nb + nb + j)),
            pl.BlockSpec((1, wt), lambda o, j: (0, o * 2 * nb + j)),
            pl.BlockSpec((1, wt), lambda o, j: (0, o * 2 * nb + nb + j)),
        ],
        out_specs=[
            pl.BlockSpec((l, wt), lambda o, j: (0, o * nb + j)),
            pl.BlockSpec((l, wt), lambda o, j: (0, o * nb + j)),
            pl.BlockSpec((1, wt), lambda o, j: (0, o * nb + j)),
        ],
        out_shape=[
            jax.ShapeDtypeStruct((l, HY_ORDER * HY_W), F32),
            jax.ShapeDtypeStruct((l, HY_ORDER * HY_W), F32),
            jax.ShapeDtypeStruct((1, HY_ORDER * HY_W), F32),
        ],
        compiler_params=_params(("parallel", "parallel")),
        name=f"hyena_spectra_{l}",
    )(cmat, smat, taps, taps, sums, sums)


def _hy_direct_kernel(c_ref, s_ref, hv_ref, hx1_ref, hx2_ref, cw_ref, cb_ref,
                      kc0, ki0, kc1, ki1, kn_ref, sk_ref, o_ref, z_scr, zb_scr, p_scr, q_scr, *, l, rc, sb):
    row = lax.broadcasted_iota(jnp.int32, (l, 1), 0)
    first = row == 0
    last = row == l - 1
    sgn = jnp.where(row % 2 == 0, 1.0, -1.0)
    cw = cw_ref[...]
    cb = cb_ref[...]

    def short_conv(ref, rows, part):
        x = ref[rows, :].astype(F32)
        xm = jnp.where(first, 0.0, pltpu.roll(x, 1, 0))
        xp = jnp.where(last, 0.0, pltpu.roll(x, l - 1, 0))
        return (xm * cw[0, part:part + 1] + x * cw[1, part:part + 1] + xp * cw[2, part:part + 1]
                + cb[part:part + 1])

    for s in range(sb):
        rows = slice(s * l, (s + 1) * l)
        z_scr[s] = short_conv(hv_ref, rows, 0)
        for o, (gate_ref, kc, ki) in enumerate(((hx1_ref, kc0, ki0), (hx2_ref, kc1, ki1))):
            z = z_scr[s]
            zb_scr[s] = z.astype(BF16)
            nyq = jnp.sum(z * sgn, axis=0, keepdims=True) * kn_ref[o:o + 1]
            for r in range(l // rc):
                sl = slice(r * rc, (r + 1) * rc)
                zc = _dot(c_ref[sl, :], zb_scr[s])
                zs = _dot(s_ref[sl, :], zb_scr[s])
                p_scr[s, sl, :] = (zc * kc[sl, :] + zs * ki[sl, :]).astype(BF16)
                q_scr[s, sl, :] = (zs * kc[sl, :] - zc * ki[sl, :]).astype(BF16)
            gate = short_conv(gate_ref, rows, o + 1)
            for r in range(l // rc):
                sl = slice(r * rc, (r + 1) * rc)
                y = _dot(c_ref[sl, :], p_scr[s]) + _dot(s_ref[sl, :], q_scr[s])
                y = y + sgn[sl] * nyq + z_scr[s, sl, :] * sk_ref[o:o + 1]
                z_scr[s, sl, :] = gate[sl] * y
        o_ref[rows, :] = z_scr[s].astype(BF16)


def _hy_kernel(ce_ref, se_ref, co_ref, so_ref, cot_ref, sot_ref, j_ref, hv_ref, hx1_ref, hx2_ref, cw_ref, cb_ref,
               kc0, ki0, kc1, ki1, kn_ref, sk_ref, o_ref, z_scr, zp_scr, zm_scr, p_scr, q_scr, yr_scr,
               *, l, rc, sb):
    h = l // 2
    row = lax.broadcasted_iota(jnp.int32, (l, 1), 0)
    first = row == 0
    last = row == l - 1
    sgn = jnp.where(row % 2 == 0, 1.0, -1.0)
    half_row = lax.broadcasted_iota(jnp.int32, (h, 1), 0)
    sm = jnp.where(half_row % 2 == 0, 1.0, -1.0)
    cw = cw_ref[...]
    cb = cb_ref[...]

    def short_conv(ref, rows, part):
        x = ref[rows, :].astype(F32)
        xm = jnp.where(first, 0.0, pltpu.roll(x, 1, 0))
        xp = jnp.where(last, 0.0, pltpu.roll(x, l - 1, 0))
        return (xm * cw[0, part:part + 1] + x * cw[1, part:part + 1] + xp * cw[2, part:part + 1]
                + cb[part:part + 1])

    for s in range(sb):
        rows = slice(s * l, (s + 1) * l)
        z_scr[s] = short_conv(hv_ref, rows, 0)
        for o, (gate_ref, kc, ki) in enumerate(((hx1_ref, kc0, ki0), (hx2_ref, kc1, ki1))):
            z = z_scr[s]
            nyq = jnp.sum(z * sgn, axis=0, keepdims=True) * kn_ref[o:o + 1]
            zmid = z[h:h + 1]
            zrev = _dot(j_ref[...], z[h:].astype(BF16))
            zp_scr[s] = (z[:h] + zrev).astype(BF16)
            zm_scr[s] = (z[:h] - zrev).astype(BF16)
            ymid = jnp.zeros_like(zmid)
            for r in range(h // rc):
                sl = slice(r * rc, (r + 1) * rc)
                so_ = slice(h + r * rc, h + (r + 1) * rc)
                zc_e = _dot(ce_ref[sl, :], zp_scr[s]) + sm[sl] * zmid
                zs_e = _dot(se_ref[sl, :], zm_scr[s])
                zc_o = _dot(co_ref[sl, :], zm_scr[s])
                zs_o = _dot(so_ref[sl, :], zp_scr[s]) + sm[sl] * zmid
                pe = zc_e * kc[sl, :] + zs_e * ki[sl, :]
                qe = zs_e * kc[sl, :] - zc_e * ki[sl, :]
                po = zc_o * kc[so_, :] + zs_o * ki[so_, :]
                qo = zs_o * kc[so_, :] - zc_o * ki[so_, :]
                ymid = ymid + jnp.sum(sm[sl] * (pe + qo), axis=0, keepdims=True)
                p_scr[s, sl, :] = pe.astype(BF16)
                q_scr[s, sl, :] = qe.astype(BF16)
                p_scr[s, so_, :] = po.astype(BF16)
                q_scr[s, so_, :] = qo.astype(BF16)
            gate = short_conv(gate_ref, rows, o + 1)
            skip = sk_ref[o:o + 1]
            for r in range(h // rc):
                sl = slice(r * rc, (r + 1) * rc)
                a1 = _dot(ce_ref[sl, :], p_scr[s, :h, :])
                a2 = _dot(se_ref[sl, :], q_scr[s, :h, :])
                a3 = _dot(cot_ref[sl, :], p_scr[s, h:, :])
                a4 = _dot(sot_ref[sl, :], q_scr[s, h:, :])
                yr_scr[s, sl, :] = ((a1 + a4) - (a2 + a3)).astype(BF16)
                y = (a1 + a4) + (a2 + a3) + sgn[sl] * nyq + z[sl] * skip
                z_scr[s, sl, :] = gate[sl] * y
            y_hi = jnp.where(half_row == 0, ymid, _dot(j_ref[...], yr_scr[s]))
            y_hi = y_hi + sgn[h:] * nyq + z[h:] * skip
            z_scr[s, h:, :] = gate[h:] * y_hi
        o_ref[rows, :] = z_scr[s].astype(BF16)


def _fold_dft(l):
    return l >= 1024


def _hyena_branch(proj, row_blk0, nb, l, mats, conv_w, conv_b, kc, ki, kn, skip):
    wt = 256
    nw = HY_W // wt
    h = l // 2
    folded = _fold_dft(l)
    rc = min(h if folded else l, 512)
    sb = math.gcd(nb, max(1, 2048 // l))
    assert row_blk0 % sb == 0
    hv0, hx10, hx20 = COL_HV // wt, COL_HX1 // wt, COL_HX2 // wt
    seq = lambda c0: pl.BlockSpec((sb * l, wt), lambda j, b: (row_blk0 // sb + b, c0 + j))
    tab = lambda o: pl.BlockSpec((l, wt), lambda j, b: (0, o * nw + j))
    if folded:
        cmat, smat, jmat = mats
        quarter = lambda blk: _resident((h, h), lambda j, b: (blk, 0))
        mat_specs = [quarter(0), quarter(0), quarter(1), quarter(1),
                     quarter(0), quarter(0), quarter(0)]
        mat_args = [cmat, smat, cmat, smat, cmat[h:, :h].T, smat[h:, :h].T, jmat]
        body = _hy_kernel
        scratch = [pltpu.VMEM((sb, l, wt), F32),
                   pltpu.VMEM((sb, h, wt), BF16), pltpu.VMEM((sb, h, wt), BF16),
                   pltpu.VMEM((sb, l, wt), BF16), pltpu.VMEM((sb, l, wt), BF16),
                   pltpu.VMEM((sb, h, wt), BF16)]
    else:
        mat_specs = [_resident((l, l), lambda j, b: (0, 0))] * 2
        mat_args = list(mats)
        body = _hy_direct_kernel
        scratch = [pltpu.VMEM((sb, l, wt), F32), pltpu.VMEM((sb, l, wt), BF16),
                   pltpu.VMEM((sb, l, wt), BF16), pltpu.VMEM((sb, l, wt), BF16)]
    return pl.pallas_call(
        functools.partial(body, l=l, rc=rc, sb=sb),
        grid=(nw, nb // sb),
        in_specs=[
            *mat_specs,
            seq(hv0), seq(hx10), seq(hx20),
            pl.BlockSpec((3, 3, wt), lambda j, b: (0, 0, j)),
            pl.BlockSpec((3, wt), lambda j, b: (0, j)),
            tab(0), tab(0), tab(1), tab(1),
            pl.BlockSpec((HY_ORDER, wt), lambda j, b: (0, j)),
            pl.BlockSpec((HY_ORDER, wt), lambda j, b: (0, j)),
        ],
        out_specs=pl.BlockSpec((sb * l, wt), lambda j, b: (b, j)),
        out_shape=jax.ShapeDtypeStruct((nb * l, HY_W), BF16),
        scratch_shapes=scratch,
        compiler_params=_params(("parallel", "parallel")),
        name=f"hyena_{l}",
    )(*mat_args, proj, proj, proj, conv_w.reshape(3, 3, HY_W),
      conv_b.reshape(3, HY_W), kc, ki, kc, ki, kn.reshape(HY_ORDER, HY_W), skip)


def _log_sigmoid(x):
    return jnp.minimum(x, 0.0) - jnp.log1p(jnp.exp(-jnp.abs(x)))


def _ret_kernel(*refs, l, hb, use_rope, has_s0, has_acc, st_layers, st_own):
    it = iter(refs)
    dl_ref, q_ref, k_ref, v_ref, g_ref = (next(it) for _ in range(5))
    cos_ref, sin_ref = (next(it), next(it)) if use_rope else (None, None)
    s0_ref = next(it) if has_s0 else None
    if has_acc:
        next(it)
    o_ref, st_ref, kr_scr, kv_scr = (next(it) for _ in range(4))
    for other in range(st_layers):
        if other != st_own:
            st_ref[0, other] = jnp.zeros(st_ref.shape[2:], F32)

    c, dk, dv = RET_CHUNK, RET_DK, RET_DV
    nc = l // c
    unroll = min(nc, 4)
    pos = lax.broadcasted_iota(jnp.int32, (c, 1), 0).astype(F32)
    diff = pos - lax.broadcasted_iota(jnp.int32, (1, c), 1).astype(F32)

    def rope(x, r0):
        if not use_rope:
            return x
        return x * cos_ref[pl.ds(r0, c), :] + pltpu.roll(x, dk // 2, 1) * sin_ref[pl.ds(r0, c), :]

    for h in range(hb):
        qs = slice(h * dk, (h + 1) * dk)
        vs = slice(h * dv, (h + 1) * dv)
        lgf = _log_sigmoid(dl_ref[0, h])
        lgb = _log_sigmoid(dl_ref[1, h])
        dm = jnp.where(diff >= 0, jnp.exp(lgf * diff), jnp.exp(-lgb * diff))
        zeta_f = jnp.exp(lgf * (c - 1.0 - pos))
        zeta_b = jnp.exp(lgb * pos)
        xi_f = jnp.exp(lgf * (pos + 1.0))
        xi_b = jnp.exp(lgb * (c - pos))
        gc_f = jnp.exp(lgf * c)
        gc_b = jnp.exp(lgb * c)

        def kv_step(ci, carry):
            r0 = pl.multiple_of(ci * c, c)
            k = rope(k_ref[pl.ds(r0, c), qs].astype(F32), r0) * (dk ** -0.5)
            kr_scr[pl.ds(r0, c), qs] = k.astype(BF16)
            kk = jnp.concatenate([k * zeta_f, k * zeta_b], axis=1).astype(BF16)
            kv_scr[h, ci] = lax.dot_general(kk, v_ref[pl.ds(r0, c), vs], (((0,), (0,)), ((), ())),
                                            preferred_element_type=F32)
            return carry

        lax.fori_loop(0, nc, kv_step, 0, unroll=unroll)

        def fwd_scan(ci, s):
            upd = kv_scr[h, ci, 0:dk, :]
            kv_scr[h, ci, 0:dk, :] = s
            return gc_f * s + upd

        def bwd_scan(i, r):
            ci = nc - 1 - i
            upd = kv_scr[h, ci, dk:2 * dk, :]
            kv_scr[h, ci, dk:2 * dk, :] = r
            return gc_b * r + upd

        zero = jnp.zeros((dk, dv), F32)
        st_ref[0, st_own, 0, h] = lax.fori_loop(0, nc, fwd_scan, s0_ref[0, 0, 0, h] if has_s0 else zero)
        st_ref[0, st_own, 1, h] = lax.fori_loop(0, nc, bwd_scan, s0_ref[0, 0, 1, h] if has_s0 else zero)

        def out_step(ci, carry):
            r0 = pl.multiple_of(ci * c, c)
            q = rope(q_ref[pl.ds(r0, c), qs].astype(F32), r0)
            sc = lax.dot_general(q.astype(BF16), kr_scr[pl.ds(r0, c), qs], (((1,), (1,)), ((), ())),
                                 preferred_element_type=F32) * dm
            o = _dot(sc.astype(BF16), v_ref[pl.ds(r0, c), vs])
            if has_s0 or nc > 1:
                qq = jnp.concatenate([q * xi_f, q * xi_b], axis=1).astype(BF16)
                o = o + _dot(qq, kv_scr[h, ci].astype(BF16))
            o = o * lax.rsqrt(jnp.mean(o * o, axis=-1, keepdims=True) + EPS)
            o_ref[pl.ds(r0, c), vs] = (_silu(g_ref[pl.ds(r0, c), vs].astype(F32)) * o).astype(BF16)
            return carry

        lax.fori_loop(0, nc, out_step, 0, unroll=unroll)


def _retention_branch(proj, row_blk0, nb, l, decay_logit, rope_tabs, state, layer, n_state_layers=1,
                      states_acc=None):
    nh, dk, dv = N_RET_HEADS, RET_DK, RET_DV
    use_rope = rope_tabs is not None
    has_s0 = state is not None
    has_acc = states_acc is not None
    own_layer = layer if n_state_layers > 1 else 0
    st_layers = 1 if has_acc else n_state_layers
    hb = nh if l <= 512 else 2
    ng = nh // hb
    seq = lambda col, w: pl.BlockSpec((l, hb * w), lambda b, g: (row_blk0 + b, col // (hb * w) + g))
    in_specs = [
        pl.BlockSpec((2, hb, 1, 1), lambda b, g: (0, g, 0, 0)),
        seq(COL_Q, dk), seq(COL_K, dk), seq(COL_V, dv), seq(COL_G, dv),
    ]
    args = [decay_logit.reshape(2, nh, 1, 1), proj, proj, proj, proj]
    if use_rope:
        in_specs += [pl.BlockSpec((l, dk), lambda b, g: (0, 0))] * 2
        args += list(rope_tabs)
    if has_s0:
        in_specs.append(pl.BlockSpec((1, 1, 2, hb, dk, dv), lambda b, g: (b, layer, 0, g, 0, 0)))
        args.append(state)
    if has_acc:
        in_specs.append(pl.BlockSpec(memory_space=pl.ANY))
        args.append(states_acc)
    return pl.pallas_call(
        functools.partial(_ret_kernel, l=l, hb=hb, use_rope=use_rope, has_s0=has_s0, has_acc=has_acc,
                          st_layers=st_layers, st_own=0 if has_acc else own_layer),
        grid=(nb, ng),
        in_specs=in_specs,
        out_specs=[
            pl.BlockSpec((l, hb * dv), lambda b, g: (b, g)),
            pl.BlockSpec((1, st_layers, 2, hb, dk, dv),
                         lambda b, g: (b, own_layer if has_acc else 0, 0, g, 0, 0)),
        ],
        out_shape=[
            jax.ShapeDtypeStruct((nb * l, nh * dv), BF16),
            jax.ShapeDtypeStruct((nb, n_state_layers, 2, nh, dk, dv), F32),
        ],
        input_output_aliases={len(args) - 1: 1} if has_acc else {},
        scratch_shapes=[
            pltpu.VMEM((l, hb * dk), BF16),
            pltpu.VMEM((hb, l // RET_CHUNK, 2 * dk, dv), F32),
        ],
        compiler_params=_params(("parallel", "parallel")),
        name=f"retention_{l}",
    )(*args)


ROW_ALIGN = 16


def _sorted_rows(tm):
    worst = TOP_K * tm + N_EXPERTS * (ROW_ALIGN - 1) + ROW_ALIGN
    return -(-worst // LANES) * LANES


def _mix_kernel(zap_ref, zas_ref, zbp_ref, zbs_ref, gap_ref, gas_ref, gbp_ref, gbs_ref, xp_ref, xs_ref,
                mod_ref, wa_ref, wb_ref, wo_ref, nw_ref, rw_ref, rb_ref, x1_ref, xl_ref, route_ref, seg_ref,
                h_scr, pr_scr, sg_scr, *, d, tm, rmax, n_tiles, n_prompt_tiles):
    i = pl.program_id(0)

    @pl.when(i == 0)
    def _():
        h_scr[...] = jnp.zeros_like(h_scr)
        pr_scr[...] = jnp.zeros_like(pr_scr)
        sg_scr[...] = jnp.zeros_like(sg_scr)

    @pl.when(i > n_tiles)
    def _():
        xl_ref[...] = jnp.zeros_like(xl_ref)

    @pl.when(i <= n_tiles)
    def _():
        _mix_step(zap_ref, zas_ref, zbp_ref, zbs_ref, gap_ref, gas_ref, gbp_ref, gbs_ref, xp_ref, xs_ref,
                  mod_ref, wa_ref, wb_ref, wo_ref, nw_ref, rw_ref, rb_ref, x1_ref, xl_ref, route_ref, seg_ref,
                  h_scr, pr_scr, sg_scr, d=d, tm=tm, rmax=rmax, n_tiles=n_tiles, n_prompt_tiles=n_prompt_tiles)


def _mix_step(zap_ref, zas_ref, zbp_ref, zbs_ref, gap_ref, gas_ref, gbp_ref, gbs_ref, xp_ref, xs_ref,
              mod_ref, wa_ref, wb_ref, wo_ref, nw_ref, rw_ref, rb_ref, x1_ref, xl_ref, route_ref, seg_ref,
              h_scr, pr_scr, sg_scr, *, d, tm, rmax, n_tiles, n_prompt_tiles):
    i = pl.program_id(0)
    r_col = lax.broadcasted_iota(jnp.int32, (rmax, 1), 0).astype(F32)
    in_seg = jnp.logical_and(r_col >= sg_scr[0:1], r_col < sg_scr[0:1] + sg_scr[1:2])
    seg_hot = jnp.where(in_seg, 1.0, 0.0).astype(BF16)
    row_of = _dot(jnp.concatenate([seg_hot, seg_hot], axis=1), pr_scr[...])
    sel = jnp.where(row_of == r_col + 1.0, 1.0, 0.0).astype(BF16)
    xl_ref[...] = _dot(sel, h_scr[...]).astype(BF16)

    is_prompt = jnp.minimum(i, n_tiles - 1) < n_prompt_tiles
    pick = lambda p_ref, s_ref: jnp.where(is_prompt, p_ref[...], s_ref[...])
    za, zb = pick(zap_ref, zas_ref), pick(zbp_ref, zbs_ref)
    ga, gb = pick(gap_ref, gas_ref), pick(gbp_ref, gbs_ref)
    merged = (_sigmoid(ga.astype(F32)) * _dot(za, wa_ref[...])
              + _sigmoid(gb.astype(F32)) * _dot(zb, wb_ref[...]))
    gate1 = mod_ref[0, :, 2 * d:3 * d]
    shift2 = mod_ref[0, :, 3 * d:4 * d]
    scale2 = mod_ref[0, :, 4 * d:5 * d]
    x1 = pick(xp_ref, xs_ref) + gate1 * _dot(merged.astype(BF16), wo_ref[...])
    x1_ref[...] = x1
    hn = x1 * lax.rsqrt(jnp.mean(x1 * x1, axis=-1, keepdims=True) + EPS) * nw_ref[...]
    hmo = hn * (1.0 + scale2) + shift2

    h_hi, h_lo = _split_bf16(hmo)
    both = _dot(h_hi, rw_ref[...])
    logits = (both[:, :LANES] + both[:, LANES:]) + _dot(h_lo, rw_ref[:, :LANES]) + rb_ref[...]
    lane = lax.broadcasted_iota(jnp.int32, (1, LANES), 1)
    idxs, vals = [], []
    for _ in range(TOP_K):
        m = jnp.max(logits, axis=1, keepdims=True)
        ik = jnp.min(jnp.where(logits == m, lane, LANES), axis=1, keepdims=True)
        idxs.append(ik)
        vals.append(m)
        logits = jnp.where(lane == ik, -jnp.inf, logits)
    exps = [jnp.exp(v - vals[0]) for v in vals]
    denom = exps[0] + exps[1] + exps[2] + exps[3]

    hot = jnp.zeros((tm, LANES), F32)
    for ik in idxs:
        hot = hot + jnp.where(lane == ik, 1.0, 0.0)
    r_i = lax.broadcasted_iota(jnp.int32, (tm, tm), 0)
    c_i = lax.broadcasted_iota(jnp.int32, (tm, tm), 1)
    below = jnp.where(c_i < r_i, 1.0, 0.0).astype(BF16)
    before = _dot(below, hot.astype(BF16))
    seg = jnp.ceil(jnp.sum(hot, axis=0, keepdims=True) * (1.0 / ROW_ALIGN)) * ROW_ALIGN
    seg_ref[0] = seg
    e_r = lax.broadcasted_iota(jnp.int32, (LANES, LANES), 0)
    e_c = lax.broadcasted_iota(jnp.int32, (LANES, LANES), 1)
    seg_start = _dot3(jnp.broadcast_to(seg, (8, LANES)), jnp.where(e_r < e_c, 1.0, 0.0))[0:1]
    base = seg_start + before

    route = jnp.zeros((tm, LANES), F32)
    for k in range(TOP_K):
        srow = jnp.sum(jnp.where(lane == idxs[k], base, 0.0), axis=1, keepdims=True)
        route = route + jnp.where(lane == k, srow, 0.0)
        route = route + jnp.where(lane == TOP_K + k, idxs[k].astype(F32), 0.0)
        route = route + jnp.where(lane == 2 * TOP_K + k, exps[k] / denom, 0.0)
    route_ref[...] = route

    h_scr[...] = h_hi
    pair_row = jnp.where(hot > 0.0, base + 1.0, 0.0).T
    pr_scr[...] = jnp.concatenate(_split_bf16(pair_row), axis=0)
    sg_scr[0:1] = seg_start
    sg_scr[1:2] = seg


def _mix(za, zb, proj, x, mod3, mod_row, wa, wb, wo, norm_w, rw, rb, tm, spare_rows):
    d = x[0].shape[1]
    n_p = x[0].shape[0] // tm
    n_s = x[1].shape[0] // tm
    nt = n_p + n_s
    t = nt * tm
    rmax = _sorted_rows(tm)
    spare_blocks = -(-spare_rows // rmax)
    vw = N_RET_HEADS * RET_DV
    cur = lambda i: jnp.minimum(i, nt - 1)
    prev = lambda i: jnp.maximum(i - 1, 0)
    pair = lambda w, col: [pl.BlockSpec((tm, w), lambda i: (jnp.minimum(cur(i), n_p - 1), col)),
                           pl.BlockSpec((tm, w), lambda i: (jnp.maximum(cur(i) - n_p, 0), col))]
    const = lambda i: (0, 0)
    return pl.pallas_call(
        functools.partial(_mix_kernel, d=d, tm=tm, rmax=rmax, n_tiles=nt, n_prompt_tiles=n_p),
        grid=(nt + 1 + spare_blocks,),
        in_specs=[
            *pair(HY_W, 0), *pair(vw, 0), *pair(d, COL_GA // d), *pair(d, COL_GB // d), *pair(d, 0),
            pl.BlockSpec((1, 1, 6 * d), lambda i: (mod_row(cur(i)), 0, 0)),
            _resident((HY_W, d), const), _resident((vw, d), const), _resident((d, d), const),
            pl.BlockSpec((1, d), const),
            _resident((d, 2 * LANES), const), pl.BlockSpec((1, LANES), const),
        ],
        out_specs=[
            pl.BlockSpec((tm, d), lambda i: (cur(i), 0)),
            pl.BlockSpec((rmax, d), lambda i: (prev(i), 0)),
            pl.BlockSpec((tm, LANES), lambda i: (cur(i), 0)),
            pl.BlockSpec((1, 1, LANES), lambda i: (cur(i), 0, 0)),
        ],
        out_shape=[
            jax.ShapeDtypeStruct((t, d), F32),
            jax.ShapeDtypeStruct(((nt + spare_blocks) * rmax, d), BF16),
            jax.ShapeDtypeStruct((t, LANES), F32),
            jax.ShapeDtypeStruct((nt, 1, LANES), F32),
        ],
        scratch_shapes=[pltpu.VMEM((tm, d), BF16), pltpu.VMEM((2 * LANES, tm), BF16),
                        pltpu.VMEM((2, LANES), F32)],
        compiler_params=_params(("arbitrary",)),
        name="merge_router",
    )(*za, *zb, *proj, *proj, *x, mod3, wa, wb, wo, norm_w.reshape(1, d), rw, rb)


def _ffn_kernel(te_ref, nu_ref, src_ref, dst_ref, xl_ref, wgu_ref, bgu_ref, wd_ref, bd_ref, yl_ref,
                xbuf, ybuf, gsem, ssem, wgu_scr, wd_scr, *, dff, te):
    j = pl.program_id(0)
    nu = nu_ref[0]
    cpt = te // ROW_ALIGN
    slot = j % 2

    def hbm_chunk(ci):
        return pl.ds(pl.multiple_of(ci * ROW_ALIGN, ROW_ALIGN), ROW_ALIGN)

    def start_gather(tile, buf):
        for ci in range(cpt):
            pltpu.make_async_copy(xl_ref.at[hbm_chunk(src_ref[tile * cpt + ci])],
                                  xbuf.at[buf, pl.ds(ci * ROW_ALIGN, ROW_ALIGN)], gsem.at[buf]).start()

    def wait_gather(buf):
        pltpu.make_async_copy(xl_ref.at[pl.ds(0, te)], xbuf.at[buf], gsem.at[buf]).wait()

    def start_scatter(tile, buf):
        for ci in range(cpt):
            pltpu.make_async_copy(ybuf.at[buf, pl.ds(ci * ROW_ALIGN, ROW_ALIGN)],
                                  yl_ref.at[hbm_chunk(dst_ref[tile * cpt + ci])], ssem.at[buf]).start()

    def wait_scatter(buf):
        pltpu.make_async_copy(ybuf.at[buf], yl_ref.at[pl.ds(0, te)], ssem.at[buf]).wait()

    @pl.when(j < nu)
    def _():
        @pl.when(j == 0)
        def _():
            start_gather(0, 0)

        start_gather(j + 1, 1 - slot)
        wait_gather(slot)

        @pl.when(j >= 2)
        def _():
            wait_scatter(slot)

        @pl.when(jnp.logical_or(j == 0, te_ref[j] != te_ref[jnp.maximum(j - 1, 0)]))
        def _():
            wgu_scr[...] = wgu_ref[0].astype(BF16)
            wd_scr[...] = wd_ref[0].astype(BF16)

        hgu = _dot(xbuf[slot], wgu_scr[...]) + bgu_ref[0]
        gate = jnp.minimum(hgu[:, :dff], SWIGLU_LIMIT)
        up = jnp.clip(hgu[:, dff:], -SWIGLU_LIMIT, SWIGLU_LIMIT)
        act = gate * _sigmoid(SWIGLU_ALPHA * gate) * (up + 1.0)
        ybuf[slot] = (_dot(act.astype(BF16), wd_scr[...]) + bd_ref[0]).astype(BF16)
        start_scatter(j, slot)

        @pl.when(j == nu - 1)
        def _():
            wait_gather(1 - slot)

            @pl.when(j >= 1)
            def _():
                wait_scatter(1 - slot)
            wait_scatter(slot)


def _expert_ffn(xl, tile_expert, n_used, src_chunk, dst_chunk, w_gate_up, b_gate_up, w_down, b_down, layer, te):
    _, d = xl.shape
    depth, n_exp, _, two_ff = w_gate_up.shape
    ne = depth * n_exp
    dff = two_ff // 2
    wsel = lambda j, tex, nu, src, dst: (layer * n_exp + tex[j], 0, 0)
    return pl.pallas_call(
        functools.partial(_ffn_kernel, dff=dff, te=te),
        grid_spec=pltpu.PrefetchScalarGridSpec(
            num_scalar_prefetch=4,
            grid=(tile_expert.shape[0],),
            in_specs=[
                pl.BlockSpec(memory_space=pl.ANY),
                pl.BlockSpec((1, d, two_ff), wsel),
                pl.BlockSpec((1, 1, two_ff), wsel),
                pl.BlockSpec((1, dff, d), wsel),
                pl.BlockSpec((1, 1, d), wsel),
            ],
            out_specs=pl.BlockSpec(memory_space=pl.ANY),
            scratch_shapes=[
                pltpu.VMEM((2, te, d), BF16), pltpu.VMEM((2, te, d), BF16),
                pltpu.SemaphoreType.DMA((2,)), pltpu.SemaphoreType.DMA((2,)),
                pltpu.VMEM((d, two_ff), BF16), pltpu.VMEM((dff, d), BF16),
            ],
        ),
        out_shape=jax.ShapeDtypeStruct(xl.shape, BF16),
        input_output_aliases={4: 0},
        compiler_params=_params(("arbitrary",)),
        name="moe_ffn",
    )(tile_expert, n_used, src_chunk, dst_chunk, xl, w_gate_up.reshape(ne, d, two_ff),
      b_gate_up.reshape(ne, 1, two_ff), w_down.reshape(ne, dff, d), b_down.reshape(ne, 1, d))


def _comb_kernel(yl_ref, route_ref, x1_ref, mod_ref, fw_ref, o_ref, *, d, tm, rmax, final_norm):
    route = route_ref[...]
    r_iota = lax.broadcasted_iota(jnp.int32, (1, rmax), 1).astype(F32)
    wm = jnp.zeros((tm, rmax), F32)
    for k in range(TOP_K):
        wm = wm + jnp.where(r_iota == route[:, k:k + 1], route[:, 2 * TOP_K + k:2 * TOP_K + k + 1], 0.0)
    out = _dot(wm.astype(BF16), yl_ref[...])
    x2 = x1_ref[...] + mod_ref[0, :, 5 * d:6 * d] * out
    if final_norm:
        x2 = x2 * lax.rsqrt(jnp.mean(x2 * x2, axis=-1, keepdims=True) + EPS) * fw_ref[...]
    o_ref[...] = x2


def _combine(yl, route, x1, mod3, mod_row, final_w, tile0, n_tiles, tm, final_norm):
    d = x1.shape[1]
    rmax = _sorted_rows(tm)
    return pl.pallas_call(
        functools.partial(_comb_kernel, d=d, tm=tm, rmax=rmax, final_norm=final_norm),
        grid=(n_tiles,),
        in_specs=[
            pl.BlockSpec((rmax, d), lambda i: (tile0 + i, 0)),
            pl.BlockSpec((tm, LANES), lambda i: (tile0 + i, 0)),
            pl.BlockSpec((tm, d), lambda i: (tile0 + i, 0)),
            pl.BlockSpec((1, 1, 6 * d), lambda i: (mod_row(tile0 + i), 0, 0)),
            pl.BlockSpec((1, d), lambda i: (0, 0)),
        ],
        out_specs=pl.BlockSpec((tm, d), lambda i: (i, 0)),
        out_shape=jax.ShapeDtypeStruct((n_tiles * tm, d), F32),
        compiler_params=_params(("parallel",)),
        name="moe_combine",
    )(yl, route, x1, mod3, final_w.reshape(1, d))


@functools.lru_cache(maxsize=None)
def _dft_mats(l):
    s = np.arange(l, dtype=np.int64)
    f = np.concatenate([np.arange(0, l, 2), np.arange(1, l, 2)]).astype(np.int64) if _fold_dft(l) else s
    ang = ((f[:, None] * s[None, :]) % (2 * l)).astype(np.float64) * (math.pi / l)
    mats = (np.cos(ang).astype(np.float32), np.sin(ang).astype(np.float32))
    if _fold_dft(l):
        a = np.arange(l // 2)
        mats += (((a[:, None] + a[None, :] == l // 2) & (a[:, None] >= 1)).astype(np.float32),)
    return mats


@functools.lru_cache(maxsize=None)
def _filter_features(l):
    f32 = np.float32
    t = np.linspace(0.0, 1.0, l, dtype=f32)[:, None]
    bands = (HY_EMB - 1) // 2
    w = f32(2.0 * math.pi) * np.arange(l, dtype=f32)[:, None] / f32(l)
    f = np.linspace(1e-4, bands - 1, bands, dtype=f32)[None, :]
    z = np.concatenate([t, np.cos(f * w), -np.sin(f * w)], axis=-1).astype(f32)
    return np.pad(z, ((0, 0), (0, LANES - HY_EMB))), t


@functools.lru_cache(maxsize=None)
def _rope_tables(rows):
    f32 = np.float32
    row = np.repeat(np.arange(rows, dtype=f32), GRID_W)
    col = np.tile(np.arange(GRID_W, dtype=f32), rows)
    half = RET_DK // 4
    inv = (f32(ROPE_BASE) ** (-np.arange(half, dtype=f32) / f32(half))).astype(f32)
    ang = np.concatenate([row[:, None] * inv, col[:, None] * inv], axis=-1)
    ang = np.concatenate([ang, ang], axis=-1).astype(f32)
    sign = np.where(np.arange(RET_DK) < RET_DK // 2, -1.0, 1.0).astype(f32)
    return np.cos(ang), np.sin(ang) * sign


@functools.lru_cache(maxsize=None)
def _decay_rates():
    return np.linspace(math.log(HY_TARGET) / HY_SLOW_PCT, math.log(HY_TARGET) / HY_FAST_PCT,
                       HY_W, dtype=np.float32)[None, :]


def _pad2(a, rows, cols):
    return jnp.pad(a, ((0, rows - a.shape[0]), (0, cols - a.shape[1])))


def _ffn_plan(seg, rmax, te, n_ffn_tiles):
    nt, ne = seg.shape
    cum_t = jnp.cumsum(seg, axis=0)
    tot = cum_t[-1]
    padded = ((tot + te - 1) // te) * te
    ends = jnp.cumsum(padded)
    offs = ends - padded
    n_used = ends[-1] // te
    expert_of = lambda rows: jnp.minimum(jnp.sum(ends[None, :] <= rows[:, None], axis=1), ne - 1)
    tiles = jnp.arange(n_ffn_tiles, dtype=jnp.int32)
    tile_expert = expert_of(jnp.minimum(tiles, n_used - 1) * te).astype(jnp.int32)

    rows = jnp.arange(n_ffn_tiles * te // ROW_ALIGN, dtype=jnp.int32) * ROW_ALIGN
    hot_e = expert_of(rows)[:, None] == jnp.arange(ne)[None, :]
    pick_e = lambda tab: jnp.sum(jnp.where(hot_e[:, None, :], tab[None], 0), axis=2)
    r_s = rows - pick_e(offs[None, :])[:, 0]
    cum_sel = pick_e(cum_t)
    i_s = jnp.minimum(jnp.sum(cum_sel <= r_s[:, None], axis=1), nt - 1)
    hot_i = i_s[:, None] == jnp.arange(nt)[None, :]
    seg_base = (jnp.arange(nt)[:, None] * rmax + jnp.cumsum(seg, axis=1) - seg) - (cum_t - seg)
    src_row = r_s + jnp.sum(jnp.where(hot_i, pick_e(seg_base), 0), axis=1)
    valid = (r_s < pick_e(tot[None, :])[:, 0]) & (rows < ends[-1])
    src_chunk = jnp.where(valid, src_row // ROW_ALIGN, rmax // ROW_ALIGN - 1).astype(jnp.int32)
    pad_rank = jnp.cumsum(jnp.where(valid, 0, 1)) - 1
    spare = nt * rmax // ROW_ALIGN + jnp.minimum(pad_rank, _spare_chunks(te) - 1)
    dst_chunk = jnp.where(valid, src_row // ROW_ALIGN, spare).astype(jnp.int32)
    return tile_expert, n_used.reshape(1).astype(jnp.int32), src_chunk, dst_chunk


def _spare_chunks(te):
    return N_EXPERTS * (te // ROW_ALIGN - 1)


def kernel(x_prompt, x_sample, state_ret, c, c_ctx, ada_w, ada_b, norm_mix_w, w_in, conv_w, conv_b,
           f_w1, f_b1, f_w2, f_b2, f_w3, f_b3, f_freq, f_wout, hy_skip, ret_decay_logit,
           w_branch_a, w_branch_b, w_out, norm_ffn_w, router_w, router_b, w_gate_up, b_gate_up,
           w_down, b_down, final_norm_w):
    bp, lp, d = x_prompt.shape
    bs, ls, _ = x_sample.shape
    depth = ada_w.shape[0]
    tp, ts = bp * lp, bs * ls
    t = tp + ts
    assert ls % lp == 0 and tp % ls == 0 and lp % RET_CHUNK == 0 and ls % GRID_W == 0
    tm = 512 if (tp % 512 == 0 and ls % 512 == 0) else lp
    tm_in = 1024 if (tp % 1024 == 0 and ls % 1024 == 0) else tm
    te = 512
    nt = t // tm
    rmax = _sorted_rows(tm)
    n_ffn_tiles = -(-(t * TOP_K + nt * N_EXPERTS * (ROW_ALIGN - 1)) // te) + N_EXPERTS

    dft = {l: tuple(jnp.asarray(a).astype(BF16) for a in _dft_mats(l)) for l in (lp, ls)}
    feats = {l: tuple(jnp.asarray(a) for a in _filter_features(l)) for l in (lp, ls)}
    rope_tabs = tuple(jnp.asarray(a) for a in _rope_tables(ls // GRID_W))
    deltas = jnp.asarray(_decay_rates())

    cvec = jnp.zeros((8, d), F32).at[:bs].set(c).at[bs].set(c_ctx)
    mod3 = _modulation(cvec, ada_w, ada_b).reshape(depth * 8, 1, 6 * d)
    n_p, n_s = tp // tm, ts // tm

    x = (x_prompt.reshape(tp, d), x_sample.reshape(ts, d))
    states = None
    for l in range(depth):
        mod_row = lambda i, l=l: l * 8 + jnp.where(i < n_p, bs, jnp.maximum(i - n_p, 0) // (ls // tm))
        w_in_bf = w_in[l].astype(BF16)
        proj = (_in_proj(x[0], norm_mix_w[l], mod3, lambda i, l=l: l * 8 + bs, w_in_bf, tm_in),
                _in_proj(x[1], norm_mix_w[l], mod3, lambda i, l=l: l * 8 + i // (ls // tm_in), w_in_bf, tm_in))

        w1 = _pad2(f_w1[l], LANES, LANES)
        w2 = _pad2(f_w2[l], LANES, LANES)
        w3 = _pad2(f_w3[l], LANES, LANES)
        wo = _pad2(f_wout[l], LANES, f_wout.shape[2])
        b1, b2, b3, fr = (_pad2(a[l][None, :], 1, LANES) for a in (f_b1, f_b2, f_b3, f_freq))
        za, zb = [], []
        for grp, (nb, ln) in enumerate(((bp, lp), (bs, ls))):
            taps, sums = _hyena_taps(ln, feats[ln][0], feats[ln][1], deltas, w1, b1, w2, b2, w3, b3, fr, wo)
            kc, ki, kn = _filter_spectra(ln, dft[ln][0], dft[ln][1], taps, sums)
            za.append(_hyena_branch(proj[grp], 0, nb, ln, dft[ln], conv_w[l], conv_b[l], kc, ki, kn, hy_skip[l]))
            if grp == 0:
                zb_g, states = _retention_branch(proj[grp], 0, nb, ln, ret_decay_logit[l], None, None, l,
                                                 depth, states)
            else:
                zb_g, _ = _retention_branch(proj[grp], 0, nb, ln, ret_decay_logit[l], rope_tabs, state_ret, l)
            zb.append(zb_g)

        rw = jnp.concatenate(_split_bf16(_pad2(router_w[l], d, LANES)), axis=1)
        rb = jnp.full((1, LANES), -jnp.inf, F32).at[0, :N_EXPERTS].set(router_b[l])
        x1, xl, route, seg = _mix(za, zb, proj, x, mod3, mod_row,
                                  w_branch_a[l].astype(BF16), w_branch_b[l].astype(BF16),
                                  w_out[l].astype(BF16), norm_ffn_w[l], rw, rb, tm, _spare_chunks(te) * ROW_ALIGN)
        tile_expert, n_used, src_chunk, dst_chunk = _ffn_plan(seg[:, 0, :N_EXPERTS].astype(jnp.int32), rmax, te,
                                                              n_ffn_tiles)
        yl = _expert_ffn(xl, tile_expert, n_used, src_chunk, dst_chunk, w_gate_up, b_gate_up, w_down, b_down,
                         l, te)
        last = l == depth - 1
        x = (_combine(yl, route, x1, mod3, mod_row, final_norm_w, 0, n_p, tm, last),
             _combine(yl, route, x1, mod3, mod_row, final_norm_w, n_p, n_s, tm, last))

    return (x[0].reshape(bp, lp, d), x[1].reshape(bs, ls, d), states)
```

```python
import functools
import math

import jax
import jax.numpy as jnp
import numpy as np
from jax import lax
from jax.experimental import pallas as pl
from jax.experimental.pallas import tpu as pltpu

F32 = jnp.float32
BF16 = jnp.bfloat16

GRID_W = 64
HY_W = 512
HY_ORDER = 2
HY_EMB = 33
HY_FAST_PCT = 0.3
HY_SLOW_PCT = 1.5
HY_TARGET = 1e-2
N_RET_HEADS = 4
RET_DK = 128
RET_DV = 256
RET_CHUNK = 256
ROPE_BASE = 10000.0
N_EXPERTS = 32
TOP_K = 4
SWIGLU_LIMIT = 7.0
SWIGLU_ALPHA = 1.702
EPS = 1e-6

LANES = 128
VMEM_LIMIT = 56 * 1024 * 1024

COL_V, COL_G, COL_GA, COL_GB = 0, 1024, 2048, 3072
COL_HV, COL_HX1, COL_HX2, COL_Q, COL_K = 4096, 4608, 5120, 5632, 6144
IN_W = 6656
IN_COL_BLOCKS = (5, 6, 7, 8, 9, 10, 11, 12, 0, 1, 2, 3, 4)


def _dot(a, b):
    return jnp.dot(a, b, preferred_element_type=F32)


def _split_bf16(a):
    hi = a.astype(BF16)
    lo = (a - hi.astype(F32)).astype(BF16)
    return hi, lo


def _dot3(a, b):
    ah, al = _split_bf16(a)
    bh, bl = _split_bf16(b)
    return _dot(ah, bh) + (_dot(ah, bl) + _dot(al, bh))


def _sigmoid(x):
    return 0.5 * jnp.tanh(0.5 * x) + 0.5


def _silu(x):
    return x * _sigmoid(x)


def _params(sem, vmem=VMEM_LIMIT):
    return pltpu.CompilerParams(dimension_semantics=sem, vmem_limit_bytes=vmem)


def _resident(shape, index_map):
    return pl.BlockSpec(shape, index_map, pipeline_mode=pl.Buffered(1))


def _mod_kernel(c_ref, w_ref, b_ref, o_ref):
    o_ref[0] = _dot3(_silu(c_ref[...]), w_ref[0]) + b_ref[0]


def _modulation(cvec, ada_w, ada_b):
    depth, d, six_d = ada_w.shape
    tn = six_d // 4
    return pl.pallas_call(
        _mod_kernel,
        grid=(depth, six_d // tn),
        in_specs=[
            pl.BlockSpec((8, d), lambda l, j: (0, 0)),
            pl.BlockSpec((1, d, tn), lambda l, j: (l, 0, j)),
            pl.BlockSpec((1, 1, tn), lambda l, j: (l, 0, j)),
        ],
        out_specs=pl.BlockSpec((1, 8, tn), lambda l, j: (l, 0, j)),
        out_shape=jax.ShapeDtypeStruct((depth, 8, six_d), F32),
        compiler_params=_params(("parallel", "parallel")),
        name="adaln_mod",
    )(cvec, ada_w, ada_b.reshape(depth, 1, six_d))


def _in_kernel(x_ref, nw_ref, mod_ref, w_ref, o_ref, *, d, cw):
    x = x_ref[...]
    y = x * lax.rsqrt(jnp.mean(x * x, axis=-1, keepdims=True) + EPS) * nw_ref[...]
    shift = mod_ref[0, :, 0:d]
    scale = mod_ref[0, :, d:2 * d]
    u = (y * (1.0 + scale) + shift).astype(BF16)
    for j, src in enumerate(IN_COL_BLOCKS):
        o_ref[:, j * cw:(j + 1) * cw] = _dot(u, w_ref[:, src * cw:(src + 1) * cw]).astype(BF16)


def _in_proj(x, norm_w, mod3, mod_row, w_in_bf, tm):
    t, d = x.shape
    return pl.pallas_call(
        functools.partial(_in_kernel, d=d, cw=IN_W // len(IN_COL_BLOCKS)),
        grid=(t // tm,),
        in_specs=[
            pl.BlockSpec((tm, d), lambda i: (i, 0)),
            pl.BlockSpec((1, d), lambda i: (0, 0)),
            pl.BlockSpec((1, 1, 2 * d), lambda i: (mod_row(i), 0, 0)),
            _resident((d, IN_W), lambda i: (0, 0)),
        ],
        out_specs=pl.BlockSpec((tm, IN_W), lambda i: (i, 0)),
        out_shape=jax.ShapeDtypeStruct((t, IN_W), BF16),
        compiler_params=_params(("parallel",)),
        name="in_proj",
    )(x, norm_w.reshape(1, d), mod3, w_in_bf)


def _filt_kernel(z_ref, t_ref, w1, b1, w2, b2, w3, b3, fr, wo, dl_ref, h_ref, s_ref, *, tl):
    i = pl.program_id(0)
    freq = fr[...]
    h = jnp.sin(freq * (_dot3(z_ref[...], w1[...]) + b1[...]))
    h = jnp.sin(freq * (_dot3(h, w2[...]) + b2[...]))
    h = jnp.sin(freq * (_dot3(h, w3[...]) + b3[...]))
    h = _dot3(h, wo[...])
    decay = jnp.exp(-t_ref[...] * jnp.abs(dl_ref[...]))
    h = h * jnp.concatenate([decay] * (2 * HY_ORDER), axis=1)
    h_ref[...] = h
    row = i * tl + lax.broadcasted_iota(jnp.int32, (tl, 1), 0)
    col = lax.broadcasted_iota(jnp.int32, (1, 2 * HY_ORDER * HY_W), 1)
    is_bwd = ((col // HY_W) % 2) == 1
    part = jnp.sum(jnp.where((row == 0) & is_bwd, 0.0, jnp.abs(h)), axis=0, keepdims=True)

    @pl.when(i == 0)
    def _():
        s_ref[...] = jnp.zeros_like(s_ref)

    s_ref[...] += part


def _hyena_taps(l, zfeat, tcol, deltas, w1, b1, w2, b2, w3, b3, freq, wout):
    tl = min(l, 256)
    nw = 2 * HY_ORDER * HY_W
    full = lambda shape: pl.BlockSpec(shape, lambda i: (0,) * len(shape))
    return pl.pallas_call(
        functools.partial(_filt_kernel, tl=tl),
        grid=(l // tl,),
        in_specs=[
            pl.BlockSpec((tl, LANES), lambda i: (i, 0)),
            pl.BlockSpec((tl, 1), lambda i: (i, 0)),
            full((LANES, LANES)), full((1, LANES)),
            full((LANES, LANES)), full((1, LANES)),
            full((LANES, LANES)), full((1, LANES)),
            full((1, LANES)),
            full((LANES, nw)),
            full((1, HY_W)),
        ],
        out_specs=[pl.BlockSpec((tl, nw), lambda i: (i, 0)), pl.BlockSpec((1, nw), lambda i: (0, 0))],
        out_shape=[jax.ShapeDtypeStruct((l, nw), F32), jax.ShapeDtypeStruct((1, nw), F32)],
        compiler_params=_params(("arbitrary",)),
        name=f"hyena_taps_{l}",
    )(zfeat, tcol, w1, b1, w2, b2, w3, b3, freq, wout, deltas)


def _fdft_kernel(c_ref, s_ref, hf_ref, hb_ref, sf_ref, sb_ref, kc_ref, ki_ref, kn_ref, *, l):
    row = lax.broadcasted_iota(jnp.int32, (l, 1), 0)
    hf = hf_ref[...]
    hb = jnp.where(row == 0, 0.0, hb_ref[...])
    inv = 1.0 / (sf_ref[...] + sb_ref[...])
    a = hf + hb
    b = hb - hf
    wf = jnp.where(row == 0, 0.5 / l, 1.0 / l)
    kc_ref[...] = _dot(c_ref[...], a.astype(BF16)) * wf * inv
    ki_ref[...] = _dot(s_ref[...], b.astype(BF16)) * wf * inv
    sgn = jnp.where(row % 2 == 0, 1.0, -1.0)
    kn_ref[...] = jnp.sum(a * sgn, axis=0, keepdims=True) * inv * (0.5 / l)


def _filter_spectra(l, cmat, smat, taps, sums):
    wt = 256
    nb = HY_W // wt
    two_w = 2 * HY_W
    return pl.pallas_call(
        functools.partial(_fdft_kernel, l=l),
        grid=(HY_ORDER, nb),
        in_specs=[
            _resident((l, l), lambda o, j: (0, 0)),
            _resident((l, l), lambda o, j: (0, 0)),
            pl.BlockSpec((l, wt), lambda o, j: (0, o * 2 * nb + j)),
            pl.BlockSpec((l, wt), lambda o, j: (0, o * 2 * nb + nb + j)),
            pl.BlockSpec((1, wt), lambda o, j: (0, o * 2 * nb + j)),
            pl.BlockSpec((1, wt), lambda o, j: (0, o * 2 * nb + nb + j)),
        ],
        out_specs=[
            pl.BlockSpec((l, wt), lambda o, j: (0, o * nb + j)),
            pl.BlockSpec((l, wt), lambda o, j: (0, o * nb + j)),
            pl.BlockSpec((1, wt), lambda o, j: (0, o * nb + j)),
        ],
        out_shape=[
            jax.ShapeDtypeStruct((l, HY_ORDER * HY_W), F32),
            jax.ShapeDtypeStruct((l, HY_ORDER * HY_W), F32),
            jax.ShapeDtypeStruct((1, HY_ORDER * HY_W), F32),
        ],
        compiler_params=_params(("parallel", "parallel")),
        name=f"hyena_spectra_{l}",
    )(cmat, smat, taps, taps, sums, sums)


def _hy_direct_kernel(c_ref, s_ref, hv_ref, hx1_ref, hx2_ref, cw_ref, cb_ref,
                      kc0, ki0, kc1, ki1, kn_ref, sk_ref, o_ref, z_scr, zb_scr, p_scr, q_scr, *, l, rc, sb):
    row = lax.broadcasted_iota(jnp.int32, (l, 1), 0)
    first = row == 0
    last = row == l - 1
    sgn = jnp.where(row % 2 == 0, 1.0, -1.0)
    cw = cw_ref[...]
    cb = cb_ref[...]

    def short_conv(ref, rows, part):
        x = ref[rows, :].astype(F32)
        xm = jnp.where(first, 0.0, pltpu.roll(x, 1, 0))
        xp = jnp.where(last, 0.0, pltpu.roll(x, l - 1, 0))
        return (xm * cw[0, part:part + 1] + x * cw[1, part:part + 1] + xp * cw[2, part:part + 1]
                + cb[part:part + 1])

    for s in range(sb):
        rows = slice(s * l, (s + 1) * l)
        z_scr[s] = short_conv(hv_ref, rows, 0)
        for o, (gate_ref, kc, ki) in enumerate(((hx1_ref, kc0, ki0), (hx2_ref, kc1, ki1))):
            z = z_scr[s]
            zb_scr[s] = z.astype(BF16)
            nyq = jnp.sum(z * sgn, axis=0, keepdims=True) * kn_ref[o:o + 1]
            for r in range(l // rc):
                sl = slice(r * rc, (r + 1) * rc)
                zc = _dot(c_ref[sl, :], zb_scr[s])
                zs = _dot(s_ref[sl, :], zb_scr[s])
                p_scr[s, sl, :] = (zc * kc[sl, :] + zs * ki[sl, :]).astype(BF16)
                q_scr[s, sl, :] = (zs * kc[sl, :] - zc * ki[sl, :]).astype(BF16)
            gate = short_conv(gate_ref, rows, o + 1)
            for r in range(l // rc):
                sl = slice(r * rc, (r + 1) * rc)
                y = _dot(c_ref[sl, :], p_scr[s]) + _dot(s_ref[sl, :], q_scr[s])
                y = y + sgn[sl] * nyq + z_scr[s, sl, :] * sk_ref[o:o + 1]
                z_scr[s, sl, :] = gate[sl] * y
        o_ref[rows, :] = z_scr[s].astype(BF16)


def _hy_kernel(ce_ref, se_ref, co_ref, so_ref, cot_ref, sot_ref, j_ref, hv_ref, hx1_ref, hx2_ref, cw_ref, cb_ref,
               kc0, ki0, kc1, ki1, kn_ref, sk_ref, o_ref, z_scr, zp_scr, zm_scr, p_scr, q_scr, yr_scr,
               *, l, rc, sb):
    h = l // 2
    row = lax.broadcasted_iota(jnp.int32, (l, 1), 0)
    first = row == 0
    last = row == l - 1
    sgn = jnp.where(row % 2 == 0, 1.0, -1.0)
    half_row = lax.broadcasted_iota(jnp.int32, (h, 1), 0)
    sm = jnp.where(half_row % 2 == 0, 1.0, -1.0)
    cw = cw_ref[...]
    cb = cb_ref[...]

    def short_conv(ref, rows, part):
        x = ref[rows, :].astype(F32)
        xm = jnp.where(first, 0.0, pltpu.roll(x, 1, 0))
        xp = jnp.where(last, 0.0, pltpu.roll(x, l - 1, 0))
        return (xm * cw[0, part:part + 1] + x * cw[1, part:part + 1] + xp * cw[2, part:part + 1]
                + cb[part:part + 1])

    for s in range(sb):
        rows = slice(s * l, (s + 1) * l)
        z_scr[s] = short_conv(hv_ref, rows, 0)
        for o, (gate_ref, kc, ki) in enumerate(((hx1_ref, kc0, ki0), (hx2_ref, kc1, ki1))):
            z = z_scr[s]
            nyq = jnp.sum(z * sgn, axis=0, keepdims=True) * kn_ref[o:o + 1]
            zmid = z[h:h + 1]
            zrev = _dot(j_ref[...], z[h:].astype(BF16))
            zp_scr[s] = (z[:h] + zrev).astype(BF16)
            zm_scr[s] = (z[:h] - zrev).astype(BF16)
            ymid = jnp.zeros_like(zmid)
            for r in range(h // rc):
                sl = slice(r * rc, (r + 1) * rc)
                so_ = slice(h + r * rc, h + (r + 1) * rc)
                zc_e = _dot(ce_ref[sl, :], zp_scr[s]) + sm[sl] * zmid
                zs_e = _dot(se_ref[sl, :], zm_scr[s])
                zc_o = _dot(co_ref[sl, :], zm_scr[s])
                zs_o = _dot(so_ref[sl, :], zp_scr[s]) + sm[sl] * zmid
                pe = zc_e * kc[sl, :] + zs_e * ki[sl, :]
                qe = zs_e * kc[sl, :] - zc_e * ki[sl, :]
                po = zc_o * kc[so_, :] + zs_o * ki[so_, :]
                qo = zs_o * kc[so_, :] - zc_o * ki[so_, :]
                ymid = ymid + jnp.sum(sm[sl] * (pe + qo), axis=0, keepdims=True)
                p_scr[s, sl, :] = pe.astype(BF16)
                q_scr[s, sl, :] = qe.astype(BF16)
                p_scr[s, so_, :] = po.astype(BF16)
                q_scr[s, so_, :] = qo.astype(BF16)
            gate = short_conv(gate_ref, rows, o + 1)
            skip = sk_ref[o:o + 1]
            for r in range(h // rc):
                sl = slice(r * rc, (r + 1) * rc)
                a1 = _dot(ce_ref[sl, :], p_scr[s, :h, :])
                a2 = _dot(se_ref[sl, :], q_scr[s, :h, :])
                a3 = _dot(cot_ref[sl, :], p_scr[s, h:, :])
                a4 = _dot(sot_ref[sl, :], q_scr[s, h:, :])
                yr_scr[s, sl, :] = ((a1 + a4) - (a2 + a3)).astype(BF16)
                y = (a1 + a4) + (a2 + a3) + sgn[sl] * nyq + z[sl] * skip
                z_scr[s, sl, :] = gate[sl] * y
            y_hi = jnp.where(half_row == 0, ymid, _dot(j_ref[...], yr_scr[s]))
            y_hi = y_hi + sgn[h:] * nyq + z[h:] * skip
            z_scr[s, h:, :] = gate[h:] * y_hi
        o_ref[rows, :] = z_scr[s].astype(BF16)


def _fold_dft(l):
    return l >= 1024


def _hyena_branch(proj, row_blk0, nb, l, mats, conv_w, conv_b, kc, ki, kn, skip):
    wt = 256
    nw = HY_W // wt
    h = l // 2
    folded = _fold_dft(l)
    rc = min(h if folded else l, 512)
    sb = math.gcd(nb, max(1, 2048 // l))
    assert row_blk0 % sb == 0
    hv0, hx10, hx20 = COL_HV // wt, COL_HX1 // wt, COL_HX2 // wt
    seq = lambda c0: pl.BlockSpec((sb * l, wt), lambda j, b: (row_blk0 // sb + b, c0 + j))
    tab = lambda o: pl.BlockSpec((l, wt), lambda j, b: (0, o * nw + j))
    if folded:
        cmat, smat, jmat = mats
        quarter = lambda blk: _resident((h, h), lambda j, b: (blk, 0))
        mat_specs = [quarter(0), quarter(0), quarter(1), quarter(1),
                     quarter(0), quarter(0), quarter(0)]
        mat_args = [cmat, smat, cmat, smat, cmat[h:, :h].T, smat[h:, :h].T, jmat]
        body = _hy_kernel
        scratch = [pltpu.VMEM((sb, l, wt), F32),
                   pltpu.VMEM((sb, h, wt), BF16), pltpu.VMEM((sb, h, wt), BF16),
                   pltpu.VMEM((sb, l, wt), BF16), pltpu.VMEM((sb, l, wt), BF16),
                   pltpu.VMEM((sb, h, wt), BF16)]
    else:
        mat_specs = [_resident((l, l), lambda j, b: (0, 0))] * 2
        mat_args = list(mats)
        body = _hy_direct_kernel
        scratch = [pltpu.VMEM((sb, l, wt), F32), pltpu.VMEM((sb, l, wt), BF16),
                   pltpu.VMEM((sb, l, wt), BF16), pltpu.VMEM((sb, l, wt), BF16)]
    return pl.pallas_call(
        functools.partial(body, l=l, rc=rc, sb=sb),
        grid=(nw, nb // sb),
        in_specs=[
            *mat_specs,
            seq(hv0), seq(hx10), seq(hx20),
            pl.BlockSpec((3, 3, wt), lambda j, b: (0, 0, j)),
            pl.BlockSpec((3, wt), lambda j, b: (0, j)),
            tab(0), tab(0), tab(1), tab(1),
            pl.BlockSpec((HY_ORDER, wt), lambda j, b: (0, j)),
            pl.BlockSpec((HY_ORDER, wt), lambda j, b: (0, j)),
        ],
        out_specs=pl.BlockSpec((sb * l, wt), lambda j, b: (b, j)),
        out_shape=jax.ShapeDtypeStruct((nb * l, HY_W), BF16),
        scratch_shapes=scratch,
        compiler_params=_params(("parallel", "parallel")),
        name=f"hyena_{l}",
    )(*mat_args, proj, proj, proj, conv_w.reshape(3, 3, HY_W),
      conv_b.reshape(3, HY_W), kc, ki, kc, ki, kn.reshape(HY_ORDER, HY_W), skip)


def _log_sigmoid(x):
    return jnp.minimum(x, 0.0) - jnp.log1p(jnp.exp(-jnp.abs(x)))


def _ret_kernel(*refs, l, hb, use_rope, has_s0, has_acc, st_layers, st_own):
    it = iter(refs)
    dl_ref, q_ref, k_ref, v_ref, g_ref = (next(it) for _ in range(5))
    cos_ref, sin_ref = (next(it), next(it)) if use_rope else (None, None)
    s0_ref = next(it) if has_s0 else None
    if has_acc:
        next(it)
    o_ref, st_ref, kr_scr, kv_scr = (next(it) for _ in range(4))
    for other in range(st_layers):
        if other != st_own:
            st_ref[0, other] = jnp.zeros(st_ref.shape[2:], F32)

    c, dk, dv = RET_CHUNK, RET_DK, RET_DV
    nc = l // c
    unroll = min(nc, 4)
    pos = lax.broadcasted_iota(jnp.int32, (c, 1), 0).astype(F32)
    diff = pos - lax.broadcasted_iota(jnp.int32, (1, c), 1).astype(F32)

    def rope(x, r0):
        if not use_rope:
            return x
        return x * cos_ref[pl.ds(r0, c), :] + pltpu.roll(x, dk // 2, 1) * sin_ref[pl.ds(r0, c), :]

    for h in range(hb):
        qs = slice(h * dk, (h + 1) * dk)
        vs = slice(h * dv, (h + 1) * dv)
        lgf = _log_sigmoid(dl_ref[0, h])
        lgb = _log_sigmoid(dl_ref[1, h])
        dm = jnp.where(diff >= 0, jnp.exp(lgf * diff), jnp.exp(-lgb * diff))
        zeta_f = jnp.exp(lgf * (c - 1.0 - pos))
        zeta_b = jnp.exp(lgb * pos)
        xi_f = jnp.exp(lgf * (pos + 1.0))
        xi_b = jnp.exp(lgb * (c - pos))
        gc_f = jnp.exp(lgf * c)
        gc_b = jnp.exp(lgb * c)

        def kv_step(ci, carry):
            r0 = pl.multiple_of(ci * c, c)
            k = rope(k_ref[pl.ds(r0, c), qs].astype(F32), r0) * (dk ** -0.5)
            kr_scr[pl.ds(r0, c), qs] = k.astype(BF16)
            kk = jnp.concatenate([k * zeta_f, k * zeta_b], axis=1).astype(BF16)
            kv_scr[h, ci] = lax.dot_general(kk, v_ref[pl.ds(r0, c), vs], (((0,), (0,)), ((), ())),
                                            preferred_element_type=F32)
            return carry

        lax.fori_loop(0, nc, kv_step, 0, unroll=unroll)

        def fwd_scan(ci, s):
            upd = kv_scr[h, ci, 0:dk, :]
            kv_scr[h, ci, 0:dk, :] = s
            return gc_f * s + upd

        def bwd_scan(i, r):
            ci = nc - 1 - i
            upd = kv_scr[h, ci, dk:2 * dk, :]
            kv_scr[h, ci, dk:2 * dk, :] = r
            return gc_b * r + upd

        zero = jnp.zeros((dk, dv), F32)
        st_ref[0, st_own, 0, h] = lax.fori_loop(0, nc, fwd_scan, s0_ref[0, 0, 0, h] if has_s0 else zero)
        st_ref[0, st_own, 1, h] = lax.fori_loop(0, nc, bwd_scan, s0_ref[0, 0, 1, h] if has_s0 else zero)

        def out_step(ci, carry):
            r0 = pl.multiple_of(ci * c, c)
            q = rope(q_ref[pl.ds(r0, c), qs].astype(F32), r0)
            sc = lax.dot_general(q.astype(BF16), kr_scr[pl.ds(r0, c), qs], (((1,), (1,)), ((), ())),
                                 preferred_element_type=F32) * dm
            o = _dot(sc.astype(BF16), v_ref[pl.ds(r0, c), vs])
            if has_s0 or nc > 1:
                qq = jnp.concatenate([q * xi_f, q * xi_b], axis=1).astype(BF16)
                o = o + _dot(qq, kv_scr[h, ci].astype(BF16))
            o = o * lax.rsqrt(jnp.mean(o * o, axis=-1, keepdims=True) + EPS)
            o_ref[pl.ds(r0, c), vs] = (_silu(g_ref[pl.ds(r0, c), vs].astype(F32)) * o).astype(BF16)
            return carry

        lax.fori_loop(0, nc, out_step, 0, unroll=unroll)


def _retention_branch(proj, row_blk0, nb, l, decay_logit, rope_tabs, state, layer, n_state_layers=1,
                      states_acc=None):
    nh, dk, dv = N_RET_HEADS, RET_DK, RET_DV
    use_rope = rope_tabs is not None
    has_s0 = state is not None
    has_acc = states_acc is not None
    own_layer = layer if n_state_layers > 1 else 0
    st_layers = 1 if has_acc else n_state_layers
    hb = nh if l <= 512 else 1
    ng = nh // hb
    seq = lambda col, w: pl.BlockSpec((l, hb * w), lambda b, g: (row_blk0 + b, col // (hb * w) + g))
    in_specs = [
        pl.BlockSpec((2, hb, 1, 1), lambda b, g: (0, g, 0, 0)),
        seq(COL_Q, dk), seq(COL_K, dk), seq(COL_V, dv), seq(COL_G, dv),
    ]
    args = [decay_logit.reshape(2, nh, 1, 1), proj, proj, proj, proj]
    if use_rope:
        in_specs += [pl.BlockSpec((l, dk), lambda b, g: (0, 0))] * 2
        args += list(rope_tabs)
    if has_s0:
        in_specs.append(pl.BlockSpec((1, 1, 2, hb, dk, dv), lambda b, g: (b, layer, 0, g, 0, 0)))
        args.append(state)
    if has_acc:
        in_specs.append(pl.BlockSpec(memory_space=pl.ANY))
        args.append(states_acc)
    return pl.pallas_call(
        functools.partial(_ret_kernel, l=l, hb=hb, use_rope=use_rope, has_s0=has_s0, has_acc=has_acc,
                          st_layers=st_layers, st_own=0 if has_acc else own_layer),
        grid=(nb, ng),
        in_specs=in_specs,
        out_specs=[
            pl.BlockSpec((l, hb * dv), lambda b, g: (b, g)),
            pl.BlockSpec((1, st_layers, 2, hb, dk, dv),
                         lambda b, g: (b, own_layer if has_acc else 0, 0, g, 0, 0)),
        ],
        out_shape=[
            jax.ShapeDtypeStruct((nb * l, nh * dv), BF16),
            jax.ShapeDtypeStruct((nb, n_state_layers, 2, nh, dk, dv), F32),
        ],
        input_output_aliases={len(args) - 1: 1} if has_acc else {},
        scratch_shapes=[
            pltpu.VMEM((l, hb * dk), BF16),
            pltpu.VMEM((hb, l // RET_CHUNK, 2 * dk, dv), F32),
        ],
        compiler_params=_params(("parallel", "parallel")),
        name=f"retention_{l}",
    )(*args)


ROW_ALIGN = 16


def _sorted_rows(tm):
    worst = TOP_K * tm + N_EXPERTS * (ROW_ALIGN - 1) + ROW_ALIGN
    return -(-worst // LANES) * LANES


def _mix_kernel(zap_ref, zas_ref, zbp_ref, zbs_ref, gap_ref, gas_ref, gbp_ref, gbs_ref, xp_ref, xs_ref,
                mod_ref, wa_ref, wb_ref, wo_ref, nw_ref, rw_ref, rb_ref, x1_ref, xl_ref, route_ref, seg_ref,
                h_scr, pr_scr, sg_scr, *, d, tm, rmax, n_tiles, n_prompt_tiles):
    i = pl.program_id(0)

    @pl.when(i == 0)
    def _():
        h_scr[...] = jnp.zeros_like(h_scr)
        pr_scr[...] = jnp.zeros_like(pr_scr)
        sg_scr[...] = jnp.zeros_like(sg_scr)

    @pl.when(i > n_tiles)
    def _():
        xl_ref[...] = jnp.zeros_like(xl_ref)

    @pl.when(i <= n_tiles)
    def _():
        _mix_step(zap_ref, zas_ref, zbp_ref, zbs_ref, gap_ref, gas_ref, gbp_ref, gbs_ref, xp_ref, xs_ref,
                  mod_ref, wa_ref, wb_ref, wo_ref, nw_ref, rw_ref, rb_ref, x1_ref, xl_ref, route_ref, seg_ref,
                  h_scr, pr_scr, sg_scr, d=d, tm=tm, rmax=rmax, n_tiles=n_tiles, n_prompt_tiles=n_prompt_tiles)


def _mix_step(zap_ref, zas_ref, zbp_ref, zbs_ref, gap_ref, gas_ref, gbp_ref, gbs_ref, xp_ref, xs_ref,
              mod_ref, wa_ref, wb_ref, wo_ref, nw_ref, rw_ref, rb_ref, x1_ref, xl_ref, route_ref, seg_ref,
              h_scr, pr_scr, sg_scr, *, d, tm, rmax, n_tiles, n_prompt_tiles):
    i = pl.program_id(0)
    r_col = lax.broadcasted_iota(jnp.int32, (rmax, 1), 0).astype(F32)
    in_seg = jnp.logical_and(r_col >= sg_scr[0:1], r_col < sg_scr[0:1] + sg_scr[1:2])
    seg_hot = jnp.where(in_seg, 1.0, 0.0).astype(BF16)
    row_of = _dot(jnp.concatenate([seg_hot, seg_hot], axis=1), pr_scr[...])
    sel = jnp.where(row_of == r_col + 1.0, 1.0, 0.0).astype(BF16)
    xl_ref[...] = _dot(sel, h_scr[...]).astype(BF16)

    is_prompt = jnp.minimum(i, n_tiles - 1) < n_prompt_tiles
    pick = lambda p_ref, s_ref: jnp.where(is_prompt, p_ref[...], s_ref[...])
    za, zb = pick(zap_ref, zas_ref), pick(zbp_ref, zbs_ref)
    ga, gb = pick(gap_ref, gas_ref), pick(gbp_ref, gbs_ref)
    merged = (_sigmoid(ga.astype(F32)) * _dot(za, wa_ref[...])
              + _sigmoid(gb.astype(F32)) * _dot(zb, wb_ref[...]))
    gate1 = mod_ref[0, :, 2 * d:3 * d]
    shift2 = mod_ref[0, :, 3 * d:4 * d]
    scale2 = mod_ref[0, :, 4 * d:5 * d]
    x1 = pick(xp_ref, xs_ref) + gate1 * _dot(merged.astype(BF16), wo_ref[...])
    x1_ref[...] = x1
    hn = x1 * lax.rsqrt(jnp.mean(x1 * x1, axis=-1, keepdims=True) + EPS) * nw_ref[...]
    hmo = hn * (1.0 + scale2) + shift2

    h_hi, h_lo = _split_bf16(hmo)
    both = _dot(h_hi, rw_ref[...])
    logits = (both[:, :LANES] + both[:, LANES:]) + _dot(h_lo, rw_ref[:, :LANES]) + rb_ref[...]
    lane = lax.broadcasted_iota(jnp.int32, (1, LANES), 1)
    idxs, vals = [], []
    for _ in range(TOP_K):
        m = jnp.max(logits, axis=1, keepdims=True)
        ik = jnp.min(jnp.where(logits == m, lane, LANES), axis=1, keepdims=True)
        idxs.append(ik)
        vals.append(m)
        logits = jnp.where(lane == ik, -jnp.inf, logits)
    exps = [jnp.exp(v - vals[0]) for v in vals]
    denom = exps[0] + exps[1] + exps[2] + exps[3]

    hot = jnp.zeros((tm, LANES), F32)
    for ik in idxs:
        hot = hot + jnp.where(lane == ik, 1.0, 0.0)
    r_i = lax.broadcasted_iota(jnp.int32, (tm, tm), 0)
    c_i = lax.broadcasted_iota(jnp.int32, (tm, tm), 1)
    below = jnp.where(c_i < r_i, 1.0, 0.0).astype(BF16)
    before = _dot(below, hot.astype(BF16))
    seg = jnp.ceil(jnp.sum(hot, axis=0, keepdims=True) * (1.0 / ROW_ALIGN)) * ROW_ALIGN
    seg_ref[0] = seg
    e_r = lax.broadcasted_iota(jnp.int32, (LANES, LANES), 0)
    e_c = lax.broadcasted_iota(jnp.int32, (LANES, LANES), 1)
    seg_start = _dot3(jnp.broadcast_to(seg, (8, LANES)), jnp.where(e_r < e_c, 1.0, 0.0))[0:1]
    base = seg_start + before

    route = jnp.zeros((tm, LANES), F32)
    for k in range(TOP_K):
        srow = jnp.sum(jnp.where(lane == idxs[k], base, 0.0), axis=1, keepdims=True)
        route = route + jnp.where(lane == k, srow, 0.0)
        route = route + jnp.where(lane == TOP_K + k, idxs[k].astype(F32), 0.0)
        route = route + jnp.where(lane == 2 * TOP_K + k, exps[k] / denom, 0.0)
    route_ref[...] = route

    h_scr[...] = h_hi
    pair_row = jnp.where(hot > 0.0, base + 1.0, 0.0).T
    pr_scr[...] = jnp.concatenate(_split_bf16(pair_row), axis=0)
    sg_scr[0:1] = seg_start
    sg_scr[1:2] = seg


def _mix(za, zb, proj, x, mod3, mod_row, wa, wb, wo, norm_w, rw, rb, tm, spare_rows):
    d = x[0].shape[1]
    n_p = x[0].shape[0] // tm
    n_s = x[1].shape[0] // tm
    nt = n_p + n_s
    t = nt * tm
    rmax = _sorted_rows(tm)
    spare_blocks = -(-spare_rows // rmax)
    vw = N_RET_HEADS * RET_DV
    cur = lambda i: jnp.minimum(i, nt - 1)
    prev = lambda i: jnp.maximum(i - 1, 0)
    pair = lambda w, col: [pl.BlockSpec((tm, w), lambda i: (jnp.minimum(cur(i), n_p - 1), col)),
                           pl.BlockSpec((tm, w), lambda i: (jnp.maximum(cur(i) - n_p, 0), col))]
    const = lambda i: (0, 0)
    return pl.pallas_call(
        functools.partial(_mix_kernel, d=d, tm=tm, rmax=rmax, n_tiles=nt, n_prompt_tiles=n_p),
        grid=(nt + 1 + spare_blocks,),
        in_specs=[
            *pair(HY_W, 0), *pair(vw, 0), *pair(d, COL_GA // d), *pair(d, COL_GB // d), *pair(d, 0),
            pl.BlockSpec((1, 1, 6 * d), lambda i: (mod_row(cur(i)), 0, 0)),
            _resident((HY_W, d), const), _resident((vw, d), const), _resident((d, d), const),
            pl.BlockSpec((1, d), const),
            _resident((d, 2 * LANES), const), pl.BlockSpec((1, LANES), const),
        ],
        out_specs=[
            pl.BlockSpec((tm, d), lambda i: (cur(i), 0)),
            pl.BlockSpec((rmax, d), lambda i: (prev(i), 0)),
            pl.BlockSpec((tm, LANES), lambda i: (cur(i), 0)),
            pl.BlockSpec((1, 1, LANES), lambda i: (cur(i), 0, 0)),
        ],
        out_shape=[
            jax.ShapeDtypeStruct((t, d), F32),
            jax.ShapeDtypeStruct(((nt + spare_blocks) * rmax, d), BF16),
            jax.ShapeDtypeStruct((t, LANES), F32),
            jax.ShapeDtypeStruct((nt, 1, LANES), F32),
        ],
        scratch_shapes=[pltpu.VMEM((tm, d), BF16), pltpu.VMEM((2 * LANES, tm), BF16),
                        pltpu.VMEM((2, LANES), F32)],
        compiler_params=_params(("arbitrary",)),
        name="merge_router",
    )(*za, *zb, *proj, *proj, *x, mod3, wa, wb, wo, norm_w.reshape(1, d), rw, rb)


def _ffn_kernel(te_ref, nu_ref, src_ref, dst_ref, xl_ref, wgu_ref, bgu_ref, wd_ref, bd_ref, yl_ref,
                xbuf, ybuf, gsem, ssem, wgu_scr, wd_scr, *, dff, te):
    j = pl.program_id(0)
    nu = nu_ref[0]
    cpt = te // ROW_ALIGN
    slot = j % 2

    def hbm_chunk(ci):
        return pl.ds(pl.multiple_of(ci * ROW_ALIGN, ROW_ALIGN), ROW_ALIGN)

    def start_gather(tile, buf):
        for ci in range(cpt):
            pltpu.make_async_copy(xl_ref.at[hbm_chunk(src_ref[tile * cpt + ci])],
                                  xbuf.at[buf, pl.ds(ci * ROW_ALIGN, ROW_ALIGN)], gsem.at[buf]).start()

    def wait_gather(buf):
        pltpu.make_async_copy(xl_ref.at[pl.ds(0, te)], xbuf.at[buf], gsem.at[buf]).wait()

    def start_scatter(tile, buf):
        for ci in range(cpt):
            pltpu.make_async_copy(ybuf.at[buf, pl.ds(ci * ROW_ALIGN, ROW_ALIGN)],
                                  yl_ref.at[hbm_chunk(dst_ref[tile * cpt + ci])], ssem.at[buf]).start()

    def wait_scatter(buf):
        pltpu.make_async_copy(ybuf.at[buf], yl_ref.at[pl.ds(0, te)], ssem.at[buf]).wait()

    @pl.when(j < nu)
    def _():
        @pl.when(j == 0)
        def _():
            start_gather(0, 0)

        start_gather(j + 1, 1 - slot)
        wait_gather(slot)

        @pl.when(j >= 2)
        def _():
            wait_scatter(slot)

        @pl.when(jnp.logical_or(j == 0, te_ref[j] != te_ref[jnp.maximum(j - 1, 0)]))
        def _():
            wgu_scr[...] = wgu_ref[0].astype(BF16)
            wd_scr[...] = wd_ref[0].astype(BF16)

        hgu = _dot(xbuf[slot], wgu_scr[...]) + bgu_ref[0]
        gate = jnp.minimum(hgu[:, :dff], SWIGLU_LIMIT)
        up = jnp.clip(hgu[:, dff:], -SWIGLU_LIMIT, SWIGLU_LIMIT)
        act = gate * _sigmoid(SWIGLU_ALPHA * gate) * (up + 1.0)
        ybuf[slot] = (_dot(act.astype(BF16), wd_scr[...]) + bd_ref[0]).astype(BF16)
        start_scatter(j, slot)

        @pl.when(j == nu - 1)
        def _():
            wait_gather(1 - slot)

            @pl.when(j >= 1)
            def _():
                wait_scatter(1 - slot)
            wait_scatter(slot)


def _expert_ffn(xl, tile_expert, n_used, src_chunk, dst_chunk, w_gate_up, b_gate_up, w_down, b_down, layer, te):
    _, d = xl.shape
    depth, n_exp, _, two_ff = w_gate_up.shape
    ne = depth * n_exp
    dff = two_ff // 2
    wsel = lambda j, tex, nu, src, dst: (layer * n_exp + tex[j], 0, 0)
    return pl.pallas_call(
        functools.partial(_ffn_kernel, dff=dff, te=te),
        grid_spec=pltpu.PrefetchScalarGridSpec(
            num_scalar_prefetch=4,
            grid=(tile_expert.shape[0],),
            in_specs=[
                pl.BlockSpec(memory_space=pl.ANY),
                pl.BlockSpec((1, d, two_ff), wsel),
                pl.BlockSpec((1, 1, two_ff), wsel),
                pl.BlockSpec((1, dff, d), wsel),
                pl.BlockSpec((1, 1, d), wsel),
            ],
            out_specs=pl.BlockSpec(memory_space=pl.ANY),
            scratch_shapes=[
                pltpu.VMEM((2, te, d), BF16), pltpu.VMEM((2, te, d), BF16),
                pltpu.SemaphoreType.DMA((2,)), pltpu.SemaphoreType.DMA((2,)),
                pltpu.VMEM((d, two_ff), BF16), pltpu.VMEM((dff, d), BF16),
            ],
        ),
        out_shape=jax.ShapeDtypeStruct(xl.shape, BF16),
        input_output_aliases={4: 0},
        compiler_params=_params(("arbitrary",)),
        name="moe_ffn",
    )(tile_expert, n_used, src_chunk, dst_chunk, xl, w_gate_up.reshape(ne, d, two_ff),
      b_gate_up.reshape(ne, 1, two_ff), w_down.reshape(ne, dff, d), b_down.reshape(ne, 1, d))


def _comb_kernel(yl_ref, route_ref, x1_ref, mod_ref, fw_ref, o_ref, *, d, tm, rmax, final_norm):
    route = route_ref[...]
    r_iota = lax.broadcasted_iota(jnp.int32, (1, rmax), 1).astype(F32)
    wm = jnp.zeros((tm, rmax), F32)
    for k in range(TOP_K):
        wm = jnp.where(r_iota == route[:, k:k + 1], route[:, 2 * TOP_K + k:2 * TOP_K + k + 1], wm)
    out = _dot(wm.astype(BF16), yl_ref[...])
    x2 = x1_ref[...] + mod_ref[0, :, 5 * d:6 * d] * out
    if final_norm:
        x2 = x2 * lax.rsqrt(jnp.mean(x2 * x2, axis=-1, keepdims=True) + EPS) * fw_ref[...]
    o_ref[...] = x2


def _combine(yl, route, x1, mod3, mod_row, final_w, tile0, n_tiles, tm, final_norm):
    d = x1.shape[1]
    rmax = _sorted_rows(tm)
    return pl.pallas_call(
        functools.partial(_comb_kernel, d=d, tm=tm, rmax=rmax, final_norm=final_norm),
        grid=(n_tiles,),
        in_specs=[
            pl.BlockSpec((rmax, d), lambda i: (tile0 + i, 0)),
            pl.BlockSpec((tm, LANES), lambda i: (tile0 + i, 0)),
            pl.BlockSpec((tm, d), lambda i: (tile0 + i, 0)),
            pl.BlockSpec((1, 1, 6 * d), lambda i: (mod_row(tile0 + i), 0, 0)),
            pl.BlockSpec((1, d), lambda i: (0, 0)),
        ],
        out_specs=pl.BlockSpec((tm, d), lambda i: (i, 0)),
        out_shape=jax.ShapeDtypeStruct((n_tiles * tm, d), F32),
        compiler_params=_params(("parallel",)),
        name="moe_combine",
    )(yl, route, x1, mod3, final_w.reshape(1, d))


@functools.lru_cache(maxsize=None)
def _dft_mats(l):
    s = np.arange(l, dtype=np.int64)
    f = np.concatenate([np.arange(0, l, 2), np.arange(1, l, 2)]).astype(np.int64) if _fold_dft(l) else s
    ang = ((f[:, None] * s[None, :]) % (2 * l)).astype(np.float64) * (math.pi / l)
    mats = (np.cos(ang).astype(np.float32), np.sin(ang).astype(np.float32))
    if _fold_dft(l):
        a = np.arange(l // 2)
        mats += (((a[:, None] + a[None, :] == l // 2) & (a[:, None] >= 1)).astype(np.float32),)
    return mats


@functools.lru_cache(maxsize=None)
def _filter_features(l):
    f32 = np.float32
    t = np.linspace(0.0, 1.0, l, dtype=f32)[:, None]
    bands = (HY_EMB - 1) // 2
    w = f32(2.0 * math.pi) * np.arange(l, dtype=f32)[:, None] / f32(l)
    f = np.linspace(1e-4, bands - 1, bands, dtype=f32)[None, :]
    z = np.concatenate([t, np.cos(f * w), -np.sin(f * w)], axis=-1).astype(f32)
    return np.pad(z, ((0, 0), (0, LANES - HY_EMB))), t


@functools.lru_cache(maxsize=None)
def _rope_tables(rows):
    f32 = np.float32
    row = np.repeat(np.arange(rows, dtype=f32), GRID_W)
    col = np.tile(np.arange(GRID_W, dtype=f32), rows)
    half = RET_DK // 4
    inv = (f32(ROPE_BASE) ** (-np.arange(half, dtype=f32) / f32(half))).astype(f32)
    ang = np.concatenate([row[:, None] * inv, col[:, None] * inv], axis=-1)
    ang = np.concatenate([ang, ang], axis=-1).astype(f32)
    sign = np.where(np.arange(RET_DK) < RET_DK // 2, -1.0, 1.0).astype(f32)
    return np.cos(ang), np.sin(ang) * sign


@functools.lru_cache(maxsize=None)
def _decay_rates():
    return np.linspace(math.log(HY_TARGET) / HY_SLOW_PCT, math.log(HY_TARGET) / HY_FAST_PCT,
                       HY_W, dtype=np.float32)[None, :]


def _pad2(a, rows, cols):
    return jnp.pad(a, ((0, rows - a.shape[0]), (0, cols - a.shape[1])))


def _ffn_plan(seg, rmax, te, n_ffn_tiles):
    nt, ne = seg.shape
    cum_t = jnp.cumsum(seg, axis=0)
    tot = cum_t[-1]
    padded = ((tot + te - 1) // te) * te
    ends = jnp.cumsum(padded)
    offs = ends - padded
    n_used = ends[-1] // te
    expert_of = lambda rows: jnp.minimum(jnp.sum(ends[None, :] <= rows[:, None], axis=1), ne - 1)
    tiles = jnp.arange(n_ffn_tiles, dtype=jnp.int32)
    tile_expert = expert_of(jnp.minimum(tiles, n_used - 1) * te).astype(jnp.int32)

    rows = jnp.arange(n_ffn_tiles * te // ROW_ALIGN, dtype=jnp.int32) * ROW_ALIGN
    hot_e = expert_of(rows)[:, None] == jnp.arange(ne)[None, :]
    pick_e = lambda tab: jnp.sum(jnp.where(hot_e[:, None, :], tab[None], 0), axis=2)
    r_s = rows - pick_e(offs[None, :])[:, 0]
    cum_sel = pick_e(cum_t)
    i_s = jnp.minimum(jnp.sum(cum_sel <= r_s[:, None], axis=1), nt - 1)
    hot_i = i_s[:, None] == jnp.arange(nt)[None, :]
    seg_base = (jnp.arange(nt)[:, None] * rmax + jnp.cumsum(seg, axis=1) - seg) - (cum_t - seg)
    src_row = r_s + jnp.sum(jnp.where(hot_i, pick_e(seg_base), 0), axis=1)
    valid = (r_s < pick_e(tot[None, :])[:, 0]) & (rows < ends[-1])
    src_chunk = jnp.where(valid, src_row // ROW_ALIGN, rmax // ROW_ALIGN - 1).astype(jnp.int32)
    pad_rank = jnp.cumsum(jnp.where(valid, 0, 1)) - 1
    spare = nt * rmax // ROW_ALIGN + jnp.minimum(pad_rank, _spare_chunks(te) - 1)
    dst_chunk = jnp.where(valid, src_row // ROW_ALIGN, spare).astype(jnp.int32)
    return tile_expert, n_used.reshape(1).astype(jnp.int32), src_chunk, dst_chunk


def _spare_chunks(te):
    return N_EXPERTS * (te // ROW_ALIGN - 1)


def kernel(x_prompt, x_sample, state_ret, c, c_ctx, ada_w, ada_b, norm_mix_w, w_in, conv_w, conv_b,
           f_w1, f_b1, f_w2, f_b2, f_w3, f_b3, f_freq, f_wout, hy_skip, ret_decay_logit,
           w_branch_a, w_branch_b, w_out, norm_ffn_w, router_w, router_b, w_gate_up, b_gate_up,
           w_down, b_down, final_norm_w):
    bp, lp, d = x_prompt.shape
    bs, ls, _ = x_sample.shape
    depth = ada_w.shape[0]
    tp, ts = bp * lp, bs * ls
    t = tp + ts
    assert ls % lp == 0 and tp % ls == 0 and lp % RET_CHUNK == 0 and ls % GRID_W == 0
    tm = 512 if (tp % 512 == 0 and ls % 512 == 0) else lp
    tm_in = tm
    te = 512
    nt = t // tm
    rmax = _sorted_rows(tm)
    n_ffn_tiles = -(-(t * TOP_K + nt * N_EXPERTS * (ROW_ALIGN - 1)) // te) + N_EXPERTS

    dft = {l: tuple(jnp.asarray(a).astype(BF16) for a in _dft_mats(l)) for l in (lp, ls)}
    feats = {l: tuple(jnp.asarray(a) for a in _filter_features(l)) for l in (lp, ls)}
    rope_tabs = tuple(jnp.asarray(a) for a in _rope_tables(ls // GRID_W))
    deltas = jnp.asarray(_decay_rates())

    cvec = jnp.zeros((8, d), F32).at[:bs].set(c).at[bs].set(c_ctx)
    mod3 = _modulation(cvec, ada_w, ada_b).reshape(depth * 8, 1, 6 * d)
    n_p, n_s = tp // tm, ts // tm

    x = (x_prompt.reshape(tp, d), x_sample.reshape(ts, d))
    states = None
    for l in range(depth):
        mod_row = lambda i, l=l: l * 8 + jnp.where(i < n_p, bs, jnp.maximum(i - n_p, 0) // (ls // tm))
        w_in_bf = w_in[l].astype(BF16)
        proj = (_in_proj(x[0], norm_mix_w[l], mod3, lambda i, l=l: l * 8 + bs, w_in_bf, tm_in),
                _in_proj(x[1], norm_mix_w[l], mod3, lambda i, l=l: l * 8 + i // (ls // tm_in), w_in_bf, tm_in))

        w1 = _pad2(f_w1[l], LANES, LANES)
        w2 = _pad2(f_w2[l], LANES, LANES)
        w3 = _pad2(f_w3[l], LANES, LANES)
        wo = _pad2(f_wout[l], LANES, f_wout.shape[2])
        b1, b2, b3, fr = (_pad2(a[l][None, :], 1, LANES) for a in (f_b1, f_b2, f_b3, f_freq))
        za, zb = [], []
        for grp, (nb, ln) in enumerate(((bp, lp), (bs, ls))):
            taps, sums = _hyena_taps(ln, feats[ln][0], feats[ln][1], deltas, w1, b1, w2, b2, w3, b3, fr, wo)
            kc, ki, kn = _filter_spectra(ln, dft[ln][0], dft[ln][1], taps, sums)
            za.append(_hyena_branch(proj[grp], 0, nb, ln, dft[ln], conv_w[l], conv_b[l], kc, ki, kn, hy_skip[l]))
            if grp == 0:
                zb_g, states = _retention_branch(proj[grp], 0, nb, ln, ret_decay_logit[l], None, None, l,
                                                 depth, states)
            else:
                zb_g, _ = _retention_branch(proj[grp], 0, nb, ln, ret_decay_logit[l], rope_tabs, state_ret, l)
            zb.append(zb_g)

        rw = jnp.concatenate(_split_bf16(_pad2(router_w[l], d, LANES)), axis=1)
        rb = jnp.full((1, LANES), -jnp.inf, F32).at[0, :N_EXPERTS].set(router_b[l])
        x1, xl, route, seg = _mix(za, zb, proj, x, mod3, mod_row,
                                  w_branch_a[l].astype(BF16), w_branch_b[l].astype(BF16),
                                  w_out[l].astype(BF16), norm_ffn_w[l], rw, rb, tm, _spare_chunks(te) * ROW_ALIGN)
        tile_expert, n_used, src_chunk, dst_chunk = _ffn_plan(seg[:, 0, :N_EXPERTS].astype(jnp.int32), rmax, te,
                                                              n_ffn_tiles)
        yl = _expert_ffn(xl, tile_expert, n_used, src_chunk, dst_chunk, w_gate_up, b_gate_up, w_down, b_down,
                         l, te)
        last = l == depth - 1
        x = (_combine(yl, route, x1, mod3, mod_row, final_norm_w, 0, n_p, tm, last),
             _combine(yl, route, x1, mod3, mod_row, final_norm_w, n_p, n_s, tm, last))

    return (x[0].reshape(bp, lp, d), x[1].reshape(bs, ls, d), states)
```

```python
import functools
import math

import jax
import jax.numpy as jnp
import numpy as np
from jax import lax
from jax.experimental import pallas as pl
from jax.experimental.pallas import tpu as pltpu

F32 = jnp.float32
BF16 = jnp.bfloat16

GRID_W = 64
HY_W = 512
HY_ORDER = 2
HY_EMB = 33
HY_FAST_PCT = 0.3
HY_SLOW_PCT = 1.5
HY_TARGET = 1e-2
N_RET_HEADS = 4
RET_DK = 128
RET_DV = 256
RET_CHUNK = 256
ROPE_BASE = 10000.0
N_EXPERTS = 32
TOP_K = 4
SWIGLU_LIMIT = 7.0
SWIGLU_ALPHA = 1.702
EPS = 1e-6

LANES = 128
VMEM_LIMIT = 56 * 1024 * 1024

COL_V, COL_G, COL_GA, COL_GB = 0, 1024, 2048, 3072
COL_HV, COL_HX1, COL_HX2, COL_Q, COL_K = 4096, 4608, 5120, 5632, 6144
IN_W = 6656
IN_COL_BLOCKS = (5, 6, 7, 8, 9, 10, 11, 12, 0, 1, 2, 3, 4)


def _dot(a, b):
    return jnp.dot(a, b, preferred_element_type=F32)


def _split_bf16(a):
    hi = a.astype(BF16)
    lo = (a - hi.astype(F32)).astype(BF16)
    return hi, lo


def _dot3(a, b):
    ah, al = _split_bf16(a)
    bh, bl = _split_bf16(b)
    return _dot(ah, bh) + (_dot(ah, bl) + _dot(al, bh))


def _sigmoid(x):
    return 0.5 * jnp.tanh(0.5 * x) + 0.5


def _silu(x):
    return x * _sigmoid(x)


def _params(sem, vmem=VMEM_LIMIT):
    return pltpu.CompilerParams(dimension_semantics=sem, vmem_limit_bytes=vmem)


def _resident(shape, index_map):
    return pl.BlockSpec(shape, index_map, pipeline_mode=pl.Buffered(1))


def _mod_kernel(c_ref, w_ref, b_ref, o_ref):
    o_ref[0] = _dot3(_silu(c_ref[...]), w_ref[0]) + b_ref[0]


def _modulation(cvec, ada_w, ada_b):
    depth, d, six_d = ada_w.shape
    tn = six_d // 4
    return pl.pallas_call(
        _mod_kernel,
        grid=(depth, six_d // tn),
        in_specs=[
            pl.BlockSpec((8, d), lambda l, j: (0, 0)),
            pl.BlockSpec((1, d, tn), lambda l, j: (l, 0, j)),
            pl.BlockSpec((1, 1, tn), lambda l, j: (l, 0, j)),
        ],
        out_specs=pl.BlockSpec((1, 8, tn), lambda l, j: (l, 0, j)),
        out_shape=jax.ShapeDtypeStruct((depth, 8, six_d), F32),
        compiler_params=_params(("parallel", "parallel")),
        name="adaln_mod",
    )(cvec, ada_w, ada_b.reshape(depth, 1, six_d))


def _in_kernel(x_ref, nw_ref, mod_ref, w_ref, o_ref, *, d, cw):
    x = x_ref[...]
    y = x * lax.rsqrt(jnp.mean(x * x, axis=-1, keepdims=True) + EPS) * nw_ref[...]
    shift = mod_ref[0, :, 0:d]
    scale = mod_ref[0, :, d:2 * d]
    u = (y * (1.0 + scale) + shift).astype(BF16)
    for j, src in enumerate(IN_COL_BLOCKS):
        o_ref[:, j * cw:(j + 1) * cw] = _dot(u, w_ref[:, src * cw:(src + 1) * cw]).astype(BF16)


def _in_proj(x, norm_w, mod3, mod_row, w_in_bf, tm):
    t, d = x.shape
    return pl.pallas_call(
        functools.partial(_in_kernel, d=d, cw=IN_W // len(IN_COL_BLOCKS)),
        grid=(t // tm,),
        in_specs=[
            pl.BlockSpec((tm, d), lambda i: (i, 0)),
            pl.BlockSpec((1, d), lambda i: (0, 0)),
            pl.BlockSpec((1, 1, 2 * d), lambda i: (mod_row(i), 0, 0)),
            _resident((d, IN_W), lambda i: (0, 0)),
        ],
        out_specs=pl.BlockSpec((tm, IN_W), lambda i: (i, 0)),
        out_shape=jax.ShapeDtypeStruct((t, IN_W), BF16),
        compiler_params=_params(("parallel",)),
        name="in_proj",
    )(x, norm_w.reshape(1, d), mod3, w_in_bf)


def _filt_kernel(z_ref, t_ref, w1, b1, w2, b2, w3, b3, fr, wo, dl_ref, h_ref, s_ref, *, tl):
    i = pl.program_id(0)
    freq = fr[...]
    h = jnp.sin(freq * (_dot3(z_ref[...], w1[...]) + b1[...]))
    h = jnp.sin(freq * (_dot3(h, w2[...]) + b2[...]))
    h = jnp.sin(freq * (_dot3(h, w3[...]) + b3[...]))
    h = _dot3(h, wo[...])
    decay = jnp.exp(-t_ref[...] * jnp.abs(dl_ref[...]))
    h = h * jnp.concatenate([decay] * (2 * HY_ORDER), axis=1)
    h_ref[...] = h
    row = i * tl + lax.broadcasted_iota(jnp.int32, (tl, 1), 0)
    col = lax.broadcasted_iota(jnp.int32, (1, 2 * HY_ORDER * HY_W), 1)
    is_bwd = ((col // HY_W) % 2) == 1
    part = jnp.sum(jnp.where((row == 0) & is_bwd, 0.0, jnp.abs(h)), axis=0, keepdims=True)

    @pl.when(i == 0)
    def _():
        s_ref[...] = jnp.zeros_like(s_ref)

    s_ref[...] += part


def _hyena_taps(l, zfeat, tcol, deltas, w1, b1, w2, b2, w3, b3, freq, wout):
    tl = min(l, 256)
    nw = 2 * HY_ORDER * HY_W
    full = lambda shape: pl.BlockSpec(shape, lambda i: (0,) * len(shape))
    return pl.pallas_call(
        functools.partial(_filt_kernel, tl=tl),
        grid=(l // tl,),
        in_specs=[
            pl.BlockSpec((tl, LANES), lambda i: (i, 0)),
            pl.BlockSpec((tl, 1), lambda i: (i, 0)),
            full((LANES, LANES)), full((1, LANES)),
            full((LANES, LANES)), full((1, LANES)),
            full((LANES, LANES)), full((1, LANES)),
            full((1, LANES)),
            full((LANES, nw)),
            full((1, HY_W)),
        ],
        out_specs=[pl.BlockSpec((tl, nw), lambda i: (i, 0)), pl.BlockSpec((1, nw), lambda i: (0, 0))],
        out_shape=[jax.ShapeDtypeStruct((l, nw), F32), jax.ShapeDtypeStruct((1, nw), F32)],
        compiler_params=_params(("arbitrary",)),
        name=f"hyena_taps_{l}",
    )(zfeat, tcol, w1, b1, w2, b2, w3, b3, freq, wout, deltas)


def _fdft_kernel(c_ref, s_ref, hf_ref, hb_ref, sf_ref, sb_ref, kc_ref, ki_ref, kn_ref, *, l):
    row = lax.broadcasted_iota(jnp.int32, (l, 1), 0)
    hf = hf_ref[...]
    hb = jnp.where(row == 0, 0.0, hb_ref[...])
    inv = 1.0 / (sf_ref[...] + sb_ref[...])
    a = hf + hb
    b = hb - hf
    wf = jnp.where(row == 0, 0.5 / l, 1.0 / l)
    kc_ref[...] = _dot(c_ref[...], a.astype(BF16)) * wf * inv
    ki_ref[...] = _dot(s_ref[...], b.astype(BF16)) * wf * inv
    sgn = jnp.where(row % 2 == 0, 1.0, -1.0)
    kn_ref[...] = jnp.sum(a * sgn, axis=0, keepdims=True) * inv * (0.5 / l)


def _filter_spectra(l, cmat, smat, taps, sums):
    wt = 256
    nb = HY_W // wt
    two_w = 2 * HY_W
    return pl.pallas_call(
        functools.partial(_fdft_kernel, l=l),
        grid=(HY_ORDER, nb),
        in_specs=[
            _resident((l, l), lambda o, j: (0, 0)),
            _resident((l, l), lambda o, j: (0, 0)),
            pl.BlockSpec((l, wt), lambda o, j: (0, o * 2 * nb + j)),
            pl.BlockSpec((l, wt), lambda o, j: (0, o * 2 * nb + nb + j)),
            pl.BlockSpec((1, wt), lambda o, j: (0, o * 2 * nb + j)),
            pl.BlockSpec((1, wt), lambda o, j: (0, o * 2 * nb + nb + j)),
        ],
        out_specs=[
            pl.BlockSpec((l, wt), lambda o, j: (0, o * nb + j)),
            pl.BlockSpec((l, wt), lambda o, j: (0, o * nb + j)),
            pl.BlockSpec((1, wt), lambda o, j: (0, o * nb + j)),
        ],
        out_shape=[
            jax.ShapeDtypeStruct((l, HY_ORDER * HY_W), F32),
            jax.ShapeDtypeStruct((l, HY_ORDER * HY_W), F32),
            jax.ShapeDtypeStruct((1, HY_ORDER * HY_W), F32),
        ],
        compiler_params=_params(("parallel", "parallel")),
        name=f"hyena_spectra_{l}",
    )(cmat, smat, taps, taps, sums, sums)


def _hy_direct_kernel(c_ref, s_ref, hv_ref, hx1_ref, hx2_ref, cw_ref, cb_ref,
                      kc0, ki0, kc1, ki1, kn_ref, sk_ref, o_ref, z_scr, zb_scr, p_scr, q_scr, *, l, rc, sb):
    row = lax.broadcasted_iota(jnp.int32, (l, 1), 0)
    first = row == 0
    last = row == l - 1
    sgn = jnp.where(row % 2 == 0, 1.0, -1.0)
    cw = cw_ref[...]
    cb = cb_ref[...]

    def short_conv(ref, rows, part):
        x = ref[rows, :].astype(F32)
        xm = jnp.where(first, 0.0, pltpu.roll(x, 1, 0))
        xp = jnp.where(last, 0.0, pltpu.roll(x, l - 1, 0))
        return (xm * cw[0, part:part + 1] + x * cw[1, part:part + 1] + xp * cw[2, part:part + 1]
                + cb[part:part + 1])

    for s in range(sb):
        rows = slice(s * l, (s + 1) * l)
        z_scr[s] = short_conv(hv_ref, rows, 0)
        for o, (gate_ref, kc, ki) in enumerate(((hx1_ref, kc0, ki0), (hx2_ref, kc1, ki1))):
            z = z_scr[s]
            zb_scr[s] = z.astype(BF16)
            nyq = jnp.sum(z * sgn, axis=0, keepdims=True) * kn_ref[o:o + 1]
            for r in range(l // rc):
                sl = slice(r * rc, (r + 1) * rc)
                zc = _dot(c_ref[sl, :], zb_scr[s])
                zs = _dot(s_ref[sl, :], zb_scr[s])
                p_scr[s, sl, :] = (zc * kc[sl, :] + zs * ki[sl, :]).astype(BF16)
                q_scr[s, sl, :] = (zs * kc[sl, :] - zc * ki[sl, :]).astype(BF16)
            gate = short_conv(gate_ref, rows, o + 1)
            for r in range(l // rc):
                sl = slice(r * rc, (r + 1) * rc)
                y = _dot(c_ref[sl, :], p_scr[s]) + _dot(s_ref[sl, :], q_scr[s])
                y = y + sgn[sl] * nyq + z_scr[s, sl, :] * sk_ref[o:o + 1]
                z_scr[s, sl, :] = gate[sl] * y
        o_ref[rows, :] = z_scr[s].astype(BF16)


def _hy_kernel(ce_ref, se_ref, co_ref, so_ref, cot_ref, sot_ref, j_ref, hv_ref, hx1_ref, hx2_ref, cw_ref, cb_ref,
               kc0, ki0, kc1, ki1, kn_ref, sk_ref, o_ref, z_scr, zp_scr, zm_scr, p_scr, q_scr, yr_scr,
               *, l, rc, sb):
    h = l // 2
    row = lax.broadcasted_iota(jnp.int32, (l, 1), 0)
    first = row == 0
    last = row == l - 1
    sgn = jnp.where(row % 2 == 0, 1.0, -1.0)
    half_row = lax.broadcasted_iota(jnp.int32, (h, 1), 0)
    sm = jnp.where(half_row % 2 == 0, 1.0, -1.0)
    cw = cw_ref[...]
    cb = cb_ref[...]

    def short_conv(ref, rows, part):
        x = ref[rows, :].astype(F32)
        xm = jnp.where(first, 0.0, pltpu.roll(x, 1, 0))
        xp = jnp.where(last, 0.0, pltpu.roll(x, l - 1, 0))
        return (xm * cw[0, part:part + 1] + x * cw[1, part:part + 1] + xp * cw[2, part:part + 1]
                + cb[part:part + 1])

    for s in range(sb):
        rows = slice(s * l, (s + 1) * l)
        z_scr[s] = short_conv(hv_ref, rows, 0)
        for o, (gate_ref, kc, ki) in enumerate(((hx1_ref, kc0, ki0), (hx2_ref, kc1, ki1))):
            z = z_scr[s]
            nyq = jnp.sum(z * sgn, axis=0, keepdims=True) * kn_ref[o:o + 1]
            zmid = z[h:h + 1]
            zrev = _dot(j_ref[...], z[h:].astype(BF16))
            zp_scr[s] = (z[:h] + zrev).astype(BF16)
            zm_scr[s] = (z[:h] - zrev).astype(BF16)
            ymid = jnp.zeros_like(zmid)
            for r in range(h // rc):
                sl = slice(r * rc, (r + 1) * rc)
                so_ = slice(h + r * rc, h + (r + 1) * rc)
                zc_e = _dot(ce_ref[sl, :], zp_scr[s]) + sm[sl] * zmid
                zs_e = _dot(se_ref[sl, :], zm_scr[s])
                zc_o = _dot(co_ref[sl, :], zm_scr[s])
                zs_o = _dot(so_ref[sl, :], zp_scr[s]) + sm[sl] * zmid
                pe = zc_e * kc[sl, :] + zs_e * ki[sl, :]
                qe = zs_e * kc[sl, :] - zc_e * ki[sl, :]
                po = zc_o * kc[so_, :] + zs_o * ki[so_, :]
                qo = zs_o * kc[so_, :] - zc_o * ki[so_, :]
                ymid = ymid + jnp.sum(sm[sl] * (pe + qo), axis=0, keepdims=True)
                p_scr[s, sl, :] = pe.astype(BF16)
                q_scr[s, sl, :] = qe.astype(BF16)
                p_scr[s, so_, :] = po.astype(BF16)
                q_scr[s, so_, :] = qo.astype(BF16)
            gate = short_conv(gate_ref, rows, o + 1)
            skip = sk_ref[o:o + 1]
            for r in range(h // rc):
                sl = slice(r * rc, (r + 1) * rc)
                a1 = _dot(ce_ref[sl, :], p_scr[s, :h, :])
                a2 = _dot(se_ref[sl, :], q_scr[s, :h, :])
                a3 = _dot(cot_ref[sl, :], p_scr[s, h:, :])
                a4 = _dot(sot_ref[sl, :], q_scr[s, h:, :])
                yr_scr[s, sl, :] = ((a1 + a4) - (a2 + a3)).astype(BF16)
                y = (a1 + a4) + (a2 + a3) + sgn[sl] * nyq + z[sl] * skip
                z_scr[s, sl, :] = gate[sl] * y
            y_hi = jnp.where(half_row == 0, ymid, _dot(j_ref[...], yr_scr[s]))
            y_hi = y_hi + sgn[h:] * nyq + z[h:] * skip
            z_scr[s, h:, :] = gate[h:] * y_hi
        o_ref[rows, :] = z_scr[s].astype(BF16)


def _fold_dft(l):
    return l >= 1024


def _hyena_branch(proj, row_blk0, nb, l, mats, conv_w, conv_b, kc, ki, kn, skip):
    wt = 256
    nw = HY_W // wt
    h = l // 2
    folded = _fold_dft(l)
    rc = min(h if folded else l, 512)
    sb = math.gcd(nb, 1 if folded else max(1, 4096 // l))
    assert row_blk0 % sb == 0
    hv0, hx10, hx20 = COL_HV // wt, COL_HX1 // wt, COL_HX2 // wt
    seq = lambda c0: pl.BlockSpec((sb * l, wt), lambda j, b: (row_blk0 // sb + b, c0 + j))
    tab = lambda o: pl.BlockSpec((l, wt), lambda j, b: (0, o * nw + j))
    if folded:
        cmat, smat, jmat = mats
        quarter = lambda blk: _resident((h, h), lambda j, b: (blk, 0))
        mat_specs = [quarter(0), quarter(0), quarter(1), quarter(1),
                     quarter(0), quarter(0), quarter(0)]
        mat_args = [cmat, smat, cmat, smat, cmat[h:, :h].T, smat[h:, :h].T, jmat]
        body = _hy_kernel
        scratch = [pltpu.VMEM((sb, l, wt), F32),
                   pltpu.VMEM((sb, h, wt), BF16), pltpu.VMEM((sb, h, wt), BF16),
                   pltpu.VMEM((sb, l, wt), BF16), pltpu.VMEM((sb, l, wt), BF16),
                   pltpu.VMEM((sb, h, wt), BF16)]
    else:
        mat_specs = [_resident((l, l), lambda j, b: (0, 0))] * 2
        mat_args = list(mats)
        body = _hy_direct_kernel
        scratch = [pltpu.VMEM((sb, l, wt), F32), pltpu.VMEM((sb, l, wt), BF16),
                   pltpu.VMEM((sb, l, wt), BF16), pltpu.VMEM((sb, l, wt), BF16)]
    return pl.pallas_call(
        functools.partial(body, l=l, rc=rc, sb=sb),
        grid=(nw, nb // sb),
        in_specs=[
            *mat_specs,
            seq(hv0), seq(hx10), seq(hx20),
            pl.BlockSpec((3, 3, wt), lambda j, b: (0, 0, j)),
            pl.BlockSpec((3, wt), lambda j, b: (0, j)),
            tab(0), tab(0), tab(1), tab(1),
            pl.BlockSpec((HY_ORDER, wt), lambda j, b: (0, j)),
            pl.BlockSpec((HY_ORDER, wt), lambda j, b: (0, j)),
        ],
        out_specs=pl.BlockSpec((sb * l, wt), lambda j, b: (b, j)),
        out_shape=jax.ShapeDtypeStruct((nb * l, HY_W), BF16),
        scratch_shapes=scratch,
        compiler_params=_params(("parallel", "parallel")),
        name=f"hyena_{l}",
    )(*mat_args, proj, proj, proj, conv_w.reshape(3, 3, HY_W),
      conv_b.reshape(3, HY_W), kc, ki, kc, ki, kn.reshape(HY_ORDER, HY_W), skip)


def _log_sigmoid(x):
    return jnp.minimum(x, 0.0) - jnp.log1p(jnp.exp(-jnp.abs(x)))


def _ret_kernel(*refs, l, hb, use_rope, has_s0, has_acc, st_layers, st_own):
    it = iter(refs)
    dl_ref, q_ref, k_ref, v_ref, g_ref = (next(it) for _ in range(5))
    cos_ref, sin_ref = (next(it), next(it)) if use_rope else (None, None)
    s0_ref = next(it) if has_s0 else None
    if has_acc:
        next(it)
    o_ref, st_ref, kr_scr, kv_scr = (next(it) for _ in range(4))
    for other in range(st_layers):
        if other != st_own:
            st_ref[0, other] = jnp.zeros(st_ref.shape[2:], F32)

    c, dk, dv = RET_CHUNK, RET_DK, RET_DV
    nc = l // c
    unroll = min(nc, 4)
    pos = lax.broadcasted_iota(jnp.int32, (c, 1), 0).astype(F32)
    diff = pos - lax.broadcasted_iota(jnp.int32, (1, c), 1).astype(F32)

    def rope(x, r0):
        if not use_rope:
            return x
        return x * cos_ref[pl.ds(r0, c), :] + pltpu.roll(x, dk // 2, 1) * sin_ref[pl.ds(r0, c), :]

    for h in range(hb):
        qs = slice(h * dk, (h + 1) * dk)
        vs = slice(h * dv, (h + 1) * dv)
        lgf = _log_sigmoid(dl_ref[0, h])
        lgb = _log_sigmoid(dl_ref[1, h])
        dm = jnp.where(diff >= 0, jnp.exp(lgf * diff), jnp.exp(-lgb * diff))
        zeta_f = jnp.exp(lgf * (c - 1.0 - pos))
        zeta_b = jnp.exp(lgb * pos)
        xi_f = jnp.exp(lgf * (pos + 1.0))
        xi_b = jnp.exp(lgb * (c - pos))
        gc_f = jnp.exp(lgf * c)
        gc_b = jnp.exp(lgb * c)

        def kv_step(ci, carry):
            r0 = pl.multiple_of(ci * c, c)
            k = rope(k_ref[pl.ds(r0, c), qs].astype(F32), r0) * (dk ** -0.5)
            kr_scr[pl.ds(r0, c), qs] = k.astype(BF16)
            kk = jnp.concatenate([k * zeta_f, k * zeta_b], axis=1).astype(BF16)
            kv_scr[h, ci] = lax.dot_general(kk, v_ref[pl.ds(r0, c), vs], (((0,), (0,)), ((), ())),
                                            preferred_element_type=F32)
            return carry

        lax.fori_loop(0, nc, kv_step, 0, unroll=unroll)

        def fwd_scan(ci, s):
            upd = kv_scr[h, ci, 0:dk, :]
            kv_scr[h, ci, 0:dk, :] = s
            return gc_f * s + upd

        def bwd_scan(i, r):
            ci = nc - 1 - i
            upd = kv_scr[h, ci, dk:2 * dk, :]
            kv_scr[h, ci, dk:2 * dk, :] = r
            return gc_b * r + upd

        zero = jnp.zeros((dk, dv), F32)
        st_ref[0, st_own, 0, h] = lax.fori_loop(0, nc, fwd_scan, s0_ref[0, 0, 0, h] if has_s0 else zero)
        st_ref[0, st_own, 1, h] = lax.fori_loop(0, nc, bwd_scan, s0_ref[0, 0, 1, h] if has_s0 else zero)

        def out_step(ci, carry):
            r0 = pl.multiple_of(ci * c, c)
            q = rope(q_ref[pl.ds(r0, c), qs].astype(F32), r0)
            sc = lax.dot_general(q.astype(BF16), kr_scr[pl.ds(r0, c), qs], (((1,), (1,)), ((), ())),
                                 preferred_element_type=F32) * dm
            o = _dot(sc.astype(BF16), v_ref[pl.ds(r0, c), vs])
            if has_s0 or nc > 1:
                qq = jnp.concatenate([q * xi_f, q * xi_b], axis=1).astype(BF16)
                o = o + _dot(qq, kv_scr[h, ci].astype(BF16))
            o = o * lax.rsqrt(jnp.mean(o * o, axis=-1, keepdims=True) + EPS)
            o_ref[pl.ds(r0, c), vs] = (_silu(g_ref[pl.ds(r0, c), vs].astype(F32)) * o).astype(BF16)
            return carry

        lax.fori_loop(0, nc, out_step, 0, unroll=unroll)


def _retention_branch(proj, row_blk0, nb, l, decay_logit, rope_tabs, state, layer, n_state_layers=1,
                      states_acc=None):
    nh, dk, dv = N_RET_HEADS, RET_DK, RET_DV
    use_rope = rope_tabs is not None
    has_s0 = state is not None
    has_acc = states_acc is not None
    own_layer = layer if n_state_layers > 1 else 0
    st_layers = 1 if has_acc else n_state_layers
    hb = nh if l <= 512 else 1
    ng = nh // hb
    seq = lambda col, w: pl.BlockSpec((l, hb * w), lambda b, g: (row_blk0 + b, col // (hb * w) + g))
    in_specs = [
        pl.BlockSpec((2, hb, 1, 1), lambda b, g: (0, g, 0, 0)),
        seq(COL_Q, dk), seq(COL_K, dk), seq(COL_V, dv), seq(COL_G, dv),
    ]
    args = [decay_logit.reshape(2, nh, 1, 1), proj, proj, proj, proj]
    if use_rope:
        in_specs += [pl.BlockSpec((l, dk), lambda b, g: (0, 0))] * 2
        args += list(rope_tabs)
    if has_s0:
        in_specs.append(pl.BlockSpec((1, 1, 2, hb, dk, dv), lambda b, g: (b, layer, 0, g, 0, 0)))
        args.append(state)
    if has_acc:
        in_specs.append(pl.BlockSpec(memory_space=pl.ANY))
        args.append(states_acc)
    return pl.pallas_call(
        functools.partial(_ret_kernel, l=l, hb=hb, use_rope=use_rope, has_s0=has_s0, has_acc=has_acc,
                          st_layers=st_layers, st_own=0 if has_acc else own_layer),
        grid=(nb, ng),
        in_specs=in_specs,
        out_specs=[
            pl.BlockSpec((l, hb * dv), lambda b, g: (b, g)),
            pl.BlockSpec((1, st_layers, 2, hb, dk, dv),
                         lambda b, g: (b, own_layer if has_acc else 0, 0, g, 0, 0)),
        ],
        out_shape=[
            jax.ShapeDtypeStruct((nb * l, nh * dv), BF16),
            jax.ShapeDtypeStruct((nb, n_state_layers, 2, nh, dk, dv), F32),
        ],
        input_output_aliases={len(args) - 1: 1} if has_acc else {},
        scratch_shapes=[
            pltpu.VMEM((l, hb * dk), BF16),
            pltpu.VMEM((hb, l // RET_CHUNK, 2 * dk, dv), F32),
        ],
        compiler_params=_params(("parallel", "parallel")),
        name=f"retention_{l}",
    )(*args)


ROW_ALIGN = 16


def _sorted_rows(tm):
    worst = TOP_K * tm + N_EXPERTS * (ROW_ALIGN - 1) + ROW_ALIGN
    return -(-worst // LANES) * LANES


def _mix_kernel(zap_ref, zas_ref, zbp_ref, zbs_ref, gap_ref, gas_ref, gbp_ref, gbs_ref, xp_ref, xs_ref,
                mod_ref, wa_ref, wb_ref, wo_ref, nw_ref, rw_ref, rb_ref, x1_ref, xl_ref, route_ref, seg_ref,
                h_scr, pr_scr, sg_scr, *, d, tm, rmax, n_tiles, n_prompt_tiles):
    i = pl.program_id(0)

    @pl.when(i == 0)
    def _():
        h_scr[...] = jnp.zeros_like(h_scr)
        pr_scr[...] = jnp.zeros_like(pr_scr)
        sg_scr[...] = jnp.zeros_like(sg_scr)

    @pl.when(i > n_tiles)
    def _():
        xl_ref[...] = jnp.zeros_like(xl_ref)

    @pl.when(i <= n_tiles)
    def _():
        _mix_step(zap_ref, zas_ref, zbp_ref, zbs_ref, gap_ref, gas_ref, gbp_ref, gbs_ref, xp_ref, xs_ref,
                  mod_ref, wa_ref, wb_ref, wo_ref, nw_ref, rw_ref, rb_ref, x1_ref, xl_ref, route_ref, seg_ref,
                  h_scr, pr_scr, sg_scr, d=d, tm=tm, rmax=rmax, n_tiles=n_tiles, n_prompt_tiles=n_prompt_tiles)


def _mix_step(zap_ref, zas_ref, zbp_ref, zbs_ref, gap_ref, gas_ref, gbp_ref, gbs_ref, xp_ref, xs_ref,
              mod_ref, wa_ref, wb_ref, wo_ref, nw_ref, rw_ref, rb_ref, x1_ref, xl_ref, route_ref, seg_ref,
              h_scr, pr_scr, sg_scr, *, d, tm, rmax, n_tiles, n_prompt_tiles):
    i = pl.program_id(0)
    r_col = lax.broadcasted_iota(jnp.int32, (rmax, 1), 0).astype(F32)
    in_seg = jnp.logical_and(r_col >= sg_scr[0:1], r_col < sg_scr[0:1] + sg_scr[1:2])
    seg_hot = jnp.where(in_seg, 1.0, 0.0).astype(BF16)
    row_of = _dot(jnp.concatenate([seg_hot, seg_hot], axis=1), pr_scr[...])
    sel = jnp.where(row_of == r_col + 1.0, 1.0, 0.0).astype(BF16)
    xl_ref[...] = _dot(sel, h_scr[...]).astype(BF16)

    is_prompt = jnp.minimum(i, n_tiles - 1) < n_prompt_tiles
    pick = lambda p_ref, s_ref: jnp.where(is_prompt, p_ref[...], s_ref[...])
    za, zb = pick(zap_ref, zas_ref), pick(zbp_ref, zbs_ref)
    ga, gb = pick(gap_ref, gas_ref), pick(gbp_ref, gbs_ref)
    merged = (_sigmoid(ga.astype(F32)) * _dot(za, wa_ref[...])
              + _sigmoid(gb.astype(F32)) * _dot(zb, wb_ref[...]))
    gate1 = mod_ref[0, :, 2 * d:3 * d]
    shift2 = mod_ref[0, :, 3 * d:4 * d]
    scale2 = mod_ref[0, :, 4 * d:5 * d]
    x1 = pick(xp_ref, xs_ref) + gate1 * _dot(merged.astype(BF16), wo_ref[...])
    x1_ref[...] = x1
    hn = x1 * lax.rsqrt(jnp.mean(x1 * x1, axis=-1, keepdims=True) + EPS) * nw_ref[...]
    hmo = hn * (1.0 + scale2) + shift2

    h_hi, h_lo = _split_bf16(hmo)
    both = _dot(h_hi, rw_ref[...])
    logits = (both[:, :LANES] + both[:, LANES:]) + _dot(h_lo, rw_ref[:, :LANES]) + rb_ref[...]
    lane = lax.broadcasted_iota(jnp.int32, (1, LANES), 1)
    idxs, vals = [], []
    for _ in range(TOP_K):
        m = jnp.max(logits, axis=1, keepdims=True)
        ik = jnp.min(jnp.where(logits == m, lane, LANES), axis=1, keepdims=True)
        idxs.append(ik)
        vals.append(m)
        logits = jnp.where(lane == ik, -jnp.inf, logits)
    exps = [jnp.exp(v - vals[0]) for v in vals]
    denom = exps[0] + exps[1] + exps[2] + exps[3]

    hot = jnp.zeros((tm, LANES), F32)
    for ik in idxs:
        hot = hot + jnp.where(lane == ik, 1.0, 0.0)
    r_i = lax.broadcasted_iota(jnp.int32, (tm, tm), 0)
    c_i = lax.broadcasted_iota(jnp.int32, (tm, tm), 1)
    below = jnp.where(c_i < r_i, 1.0, 0.0).astype(BF16)
    before = _dot(below, hot.astype(BF16))
    seg = jnp.ceil(jnp.sum(hot, axis=0, keepdims=True) * (1.0 / ROW_ALIGN)) * ROW_ALIGN
    seg_ref[0] = seg
    e_r = lax.broadcasted_iota(jnp.int32, (LANES, LANES), 0)
    e_c = lax.broadcasted_iota(jnp.int32, (LANES, LANES), 1)
    seg_start = _dot3(jnp.broadcast_to(seg, (8, LANES)), jnp.where(e_r < e_c, 1.0, 0.0))[0:1]
    base = seg_start + before

    route = jnp.zeros((tm, LANES), F32)
    for k in range(TOP_K):
        srow = jnp.sum(jnp.where(lane == idxs[k], base, 0.0), axis=1, keepdims=True)
        route = route + jnp.where(lane == k, srow, 0.0)
        route = route + jnp.where(lane == TOP_K + k, idxs[k].astype(F32), 0.0)
        route = route + jnp.where(lane == 2 * TOP_K + k, exps[k] / denom, 0.0)
    route_ref[...] = route

    h_scr[...] = h_hi
    pair_row = jnp.where(hot > 0.0, base + 1.0, 0.0).T
    pr_scr[...] = jnp.concatenate(_split_bf16(pair_row), axis=0)
    sg_scr[0:1] = seg_start
    sg_scr[1:2] = seg


def _mix(za, zb, proj, x, mod3, mod_row, wa, wb, wo, norm_w, rw, rb, tm, spare_rows):
    d = x[0].shape[1]
    n_p = x[0].shape[0] // tm
    n_s = x[1].shape[0] // tm
    nt = n_p + n_s
    t = nt * tm
    rmax = _sorted_rows(tm)
    spare_blocks = -(-spare_rows // rmax)
    vw = N_RET_HEADS * RET_DV
    cur = lambda i: jnp.minimum(i, nt - 1)
    prev = lambda i: jnp.maximum(i - 1, 0)
    pair = lambda w, col: [pl.BlockSpec((tm, w), lambda i: (jnp.minimum(cur(i), n_p - 1), col)),
                           pl.BlockSpec((tm, w), lambda i: (jnp.maximum(cur(i) - n_p, 0), col))]
    const = lambda i: (0, 0)
    return pl.pallas_call(
        functools.partial(_mix_kernel, d=d, tm=tm, rmax=rmax, n_tiles=nt, n_prompt_tiles=n_p),
        grid=(nt + 1 + spare_blocks,),
        in_specs=[
            *pair(HY_W, 0), *pair(vw, 0), *pair(d, COL_GA // d), *pair(d, COL_GB // d), *pair(d, 0),
            pl.BlockSpec((1, 1, 6 * d), lambda i: (mod_row(cur(i)), 0, 0)),
            _resident((HY_W, d), const), _resident((vw, d), const), _resident((d, d), const),
            pl.BlockSpec((1, d), const),
            _resident((d, 2 * LANES), const), pl.BlockSpec((1, LANES), const),
        ],
        out_specs=[
            pl.BlockSpec((tm, d), lambda i: (cur(i), 0)),
            pl.BlockSpec((rmax, d), lambda i: (prev(i), 0)),
            pl.BlockSpec((tm, LANES), lambda i: (cur(i), 0)),
            pl.BlockSpec((1, 1, LANES), lambda i: (cur(i), 0, 0)),
        ],
        out_shape=[
            jax.ShapeDtypeStruct((t, d), F32),
            jax.ShapeDtypeStruct(((nt + spare_blocks) * rmax, d), BF16),
            jax.ShapeDtypeStruct((t, LANES), F32),
            jax.ShapeDtypeStruct((nt, 1, LANES), F32),
        ],
        scratch_shapes=[pltpu.VMEM((tm, d), BF16), pltpu.VMEM((2 * LANES, tm), BF16),
                        pltpu.VMEM((2, LANES), F32)],
        compiler_params=_params(("arbitrary",)),
        name="merge_router",
    )(*za, *zb, *proj, *proj, *x, mod3, wa, wb, wo, norm_w.reshape(1, d), rw, rb)


def _ffn_kernel(te_ref, nu_ref, src_ref, dst_ref, xl_ref, wgu_ref, bgu_ref, wd_ref, bd_ref, yl_ref,
                xbuf, ybuf, gsem, ssem, wgu_scr, wd_scr, *, dff, te):
    j = pl.program_id(0)
    nu = nu_ref[0]
    cpt = te // ROW_ALIGN
    slot = j % 2

    def hbm_chunk(ci):
        return pl.ds(pl.multiple_of(ci * ROW_ALIGN, ROW_ALIGN), ROW_ALIGN)

    def start_gather(tile, buf):
        for ci in range(cpt):
            pltpu.make_async_copy(xl_ref.at[hbm_chunk(src_ref[tile * cpt + ci])],
                                  xbuf.at[buf, pl.ds(ci * ROW_ALIGN, ROW_ALIGN)], gsem.at[buf]).start()

    def wait_gather(buf):
        pltpu.make_async_copy(xl_ref.at[pl.ds(0, te)], xbuf.at[buf], gsem.at[buf]).wait()

    def start_scatter(tile, buf):
        for ci in range(cpt):
            pltpu.make_async_copy(ybuf.at[buf, pl.ds(ci * ROW_ALIGN, ROW_ALIGN)],
                                  yl_ref.at[hbm_chunk(dst_ref[tile * cpt + ci])], ssem.at[buf]).start()

    def wait_scatter(buf):
        pltpu.make_async_copy(ybuf.at[buf], yl_ref.at[pl.ds(0, te)], ssem.at[buf]).wait()

    @pl.when(j < nu)
    def _():
        @pl.when(j == 0)
        def _():
            start_gather(0, 0)

        start_gather(j + 1, 1 - slot)
        wait_gather(slot)

        @pl.when(j >= 2)
        def _():
            wait_scatter(slot)

        @pl.when(jnp.logical_or(j == 0, te_ref[j] != te_ref[jnp.maximum(j - 1, 0)]))
        def _():
            wgu_scr[...] = wgu_ref[0].astype(BF16)
            wd_scr[...] = wd_ref[0].astype(BF16)

        hgu = _dot(xbuf[slot], wgu_scr[...]) + bgu_ref[0]
        gate = jnp.minimum(hgu[:, :dff], SWIGLU_LIMIT)
        up = jnp.clip(hgu[:, dff:], -SWIGLU_LIMIT, SWIGLU_LIMIT)
        act = gate * _sigmoid(SWIGLU_ALPHA * gate) * (up + 1.0)
        ybuf[slot] = (_dot(act.astype(BF16), wd_scr[...]) + bd_ref[0]).astype(BF16)
        start_scatter(j, slot)

        @pl.when(j == nu - 1)
        def _():
            wait_gather(1 - slot)

            @pl.when(j >= 1)
            def _():
                wait_scatter(1 - slot)
            wait_scatter(slot)


def _expert_ffn(xl, tile_expert, n_used, src_chunk, dst_chunk, w_gate_up, b_gate_up, w_down, b_down, layer, te):
    _, d = xl.shape
    depth, n_exp, _, two_ff = w_gate_up.shape
    ne = depth * n_exp
    dff = two_ff // 2
    wsel = lambda j, tex, nu, src, dst: (layer * n_exp + tex[j], 0, 0)
    return pl.pallas_call(
        functools.partial(_ffn_kernel, dff=dff, te=te),
        grid_spec=pltpu.PrefetchScalarGridSpec(
            num_scalar_prefetch=4,
            grid=(tile_expert.shape[0],),
            in_specs=[
                pl.BlockSpec(memory_space=pl.ANY),
                pl.BlockSpec((1, d, two_ff), wsel),
                pl.BlockSpec((1, 1, two_ff), wsel),
                pl.BlockSpec((1, dff, d), wsel),
                pl.BlockSpec((1, 1, d), wsel),
            ],
            out_specs=pl.BlockSpec(memory_space=pl.ANY),
            scratch_shapes=[
                pltpu.VMEM((2, te, d), BF16), pltpu.VMEM((2, te, d), BF16),
                pltpu.SemaphoreType.DMA((2,)), pltpu.SemaphoreType.DMA((2,)),
                pltpu.VMEM((d, two_ff), BF16), pltpu.VMEM((dff, d), BF16),
            ],
        ),
        out_shape=jax.ShapeDtypeStruct(xl.shape, BF16),
        input_output_aliases={4: 0},
        compiler_params=_params(("arbitrary",)),
        name="moe_ffn",
    )(tile_expert, n_used, src_chunk, dst_chunk, xl, w_gate_up.reshape(ne, d, two_ff),
      b_gate_up.reshape(ne, 1, two_ff), w_down.reshape(ne, dff, d), b_down.reshape(ne, 1, d))


def _comb_kernel(yl_ref, route_ref, x1_ref, mod_ref, fw_ref, o_ref, *, d, tm, rmax, final_norm):
    route = route_ref[...]
    r_iota = lax.broadcasted_iota(jnp.int32, (1, rmax), 1).astype(F32)
    wm = jnp.zeros((tm, rmax), F32)
    for k in range(TOP_K):
        wm = jnp.where(r_iota == route[:, k:k + 1], route[:, 2 * TOP_K + k:2 * TOP_K + k + 1], wm)
    out = _dot(wm.astype(BF16), yl_ref[...])
    x2 = x1_ref[...] + mod_ref[0, :, 5 * d:6 * d] * out
    if final_norm:
        x2 = x2 * lax.rsqrt(jnp.mean(x2 * x2, axis=-1, keepdims=True) + EPS) * fw_ref[...]
    o_ref[...] = x2


def _combine(yl, route, x1, mod3, mod_row, final_w, tile0, n_tiles, tm, final_norm):
    d = x1.shape[1]
    rmax = _sorted_rows(tm)
    return pl.pallas_call(
        functools.partial(_comb_kernel, d=d, tm=tm, rmax=rmax, final_norm=final_norm),
        grid=(n_tiles,),
        in_specs=[
            pl.BlockSpec((rmax, d), lambda i: (tile0 + i, 0)),
            pl.BlockSpec((tm, LANES), lambda i: (tile0 + i, 0)),
            pl.BlockSpec((tm, d), lambda i: (tile0 + i, 0)),
            pl.BlockSpec((1, 1, 6 * d), lambda i: (mod_row(tile0 + i), 0, 0)),
            pl.BlockSpec((1, d), lambda i: (0, 0)),
        ],
        out_specs=pl.BlockSpec((tm, d), lambda i: (i, 0)),
        out_shape=jax.ShapeDtypeStruct((n_tiles * tm, d), F32),
        compiler_params=_params(("parallel",)),
        name="moe_combine",
    )(yl, route, x1, mod3, final_w.reshape(1, d))


@functools.lru_cache(maxsize=None)
def _dft_mats(l):
    s = np.arange(l, dtype=np.int64)
    f = np.concatenate([np.arange(0, l, 2), np.arange(1, l, 2)]).astype(np.int64) if _fold_dft(l) else s
    ang = ((f[:, None] * s[None, :]) % (2 * l)).astype(np.float64) * (math.pi / l)
    mats = (np.cos(ang).astype(np.float32), np.sin(ang).astype(np.float32))
    if _fold_dft(l):
        a = np.arange(l // 2)
        mats += (((a[:, None] + a[None, :] == l // 2) & (a[:, None] >= 1)).astype(np.float32),)
    return mats


@functools.lru_cache(maxsize=None)
def _filter_features(l):
    f32 = np.float32
    t = np.linspace(0.0, 1.0, l, dtype=f32)[:, None]
    bands = (HY_EMB - 1) // 2
    w = f32(2.0 * math.pi) * np.arange(l, dtype=f32)[:, None] / f32(l)
    f = np.linspace(1e-4, bands - 1, bands, dtype=f32)[None, :]
    z = np.concatenate([t, np.cos(f * w), -np.sin(f * w)], axis=-1).astype(f32)
    return np.pad(z, ((0, 0), (0, LANES - HY_EMB))), t


@functools.lru_cache(maxsize=None)
def _rope_tables(rows):
    f32 = np.float32
    row = np.repeat(np.arange(rows, dtype=f32), GRID_W)
    col = np.tile(np.arange(GRID_W, dtype=f32), rows)
    half = RET_DK // 4
    inv = (f32(ROPE_BASE) ** (-np.arange(half, dtype=f32) / f32(half))).astype(f32)
    ang = np.concatenate([row[:, None] * inv, col[:, None] * inv], axis=-1)
    ang = np.concatenate([ang, ang], axis=-1).astype(f32)
    sign = np.where(np.arange(RET_DK) < RET_DK // 2, -1.0, 1.0).astype(f32)
    return np.cos(ang), np.sin(ang) * sign


@functools.lru_cache(maxsize=None)
def _decay_rates():
    return np.linspace(math.log(HY_TARGET) / HY_SLOW_PCT, math.log(HY_TARGET) / HY_FAST_PCT,
                       HY_W, dtype=np.float32)[None, :]


def _pad2(a, rows, cols):
    return jnp.pad(a, ((0, rows - a.shape[0]), (0, cols - a.shape[1])))


def _ffn_plan(seg, rmax, te, n_ffn_tiles):
    nt, ne = seg.shape
    cum_t = jnp.cumsum(seg, axis=0)
    tot = cum_t[-1]
    padded = ((tot + te - 1) // te) * te
    ends = jnp.cumsum(padded)
    offs = ends - padded
    n_used = ends[-1] // te
    expert_of = lambda rows: jnp.minimum(jnp.sum(ends[None, :] <= rows[:, None], axis=1), ne - 1)
    tiles = jnp.arange(n_ffn_tiles, dtype=jnp.int32)
    tile_expert = expert_of(jnp.minimum(tiles, n_used - 1) * te).astype(jnp.int32)

    rows = jnp.arange(n_ffn_tiles * te // ROW_ALIGN, dtype=jnp.int32) * ROW_ALIGN
    hot_e = expert_of(rows)[:, None] == jnp.arange(ne)[None, :]
    pick_e = lambda tab: jnp.sum(jnp.where(hot_e[:, None, :], tab[None], 0), axis=2)
    r_s = rows - pick_e(offs[None, :])[:, 0]
    cum_sel = pick_e(cum_t)
    i_s = jnp.minimum(jnp.sum(cum_sel <= r_s[:, None], axis=1), nt - 1)
    hot_i = i_s[:, None] == jnp.arange(nt)[None, :]
    seg_base = (jnp.arange(nt)[:, None] * rmax + jnp.cumsum(seg, axis=1) - seg) - (cum_t - seg)
    src_row = r_s + jnp.sum(jnp.where(hot_i, pick_e(seg_base), 0), axis=1)
    valid = (r_s < pick_e(tot[None, :])[:, 0]) & (rows < ends[-1])
    src_chunk = jnp.where(valid, src_row // ROW_ALIGN, rmax // ROW_ALIGN - 1).astype(jnp.int32)
    pad_rank = jnp.cumsum(jnp.where(valid, 0, 1)) - 1
    spare = nt * rmax // ROW_ALIGN + pad_rank % _spare_chunks(te)
    dst_chunk = jnp.where(valid, src_row // ROW_ALIGN, spare).astype(jnp.int32)
    return tile_expert, n_used.reshape(1).astype(jnp.int32), src_chunk, dst_chunk


def _spare_chunks(te):
    return 4 * (te // ROW_ALIGN)


def kernel(x_prompt, x_sample, state_ret, c, c_ctx, ada_w, ada_b, norm_mix_w, w_in, conv_w, conv_b,
           f_w1, f_b1, f_w2, f_b2, f_w3, f_b3, f_freq, f_wout, hy_skip, ret_decay_logit,
           w_branch_a, w_branch_b, w_out, norm_ffn_w, router_w, router_b, w_gate_up, b_gate_up,
           w_down, b_down, final_norm_w):
    bp, lp, d = x_prompt.shape
    bs, ls, _ = x_sample.shape
    depth = ada_w.shape[0]
    tp, ts = bp * lp, bs * ls
    t = tp + ts
    assert ls % lp == 0 and tp % ls == 0 and lp % RET_CHUNK == 0 and ls % GRID_W == 0
    tm = 512 if (tp % 512 == 0 and ls % 512 == 0) else lp
    tm_in = tm
    te = 512
    nt = t // tm
    rmax = _sorted_rows(tm)
    n_ffn_tiles = -(-(t * TOP_K + nt * N_EXPERTS * (ROW_ALIGN - 1)) // te) + N_EXPERTS

    dft = {l: tuple(jnp.asarray(a).astype(BF16) for a in _dft_mats(l)) for l in (lp, ls)}
    feats = {l: tuple(jnp.asarray(a) for a in _filter_features(l)) for l in (lp, ls)}
    rope_tabs = tuple(jnp.asarray(a) for a in _rope_tables(ls // GRID_W))
    deltas = jnp.asarray(_decay_rates())

    cvec = jnp.zeros((8, d), F32).at[:bs].set(c).at[bs].set(c_ctx)
    mod3 = _modulation(cvec, ada_w, ada_b).reshape(depth * 8, 1, 6 * d)
    n_p, n_s = tp // tm, ts // tm

    x = (x_prompt.reshape(tp, d), x_sample.reshape(ts, d))
    states = None
    for l in range(depth):
        mod_row = lambda i, l=l: l * 8 + jnp.where(i < n_p, bs, jnp.maximum(i - n_p, 0) // (ls // tm))
        w_in_bf = w_in[l].astype(BF16)
        proj = (_in_proj(x[0], norm_mix_w[l], mod3, lambda i, l=l: l * 8 + bs, w_in_bf, tm_in),
                _in_proj(x[1], norm_mix_w[l], mod3, lambda i, l=l: l * 8 + i // (ls // tm_in), w_in_bf, tm_in))

        w1 = _pad2(f_w1[l], LANES, LANES)
        w2 = _pad2(f_w2[l], LANES, LANES)
        w3 = _pad2(f_w3[l], LANES, LANES)
        wo = _pad2(f_wout[l], LANES, f_wout.shape[2])
        b1, b2, b3, fr = (_pad2(a[l][None, :], 1, LANES) for a in (f_b1, f_b2, f_b3, f_freq))
        za, zb = [], []
        for grp, (nb, ln) in enumerate(((bp, lp), (bs, ls))):
            taps, sums = _hyena_taps(ln, feats[ln][0], feats[ln][1], deltas, w1, b1, w2, b2, w3, b3, fr, wo)
            kc, ki, kn = _filter_spectra(ln, dft[ln][0], dft[ln][1], taps, sums)
            za.append(_hyena_branch(proj[grp], 0, nb, ln, dft[ln], conv_w[l], conv_b[l], kc, ki, kn, hy_skip[l]))
            if grp == 0:
                zb_g, states = _retention_branch(proj[grp], 0, nb, ln, ret_decay_logit[l], None, None, l,
                                                 depth, states)
            else:
                zb_g, _ = _retention_branch(proj[grp], 0, nb, ln, ret_decay_logit[l], rope_tabs, state_ret, l)
            zb.append(zb_g)

        rw = jnp.concatenate(_split_bf16(_pad2(router_w[l], d, LANES)), axis=1)
        rb = jnp.full((1, LANES), -jnp.inf, F32).at[0, :N_EXPERTS].set(router_b[l])
        x1, xl, route, seg = _mix(za, zb, proj, x, mod3, mod_row,
                                  w_branch_a[l].astype(BF16), w_branch_b[l].astype(BF16),
                                  w_out[l].astype(BF16), norm_ffn_w[l], rw, rb, tm, _spare_chunks(te) * ROW_ALIGN)
        tile_expert, n_used, src_chunk, dst_chunk = _ffn_plan(seg[:, 0, :N_EXPERTS].astype(jnp.int32), rmax, te,
                                                              n_ffn_tiles)
        yl = _expert_ffn(xl, tile_expert, n_used, src_chunk, dst_chunk, w_gate_up, b_gate_up, w_down, b_down,
                         l, te)
        last = l == depth - 1
        x = (_combine(yl, route, x1, mod3, mod_row, final_norm_w, 0, n_p, tm, last),
             _combine(yl, route, x1, mod3, mod_row, final_norm_w, n_p, n_s, tm, last))

    return (x[0].reshape(bp, lp, d), x[1].reshape(bs, ls, d), states)
```

```python
import functools
import math

import jax
import jax.numpy as jnp
import numpy as np
from jax import lax
from jax.experimental import pallas as pl
from jax.experimental.pallas import tpu as pltpu

F32 = jnp.float32
BF16 = jnp.bfloat16

GRID_W = 64
HY_W = 512
HY_ORDER = 2
HY_EMB = 33
HY_FAST_PCT = 0.3
HY_SLOW_PCT = 1.5
HY_TARGET = 1e-2
N_RET_HEADS = 4
RET_DK = 128
RET_DV = 256
RET_CHUNK = 256
ROPE_BASE = 10000.0
N_EXPERTS = 32
TOP_K = 4
SWIGLU_LIMIT = 7.0
SWIGLU_ALPHA = 1.702
EPS = 1e-6

LANES = 128
VMEM_LIMIT = 56 * 1024 * 1024

COL_V, COL_G, COL_GA, COL_GB = 0, 1024, 2048, 3072
COL_HV, COL_HX1, COL_HX2, COL_Q, COL_K = 4096, 4608, 5120, 5632, 6144
IN_W = 6656
IN_COL_BLOCKS = (5, 6, 7, 8, 9, 10, 11, 12, 0, 1, 2, 3, 4)


def _dot(a, b):
    return jnp.dot(a, b, preferred_element_type=F32)


def _split_bf16(a):
    hi = a.astype(BF16)
    lo = (a - hi.astype(F32)).astype(BF16)
    return hi, lo


def _dot3(a, b):
    ah, al = _split_bf16(a)
    bh, bl = _split_bf16(b)
    return _dot(ah, bh) + (_dot(ah, bl) + _dot(al, bh))


def _sigmoid(x):
    return 0.5 * jnp.tanh(0.5 * x) + 0.5


def _silu(x):
    return x * _sigmoid(x)


def _params(sem, vmem=VMEM_LIMIT):
    return pltpu.CompilerParams(dimension_semantics=sem, vmem_limit_bytes=vmem)


def _resident(shape, index_map):
    return pl.BlockSpec(shape, index_map, pipeline_mode=pl.Buffered(1))


def _mod_kernel(c_ref, w_ref, b_ref, o_ref):
    o_ref[0] = _dot3(_silu(c_ref[...]), w_ref[0]) + b_ref[0]


def _modulation(cvec, ada_w, ada_b):
    depth, d, six_d = ada_w.shape
    tn = six_d // 4
    return pl.pallas_call(
        _mod_kernel,
        grid=(depth, six_d // tn),
        in_specs=[
            pl.BlockSpec((8, d), lambda l, j: (0, 0)),
            pl.BlockSpec((1, d, tn), lambda l, j: (l, 0, j)),
            pl.BlockSpec((1, 1, tn), lambda l, j: (l, 0, j)),
        ],
        out_specs=pl.BlockSpec((1, 8, tn), lambda l, j: (l, 0, j)),
        out_shape=jax.ShapeDtypeStruct((depth, 8, six_d), F32),
        compiler_params=_params(("parallel", "parallel")),
        name="adaln_mod",
    )(cvec, ada_w, ada_b.reshape(depth, 1, six_d))


def _in_kernel(x_ref, nw_ref, mod_ref, w_ref, o_ref, *, d, cw):
    x = x_ref[...]
    y = x * lax.rsqrt(jnp.mean(x * x, axis=-1, keepdims=True) + EPS) * nw_ref[...]
    shift = mod_ref[0, :, 0:d]
    scale = mod_ref[0, :, d:2 * d]
    u = (y * (1.0 + scale) + shift).astype(BF16)
    for j, src in enumerate(IN_COL_BLOCKS):
        o_ref[:, j * cw:(j + 1) * cw] = _dot(u, w_ref[:, src * cw:(src + 1) * cw]).astype(BF16)


def _in_proj(x, norm_w, mod3, mod_row, w_in_bf, tm):
    t, d = x.shape
    return pl.pallas_call(
        functools.partial(_in_kernel, d=d, cw=IN_W // len(IN_COL_BLOCKS)),
        grid=(t // tm,),
        in_specs=[
            pl.BlockSpec((tm, d), lambda i: (i, 0)),
            pl.BlockSpec((1, d), lambda i: (0, 0)),
            pl.BlockSpec((1, 1, 2 * d), lambda i: (mod_row(i), 0, 0)),
            _resident((d, IN_W), lambda i: (0, 0)),
        ],
        out_specs=pl.BlockSpec((tm, IN_W), lambda i: (i, 0)),
        out_shape=jax.ShapeDtypeStruct((t, IN_W), BF16),
        compiler_params=_params(("parallel",)),
        name="in_proj",
    )(x, norm_w.reshape(1, d), mod3, w_in_bf)


def _filt_kernel(z_ref, t_ref, w1, b1, w2, b2, w3, b3, fr, wo, dl_ref, h_ref, s_ref, *, tl):
    i = pl.program_id(0)
    freq = fr[...]
    h = jnp.sin(freq * (_dot3(z_ref[...], w1[...]) + b1[...]))
    h = jnp.sin(freq * (_dot3(h, w2[...]) + b2[...]))
    h = jnp.sin(freq * (_dot3(h, w3[...]) + b3[...]))
    h = _dot3(h, wo[...])
    decay = jnp.exp(-t_ref[...] * jnp.abs(dl_ref[...]))
    h = h * jnp.concatenate([decay] * (2 * HY_ORDER), axis=1)
    h_ref[...] = h
    row = i * tl + lax.broadcasted_iota(jnp.int32, (tl, 1), 0)
    col = lax.broadcasted_iota(jnp.int32, (1, 2 * HY_ORDER * HY_W), 1)
    is_bwd = ((col // HY_W) % 2) == 1
    part = jnp.sum(jnp.where((row == 0) & is_bwd, 0.0, jnp.abs(h)), axis=0, keepdims=True)

    @pl.when(i == 0)
    def _():
        s_ref[...] = jnp.zeros_like(s_ref)

    s_ref[...] += part


def _hyena_taps(l, zfeat, tcol, deltas, w1, b1, w2, b2, w3, b3, freq, wout):
    tl = min(l, 256)
    nw = 2 * HY_ORDER * HY_W
    full = lambda shape: pl.BlockSpec(shape, lambda i: (0,) * len(shape))
    return pl.pallas_call(
        functools.partial(_filt_kernel, tl=tl),
        grid=(l // tl,),
        in_specs=[
            pl.BlockSpec((tl, LANES), lambda i: (i, 0)),
            pl.BlockSpec((tl, 1), lambda i: (i, 0)),
            full((LANES, LANES)), full((1, LANES)),
            full((LANES, LANES)), full((1, LANES)),
            full((LANES, LANES)), full((1, LANES)),
            full((1, LANES)),
            full((LANES, nw)),
            full((1, HY_W)),
        ],
        out_specs=[pl.BlockSpec((tl, nw), lambda i: (i, 0)), pl.BlockSpec((1, nw), lambda i: (0, 0))],
        out_shape=[jax.ShapeDtypeStruct((l, nw), F32), jax.ShapeDtypeStruct((1, nw), F32)],
        compiler_params=_params(("arbitrary",)),
        name=f"hyena_taps_{l}",
    )(zfeat, tcol, w1, b1, w2, b2, w3, b3, freq, wout, deltas)


def _fdft_kernel(c_ref, s_ref, hf_ref, hb_ref, sf_ref, sb_ref, kc_ref, ki_ref, kn_ref, *, l):
    row = lax.broadcasted_iota(jnp.int32, (l, 1), 0)
    hf = hf_ref[...]
    hb = jnp.where(row == 0, 0.0, hb_ref[...])
    inv = 1.0 / (sf_ref[...] + sb_ref[...])
    a = hf + hb
    b = hb - hf
    wf = jnp.where(row == 0, 0.5 / l, 1.0 / l)
    kc_ref[...] = _dot(c_ref[...], a.astype(BF16)) * wf * inv
    ki_ref[...] = _dot(s_ref[...], b.astype(BF16)) * wf * inv
    sgn = jnp.where(row % 2 == 0, 1.0, -1.0)
    kn_ref[...] = jnp.sum(a * sgn, axis=0, keepdims=True) * inv * (0.5 / l)


def _filter_spectra(l, cmat, smat, taps, sums):
    wt = 256
    nb = HY_W // wt
    two_w = 2 * HY_W
    return pl.pallas_call(
        functools.partial(_fdft_kernel, l=l),
        grid=(HY_ORDER, nb),
        in_specs=[
            _resident((l, l), lambda o, j: (0, 0)),
            _resident((l, l), lambda o, j: (0, 0)),
            pl.BlockSpec((l, wt), lambda o, j: (0, o * 2 * nb + j)),
            pl.BlockSpec((l, wt), lambda o, j: (0, o * 2 * nb + nb + j)),
            pl.BlockSpec((1, wt), lambda o, j: (0, o * 2 * nb + j)),
            pl.BlockSpec((1, wt), lambda o, j: (0, o * 2 * nb + nb + j)),
        ],
        out_specs=[
            pl.BlockSpec((l, wt), lambda o, j: (0, o * nb + j)),
            pl.BlockSpec((l, wt), lambda o, j: (0, o * nb + j)),
            pl.BlockSpec((1, wt), lambda o, j: (0, o * nb + j)),
        ],
        out_shape=[
            jax.ShapeDtypeStruct((l, HY_ORDER * HY_W), F32),
            jax.ShapeDtypeStruct((l, HY_ORDER * HY_W), F32),
            jax.ShapeDtypeStruct((1, HY_ORDER * HY_W), F32),
        ],
        compiler_params=_params(("parallel", "parallel")),
        name=f"hyena_spectra_{l}",
    )(cmat, smat, taps, taps, sums, sums)


def _hy_direct_kernel(c_ref, s_ref, hv_ref, hx1_ref, hx2_ref, cw_ref, cb_ref,
                      kc0, ki0, kc1, ki1, kn_ref, sk_ref, o_ref, z_scr, zb_scr, p_scr, q_scr, *, l, rc, sb):
    row = lax.broadcasted_iota(jnp.int32, (l, 1), 0)
    first = row == 0
    last = row == l - 1
    sgn = jnp.where(row % 2 == 0, 1.0, -1.0)
    cw = cw_ref[...]
    cb = cb_ref[...]

    def short_conv(ref, rows, part):
        x = ref[rows, :].astype(F32)
        xm = jnp.where(first, 0.0, pltpu.roll(x, 1, 0))
        xp = jnp.where(last, 0.0, pltpu.roll(x, l - 1, 0))
        return (xm * cw[0, part:part + 1] + x * cw[1, part:part + 1] + xp * cw[2, part:part + 1]
                + cb[part:part + 1])

    for s in range(sb):
        rows = slice(s * l, (s + 1) * l)
        z_scr[s] = short_conv(hv_ref, rows, 0)
        for o, (gate_ref, kc, ki) in enumerate(((hx1_ref, kc0, ki0), (hx2_ref, kc1, ki1))):
            z = z_scr[s]
            zb_scr[s] = z.astype(BF16)
            nyq = jnp.sum(z * sgn, axis=0, keepdims=True) * kn_ref[o:o + 1]
            for r in range(l // rc):
                sl = slice(r * rc, (r + 1) * rc)
                zc = _dot(c_ref[sl, :], zb_scr[s])
                zs = _dot(s_ref[sl, :], zb_scr[s])
                p_scr[s, sl, :] = (zc * kc[sl, :] + zs * ki[sl, :]).astype(BF16)
                q_scr[s, sl, :] = (zs * kc[sl, :] - zc * ki[sl, :]).astype(BF16)
            gate = short_conv(gate_ref, rows, o + 1)
            for r in range(l // rc):
                sl = slice(r * rc, (r + 1) * rc)
                y = _dot(c_ref[sl, :], p_scr[s]) + _dot(s_ref[sl, :], q_scr[s])
                y = y + sgn[sl] * nyq + z_scr[s, sl, :] * sk_ref[o:o + 1]
                z_scr[s, sl, :] = gate[sl] * y
        o_ref[rows, :] = z_scr[s].astype(BF16)


def _hy_kernel(ce_ref, se_ref, co_ref, so_ref, cot_ref, sot_ref, j_ref, hv_ref, hx1_ref, hx2_ref, cw_ref, cb_ref,
               kc0, ki0, kc1, ki1, kn_ref, sk_ref, o_ref, z_scr, zp_scr, zm_scr, p_scr, q_scr, yr_scr,
               *, l, rc, sb):
    h = l // 2
    row = lax.broadcasted_iota(jnp.int32, (l, 1), 0)
    first = row == 0
    last = row == l - 1
    sgn = jnp.where(row % 2 == 0, 1.0, -1.0)
    half_row = lax.broadcasted_iota(jnp.int32, (h, 1), 0)
    sm = jnp.where(half_row % 2 == 0, 1.0, -1.0)
    cw = cw_ref[...]
    cb = cb_ref[...]

    def short_conv(ref, rows, part):
        x = ref[rows, :].astype(F32)
        xm = jnp.where(first, 0.0, pltpu.roll(x, 1, 0))
        xp = jnp.where(last, 0.0, pltpu.roll(x, l - 1, 0))
        return (xm * cw[0, part:part + 1] + x * cw[1, part:part + 1] + xp * cw[2, part:part + 1]
                + cb[part:part + 1])

    for s in range(sb):
        rows = slice(s * l, (s + 1) * l)
        z_scr[s] = short_conv(hv_ref, rows, 0)
        for o, (gate_ref, kc, ki) in enumerate(((hx1_ref, kc0, ki0), (hx2_ref, kc1, ki1))):
            z = z_scr[s]
            nyq = jnp.sum(z * sgn, axis=0, keepdims=True) * kn_ref[o:o + 1]
            zmid = z[h:h + 1]
            zrev = _dot(j_ref[...], z[h:].astype(BF16))
            zp_scr[s] = (z[:h] + zrev).astype(BF16)
            zm_scr[s] = (z[:h] - zrev).astype(BF16)
            ymid = jnp.zeros_like(zmid)
            for r in range(h // rc):
                sl = slice(r * rc, (r + 1) * rc)
                so_ = slice(h + r * rc, h + (r + 1) * rc)
                zc_e = _dot(ce_ref[sl, :], zp_scr[s]) + sm[sl] * zmid
                zs_e = _dot(se_ref[sl, :], zm_scr[s])
                zc_o = _dot(co_ref[sl, :], zm_scr[s])
                zs_o = _dot(so_ref[sl, :], zp_scr[s]) + sm[sl] * zmid
                pe = zc_e * kc[sl, :] + zs_e * ki[sl, :]
                qe = zs_e * kc[sl, :] - zc_e * ki[sl, :]
                po = zc_o * kc[so_, :] + zs_o * ki[so_, :]
                qo = zs_o * kc[so_, :] - zc_o * ki[so_, :]
                ymid = ymid + jnp.sum(sm[sl] * (pe + qo), axis=0, keepdims=True)
                p_scr[s, sl, :] = pe.astype(BF16)
                q_scr[s, sl, :] = qe.astype(BF16)
                p_scr[s, so_, :] = po.astype(BF16)
                q_scr[s, so_, :] = qo.astype(BF16)
            gate = short_conv(gate_ref, rows, o + 1)
            skip = sk_ref[o:o + 1]
            for r in range(h // rc):
                sl = slice(r * rc, (r + 1) * rc)
                a1 = _dot(ce_ref[sl, :], p_scr[s, :h, :])
                a2 = _dot(se_ref[sl, :], q_scr[s, :h, :])
                a3 = _dot(cot_ref[sl, :], p_scr[s, h:, :])
                a4 = _dot(sot_ref[sl, :], q_scr[s, h:, :])
                yr_scr[s, sl, :] = ((a1 + a4) - (a2 + a3)).astype(BF16)
                y = (a1 + a4) + (a2 + a3) + sgn[sl] * nyq + z[sl] * skip
                z_scr[s, sl, :] = gate[sl] * y
            y_hi = jnp.where(half_row == 0, ymid, _dot(j_ref[...], yr_scr[s]))
            y_hi = y_hi + sgn[h:] * nyq + z[h:] * skip
            z_scr[s, h:, :] = gate[h:] * y_hi
        o_ref[rows, :] = z_scr[s].astype(BF16)


def _fold_dft(l):
    return l >= 1024


def _hyena_branch(proj, row_blk0, nb, l, mats, conv_w, conv_b, kc, ki, kn, skip):
    folded = _fold_dft(l)
    wt = 256 if folded else HY_W
    nw = HY_W // wt
    h = l // 2
    rc = min(h if folded else l, 512)
    sb = math.gcd(nb, 1 if folded else max(1, 2048 // l))
    assert row_blk0 % sb == 0
    hv0, hx10, hx20 = COL_HV // wt, COL_HX1 // wt, COL_HX2 // wt
    seq = lambda c0: pl.BlockSpec((sb * l, wt), lambda j, b: (row_blk0 // sb + b, c0 + j))
    tab = lambda o: pl.BlockSpec((l, wt), lambda j, b: (0, o * nw + j))
    if folded:
        cmat, smat, jmat = mats
        quarter = lambda blk: _resident((h, h), lambda j, b: (blk, 0))
        mat_specs = [quarter(0), quarter(0), quarter(1), quarter(1),
                     quarter(0), quarter(0), quarter(0)]
        mat_args = [cmat, smat, cmat, smat, cmat[h:, :h].T, smat[h:, :h].T, jmat]
        body = _hy_kernel
        scratch = [pltpu.VMEM((sb, l, wt), F32),
                   pltpu.VMEM((sb, h, wt), BF16), pltpu.VMEM((sb, h, wt), BF16),
                   pltpu.VMEM((sb, l, wt), BF16), pltpu.VMEM((sb, l, wt), BF16),
                   pltpu.VMEM((sb, h, wt), BF16)]
    else:
        mat_specs = [_resident((l, l), lambda j, b: (0, 0))] * 2
        mat_args = list(mats)
        body = _hy_direct_kernel
        scratch = [pltpu.VMEM((sb, l, wt), F32), pltpu.VMEM((sb, l, wt), BF16),
                   pltpu.VMEM((sb, l, wt), BF16), pltpu.VMEM((sb, l, wt), BF16)]
    return pl.pallas_call(
        functools.partial(body, l=l, rc=rc, sb=sb),
        grid=(nw, nb // sb),
        in_specs=[
            *mat_specs,
            seq(hv0), seq(hx10), seq(hx20),
            pl.BlockSpec((3, 3, wt), lambda j, b: (0, 0, j)),
            pl.BlockSpec((3, wt), lambda j, b: (0, j)),
            tab(0), tab(0), tab(1), tab(1),
            pl.BlockSpec((HY_ORDER, wt), lambda j, b: (0, j)),
            pl.BlockSpec((HY_ORDER, wt), lambda j, b: (0, j)),
        ],
        out_specs=pl.BlockSpec((sb * l, wt), lambda j, b: (b, j)),
        out_shape=jax.ShapeDtypeStruct((nb * l, HY_W), BF16),
        scratch_shapes=scratch,
        compiler_params=_params(("parallel", "parallel")),
        name=f"hyena_{l}",
    )(*mat_args, proj, proj, proj, conv_w.reshape(3, 3, HY_W),
      conv_b.reshape(3, HY_W), kc, ki, kc, ki, kn.reshape(HY_ORDER, HY_W), skip)


def _log_sigmoid(x):
    return jnp.minimum(x, 0.0) - jnp.log1p(jnp.exp(-jnp.abs(x)))


def _ret_kernel(*refs, l, hb, use_rope, has_s0, has_acc, st_layers, st_own):
    it = iter(refs)
    dl_ref, q_ref, k_ref, v_ref, g_ref = (next(it) for _ in range(5))
    cos_ref, sin_ref = (next(it), next(it)) if use_rope else (None, None)
    s0_ref = next(it) if has_s0 else None
    if has_acc:
        next(it)
    o_ref, st_ref, kr_scr, kv_scr = (next(it) for _ in range(4))
    for other in range(st_layers):
        if other != st_own:
            st_ref[0, other] = jnp.zeros(st_ref.shape[2:], F32)

    c, dk, dv = RET_CHUNK, RET_DK, RET_DV
    nc = l // c
    unroll = min(nc, 4)
    pos = lax.broadcasted_iota(jnp.int32, (c, 1), 0).astype(F32)
    diff = pos - lax.broadcasted_iota(jnp.int32, (1, c), 1).astype(F32)

    def rope(x, r0):
        if not use_rope:
            return x
        return x * cos_ref[pl.ds(r0, c), :] + pltpu.roll(x, dk // 2, 1) * sin_ref[pl.ds(r0, c), :]

    for h in range(hb):
        qs = slice(h * dk, (h + 1) * dk)
        vs = slice(h * dv, (h + 1) * dv)
        lgf = _log_sigmoid(dl_ref[0, h])
        lgb = _log_sigmoid(dl_ref[1, h])
        dm = jnp.where(diff >= 0, jnp.exp(lgf * diff), jnp.exp(-lgb * diff))
        zeta_f = jnp.exp(lgf * (c - 1.0 - pos))
        zeta_b = jnp.exp(lgb * pos)
        xi_f = jnp.exp(lgf * (pos + 1.0))
        xi_b = jnp.exp(lgb * (c - pos))
        gc_f = jnp.exp(lgf * c)
        gc_b = jnp.exp(lgb * c)

        def kv_step(ci, carry):
            r0 = pl.multiple_of(ci * c, c)
            k = rope(k_ref[pl.ds(r0, c), qs].astype(F32), r0) * (dk ** -0.5)
            kr_scr[pl.ds(r0, c), qs] = k.astype(BF16)
            kk = jnp.concatenate([k * zeta_f, k * zeta_b], axis=1).astype(BF16)
            kv_scr[h, ci] = lax.dot_general(kk, v_ref[pl.ds(r0, c), vs], (((0,), (0,)), ((), ())),
                                            preferred_element_type=F32)
            return carry

        lax.fori_loop(0, nc, kv_step, 0, unroll=unroll)

        def fwd_scan(ci, s):
            upd = kv_scr[h, ci, 0:dk, :]
            kv_scr[h, ci, 0:dk, :] = s
            return gc_f * s + upd

        def bwd_scan(i, r):
            ci = nc - 1 - i
            upd = kv_scr[h, ci, dk:2 * dk, :]
            kv_scr[h, ci, dk:2 * dk, :] = r
            return gc_b * r + upd

        zero = jnp.zeros((dk, dv), F32)
        st_ref[0, st_own, 0, h] = lax.fori_loop(0, nc, fwd_scan, s0_ref[0, 0, 0, h] if has_s0 else zero)
        st_ref[0, st_own, 1, h] = lax.fori_loop(0, nc, bwd_scan, s0_ref[0, 0, 1, h] if has_s0 else zero)

        def out_step(ci, carry):
            r0 = pl.multiple_of(ci * c, c)
            q = rope(q_ref[pl.ds(r0, c), qs].astype(F32), r0)
            sc = lax.dot_general(q.astype(BF16), kr_scr[pl.ds(r0, c), qs], (((1,), (1,)), ((), ())),
                                 preferred_element_type=F32) * dm
            o = _dot(sc.astype(BF16), v_ref[pl.ds(r0, c), vs])
            if has_s0 or nc > 1:
                qq = jnp.concatenate([q * xi_f, q * xi_b], axis=1).astype(BF16)
                o = o + _dot(qq, kv_scr[h, ci].astype(BF16))
            o = o * lax.rsqrt(jnp.mean(o * o, axis=-1, keepdims=True) + EPS)
            o_ref[pl.ds(r0, c), vs] = (_silu(g_ref[pl.ds(r0, c), vs].astype(F32)) * o).astype(BF16)
            return carry

        lax.fori_loop(0, nc, out_step, 0, unroll=unroll)


def _retention_branch(proj, row_blk0, nb, l, decay_logit, rope_tabs, state, layer, n_state_layers=1,
                      states_acc=None):
    nh, dk, dv = N_RET_HEADS, RET_DK, RET_DV
    use_rope = rope_tabs is not None
    has_s0 = state is not None
    has_acc = states_acc is not None
    own_layer = layer if n_state_layers > 1 else 0
    st_layers = 1 if has_acc else n_state_layers
    hb = nh if l <= 512 else 1
    ng = nh // hb
    seq = lambda col, w: pl.BlockSpec((l, hb * w), lambda b, g: (row_blk0 + b, col // (hb * w) + g))
    in_specs = [
        pl.BlockSpec((2, hb, 1, 1), lambda b, g: (0, g, 0, 0)),
        seq(COL_Q, dk), seq(COL_K, dk), seq(COL_V, dv), seq(COL_G, dv),
    ]
    args = [decay_logit.reshape(2, nh, 1, 1), proj, proj, proj, proj]
    if use_rope:
        in_specs += [pl.BlockSpec((l, dk), lambda b, g: (0, 0))] * 2
        args += list(rope_tabs)
    if has_s0:
        in_specs.append(pl.BlockSpec((1, 1, 2, hb, dk, dv), lambda b, g: (b, layer, 0, g, 0, 0)))
        args.append(state)
    if has_acc:
        in_specs.append(pl.BlockSpec(memory_space=pl.ANY))
        args.append(states_acc)
    return pl.pallas_call(
        functools.partial(_ret_kernel, l=l, hb=hb, use_rope=use_rope, has_s0=has_s0, has_acc=has_acc,
                          st_layers=st_layers, st_own=0 if has_acc else own_layer),
        grid=(nb, ng),
        in_specs=in_specs,
        out_specs=[
            pl.BlockSpec((l, hb * dv), lambda b, g: (b, g)),
            pl.BlockSpec((1, st_layers, 2, hb, dk, dv),
                         lambda b, g: (b, own_layer if has_acc else 0, 0, g, 0, 0)),
        ],
        out_shape=[
            jax.ShapeDtypeStruct((nb * l, nh * dv), BF16),
            jax.ShapeDtypeStruct((nb, n_state_layers, 2, nh, dk, dv), F32),
        ],
        input_output_aliases={len(args) - 1: 1} if has_acc else {},
        scratch_shapes=[
            pltpu.VMEM((l, hb * dk), BF16),
            pltpu.VMEM((hb, l // RET_CHUNK, 2 * dk, dv), F32),
        ],
        compiler_params=_params(("parallel", "parallel")),
        name=f"retention_{l}",
    )(*args)


ROW_ALIGN = 16


def _sorted_rows(tm):
    worst = TOP_K * tm + N_EXPERTS * (ROW_ALIGN - 1) + ROW_ALIGN
    return -(-worst // LANES) * LANES


def _mix_kernel(zap_ref, zas_ref, zbp_ref, zbs_ref, gap_ref, gas_ref, gbp_ref, gbs_ref, xp_ref, xs_ref,
                mod_ref, wa_ref, wb_ref, wo_ref, nw_ref, rw_ref, rb_ref, x1_ref, xl_ref, route_ref, seg_ref,
                h_scr, pr_scr, sg_scr, *, d, tm, rmax, n_tiles, n_prompt_tiles):
    i = pl.program_id(0)

    @pl.when(i == 0)
    def _():
        h_scr[...] = jnp.zeros_like(h_scr)
        pr_scr[...] = jnp.zeros_like(pr_scr)
        sg_scr[...] = jnp.zeros_like(sg_scr)

    @pl.when(i > n_tiles)
    def _():
        xl_ref[...] = jnp.zeros_like(xl_ref)

    @pl.when(i <= n_tiles)
    def _():
        _mix_step(zap_ref, zas_ref, zbp_ref, zbs_ref, gap_ref, gas_ref, gbp_ref, gbs_ref, xp_ref, xs_ref,
                  mod_ref, wa_ref, wb_ref, wo_ref, nw_ref, rw_ref, rb_ref, x1_ref, xl_ref, route_ref, seg_ref,
                  h_scr, pr_scr, sg_scr, d=d, tm=tm, rmax=rmax, n_tiles=n_tiles, n_prompt_tiles=n_prompt_tiles)


def _mix_step(zap_ref, zas_ref, zbp_ref, zbs_ref, gap_ref, gas_ref, gbp_ref, gbs_ref, xp_ref, xs_ref,
              mod_ref, wa_ref, wb_ref, wo_ref, nw_ref, rw_ref, rb_ref, x1_ref, xl_ref, route_ref, seg_ref,
              h_scr, pr_scr, sg_scr, *, d, tm, rmax, n_tiles, n_prompt_tiles):
    i = pl.program_id(0)
    r_col = lax.broadcasted_iota(jnp.int32, (rmax, 1), 0).astype(F32)
    in_seg = jnp.logical_and(r_col >= sg_scr[0:1], r_col < sg_scr[0:1] + sg_scr[1:2])
    seg_hot = jnp.where(in_seg, 1.0, 0.0).astype(BF16)
    row_of = _dot(jnp.concatenate([seg_hot, seg_hot], axis=1), pr_scr[...])
    sel = jnp.where(row_of == r_col + 1.0, 1.0, 0.0).astype(BF16)
    xl_ref[...] = _dot(sel, h_scr[...]).astype(BF16)

    is_prompt = jnp.minimum(i, n_tiles - 1) < n_prompt_tiles
    pick = lambda p_ref, s_ref: jnp.where(is_prompt, p_ref[...], s_ref[...])
    za, zb = pick(zap_ref, zas_ref), pick(zbp_ref, zbs_ref)
    ga, gb = pick(gap_ref, gas_ref), pick(gbp_ref, gbs_ref)
    merged = (_sigmoid(ga.astype(F32)) * _dot(za, wa_ref[...])
              + _sigmoid(gb.astype(F32)) * _dot(zb, wb_ref[...]))
    gate1 = mod_ref[0, :, 2 * d:3 * d]
    shift2 = mod_ref[0, :, 3 * d:4 * d]
    scale2 = mod_ref[0, :, 4 * d:5 * d]
    x1 = pick(xp_ref, xs_ref) + gate1 * _dot(merged.astype(BF16), wo_ref[...])
    x1_ref[...] = x1
    hn = x1 * lax.rsqrt(jnp.mean(x1 * x1, axis=-1, keepdims=True) + EPS) * nw_ref[...]
    hmo = hn * (1.0 + scale2) + shift2

    h_hi, h_lo = _split_bf16(hmo)
    both = _dot(h_hi, rw_ref[...])
    logits = (both[:, :LANES] + both[:, LANES:]) + _dot(h_lo, rw_ref[:, :LANES]) + rb_ref[...]
    lane = lax.broadcasted_iota(jnp.int32, (1, LANES), 1)
    idxs, vals = [], []
    for _ in range(TOP_K):
        m = jnp.max(logits, axis=1, keepdims=True)
        ik = jnp.min(jnp.where(logits == m, lane, LANES), axis=1, keepdims=True)
        idxs.append(ik)
        vals.append(m)
        logits = jnp.where(lane == ik, -jnp.inf, logits)
    exps = [jnp.exp(v - vals[0]) for v in vals]
    denom = exps[0] + exps[1] + exps[2] + exps[3]

    hot = jnp.zeros((tm, LANES), F32)
    for ik in idxs:
        hot = hot + jnp.where(lane == ik, 1.0, 0.0)
    r_i = lax.broadcasted_iota(jnp.int32, (tm, tm), 0)
    c_i = lax.broadcasted_iota(jnp.int32, (tm, tm), 1)
    below = jnp.where(c_i < r_i, 1.0, 0.0).astype(BF16)
    before = _dot(below, hot.astype(BF16))
    seg = jnp.ceil(jnp.sum(hot, axis=0, keepdims=True) * (1.0 / ROW_ALIGN)) * ROW_ALIGN
    seg_ref[0] = seg
    e_r = lax.broadcasted_iota(jnp.int32, (LANES, LANES), 0)
    e_c = lax.broadcasted_iota(jnp.int32, (LANES, LANES), 1)
    seg_start = _dot3(jnp.broadcast_to(seg, (8, LANES)), jnp.where(e_r < e_c, 1.0, 0.0))[0:1]
    base = seg_start + before

    route = jnp.zeros((tm, LANES), F32)
    for k in range(TOP_K):
        srow = jnp.sum(jnp.where(lane == idxs[k], base, 0.0), axis=1, keepdims=True)
        route = route + jnp.where(lane == k, srow, 0.0)
        route = route + jnp.where(lane == TOP_K + k, idxs[k].astype(F32), 0.0)
        route = route + jnp.where(lane == 2 * TOP_K + k, exps[k] / denom, 0.0)
    route_ref[...] = route

    h_scr[...] = h_hi
    pair_row = jnp.where(hot > 0.0, base + 1.0, 0.0).T
    pr_scr[...] = jnp.concatenate(_split_bf16(pair_row), axis=0)
    sg_scr[0:1] = seg_start
    sg_scr[1:2] = seg


def _mix(za, zb, proj, x, mod3, mod_row, wa, wb, wo, norm_w, rw, rb, tm, spare_rows):
    d = x[0].shape[1]
    n_p = x[0].shape[0] // tm
    n_s = x[1].shape[0] // tm
    nt = n_p + n_s
    t = nt * tm
    rmax = _sorted_rows(tm)
    spare_blocks = -(-spare_rows // rmax)
    vw = N_RET_HEADS * RET_DV
    cur = lambda i: jnp.minimum(i, nt - 1)
    prev = lambda i: jnp.maximum(i - 1, 0)
    pair = lambda w, col: [pl.BlockSpec((tm, w), lambda i: (jnp.minimum(cur(i), n_p - 1), col)),
                           pl.BlockSpec((tm, w), lambda i: (jnp.maximum(cur(i) - n_p, 0), col))]
    const = lambda i: (0, 0)
    return pl.pallas_call(
        functools.partial(_mix_kernel, d=d, tm=tm, rmax=rmax, n_tiles=nt, n_prompt_tiles=n_p),
        grid=(nt + 1 + spare_blocks,),
        in_specs=[
            *pair(HY_W, 0), *pair(vw, 0), *pair(d, COL_GA // d), *pair(d, COL_GB // d), *pair(d, 0),
            pl.BlockSpec((1, 1, 6 * d), lambda i: (mod_row(cur(i)), 0, 0)),
            _resident((HY_W, d), const), _resident((vw, d), const), _resident((d, d), const),
            pl.BlockSpec((1, d), const),
            _resident((d, 2 * LANES), const), pl.BlockSpec((1, LANES), const),
        ],
        out_specs=[
            pl.BlockSpec((tm, d), lambda i: (cur(i), 0)),
            pl.BlockSpec((rmax, d), lambda i: (prev(i), 0)),
            pl.BlockSpec((tm, LANES), lambda i: (cur(i), 0)),
            pl.BlockSpec((1, 1, LANES), lambda i: (cur(i), 0, 0)),
        ],
        out_shape=[
            jax.ShapeDtypeStruct((t, d), F32),
            jax.ShapeDtypeStruct(((nt + spare_blocks) * rmax, d), BF16),
            jax.ShapeDtypeStruct((t, LANES), F32),
            jax.ShapeDtypeStruct((nt, 1, LANES), F32),
        ],
        scratch_shapes=[pltpu.VMEM((tm, d), BF16), pltpu.VMEM((2 * LANES, tm), BF16),
                        pltpu.VMEM((2, LANES), F32)],
        compiler_params=_params(("arbitrary",)),
        name="merge_router",
    )(*za, *zb, *proj, *proj, *x, mod3, wa, wb, wo, norm_w.reshape(1, d), rw, rb)


def _ffn_kernel(te_ref, nu_ref, src_ref, dst_ref, xl_ref, wgu_ref, bgu_ref, wd_ref, bd_ref, yl_ref,
                xbuf, ybuf, gsem, ssem, wgu_scr, wd_scr, *, dff, te):
    j = pl.program_id(0)
    nu = nu_ref[0]
    cpt = te // ROW_ALIGN
    slot = j % 2

    def hbm_chunk(ci):
        return pl.ds(pl.multiple_of(ci * ROW_ALIGN, ROW_ALIGN), ROW_ALIGN)

    def start_gather(tile, buf):
        for ci in range(cpt):
            pltpu.make_async_copy(xl_ref.at[hbm_chunk(src_ref[tile * cpt + ci])],
                                  xbuf.at[buf, pl.ds(ci * ROW_ALIGN, ROW_ALIGN)], gsem.at[buf]).start()

    def wait_gather(buf):
        pltpu.make_async_copy(xl_ref.at[pl.ds(0, te)], xbuf.at[buf], gsem.at[buf]).wait()

    def start_scatter(tile, buf):
        for ci in range(cpt):
            pltpu.make_async_copy(ybuf.at[buf, pl.ds(ci * ROW_ALIGN, ROW_ALIGN)],
                                  yl_ref.at[hbm_chunk(dst_ref[tile * cpt + ci])], ssem.at[buf]).start()

    def wait_scatter(buf):
        pltpu.make_async_copy(ybuf.at[buf], yl_ref.at[pl.ds(0, te)], ssem.at[buf]).wait()

    @pl.when(j < nu)
    def _():
        @pl.when(j == 0)
        def _():
            start_gather(0, 0)

        start_gather(j + 1, 1 - slot)
        wait_gather(slot)

        @pl.when(j >= 2)
        def _():
            wait_scatter(slot)

        @pl.when(jnp.logical_or(j == 0, te_ref[j] != te_ref[jnp.maximum(j - 1, 0)]))
        def _():
            wgu_scr[...] = wgu_ref[0].astype(BF16)
            wd_scr[...] = wd_ref[0].astype(BF16)

        hgu = _dot(xbuf[slot], wgu_scr[...]) + bgu_ref[0]
        gate = jnp.minimum(hgu[:, :dff], SWIGLU_LIMIT)
        up = jnp.clip(hgu[:, dff:], -SWIGLU_LIMIT, SWIGLU_LIMIT)
        act = gate * _sigmoid(SWIGLU_ALPHA * gate) * (up + 1.0)
        ybuf[slot] = (_dot(act.astype(BF16), wd_scr[...]) + bd_ref[0]).astype(BF16)
        start_scatter(j, slot)

        @pl.when(j == nu - 1)
        def _():
            wait_gather(1 - slot)

            @pl.when(j >= 1)
            def _():
                wait_scatter(1 - slot)
            wait_scatter(slot)


def _expert_ffn(xl, tile_expert, n_used, src_chunk, dst_chunk, w_gate_up, b_gate_up, w_down, b_down, layer, te):
    _, d = xl.shape
    depth, n_exp, _, two_ff = w_gate_up.shape
    ne = depth * n_exp
    dff = two_ff // 2
    wsel = lambda j, tex, nu, src, dst: (layer * n_exp + tex[j], 0, 0)
    return pl.pallas_call(
        functools.partial(_ffn_kernel, dff=dff, te=te),
        grid_spec=pltpu.PrefetchScalarGridSpec(
            num_scalar_prefetch=4,
            grid=(tile_expert.shape[0],),
            in_specs=[
                pl.BlockSpec(memory_space=pl.ANY),
                pl.BlockSpec((1, d, two_ff), wsel),
                pl.BlockSpec((1, 1, two_ff), wsel),
                pl.BlockSpec((1, dff, d), wsel),
                pl.BlockSpec((1, 1, d), wsel),
            ],
            out_specs=pl.BlockSpec(memory_space=pl.ANY),
            scratch_shapes=[
                pltpu.VMEM((2, te, d), BF16), pltpu.VMEM((2, te, d), BF16),
                pltpu.SemaphoreType.DMA((2,)), pltpu.SemaphoreType.DMA((2,)),
                pltpu.VMEM((d, two_ff), BF16), pltpu.VMEM((dff, d), BF16),
            ],
        ),
        out_shape=jax.ShapeDtypeStruct(xl.shape, BF16),
        input_output_aliases={4: 0},
        compiler_params=_params(("arbitrary",)),
        name="moe_ffn",
    )(tile_expert, n_used, src_chunk, dst_chunk, xl, w_gate_up.reshape(ne, d, two_ff),
      b_gate_up.reshape(ne, 1, two_ff), w_down.reshape(ne, dff, d), b_down.reshape(ne, 1, d))


def _comb_kernel(yl_ref, route_ref, x1_ref, mod_ref, fw_ref, o_ref, *, d, tm, rmax, final_norm):
    route = route_ref[...]
    r_iota = lax.broadcasted_iota(jnp.int32, (1, rmax), 1).astype(F32)
    wm = jnp.zeros((tm, rmax), F32)
    for k in range(TOP_K):
        wm = jnp.where(r_iota == route[:, k:k + 1], route[:, 2 * TOP_K + k:2 * TOP_K + k + 1], wm)
    out = _dot(wm.astype(BF16), yl_ref[...])
    x2 = x1_ref[...] + mod_ref[0, :, 5 * d:6 * d] * out
    if final_norm:
        x2 = x2 * lax.rsqrt(jnp.mean(x2 * x2, axis=-1, keepdims=True) + EPS) * fw_ref[...]
    o_ref[...] = x2


def _combine(yl, route, x1, mod3, mod_row, final_w, tile0, n_tiles, tm, final_norm):
    d = x1.shape[1]
    rmax = _sorted_rows(tm)
    return pl.pallas_call(
        functools.partial(_comb_kernel, d=d, tm=tm, rmax=rmax, final_norm=final_norm),
        grid=(n_tiles,),
        in_specs=[
            pl.BlockSpec((rmax, d), lambda i: (tile0 + i, 0)),
            pl.BlockSpec((tm, LANES), lambda i: (tile0 + i, 0)),
            pl.BlockSpec((tm, d), lambda i: (tile0 + i, 0)),
            pl.BlockSpec((1, 1, 6 * d), lambda i: (mod_row(tile0 + i), 0, 0)),
            pl.BlockSpec((1, d), lambda i: (0, 0)),
        ],
        out_specs=pl.BlockSpec((tm, d), lambda i: (i, 0)),
        out_shape=jax.ShapeDtypeStruct((n_tiles * tm, d), F32),
        compiler_params=_params(("parallel",)),
        name="moe_combine",
    )(yl, route, x1, mod3, final_w.reshape(1, d))


@functools.lru_cache(maxsize=None)
def _dft_mats(l):
    s = np.arange(l, dtype=np.int64)
    f = np.concatenate([np.arange(0, l, 2), np.arange(1, l, 2)]).astype(np.int64) if _fold_dft(l) else s
    ang = ((f[:, None] * s[None, :]) % (2 * l)).astype(np.float64) * (math.pi / l)
    mats = (np.cos(ang).astype(np.float32), np.sin(ang).astype(np.float32))
    if _fold_dft(l):
        a = np.arange(l // 2)
        mats += (((a[:, None] + a[None, :] == l // 2) & (a[:, None] >= 1)).astype(np.float32),)
    return mats


@functools.lru_cache(maxsize=None)
def _filter_features(l):
    f32 = np.float32
    t = np.linspace(0.0, 1.0, l, dtype=f32)[:, None]
    bands = (HY_EMB - 1) // 2
    w = f32(2.0 * math.pi) * np.arange(l, dtype=f32)[:, None] / f32(l)
    f = np.linspace(1e-4, bands - 1, bands, dtype=f32)[None, :]
    z = np.concatenate([t, np.cos(f * w), -np.sin(f * w)], axis=-1).astype(f32)
    return np.pad(z, ((0, 0), (0, LANES - HY_EMB))), t


@functools.lru_cache(maxsize=None)
def _rope_tables(rows):
    f32 = np.float32
    row = np.repeat(np.arange(rows, dtype=f32), GRID_W)
    col = np.tile(np.arange(GRID_W, dtype=f32), rows)
    half = RET_DK // 4
    inv = (f32(ROPE_BASE) ** (-np.arange(half, dtype=f32) / f32(half))).astype(f32)
    ang = np.concatenate([row[:, None] * inv, col[:, None] * inv], axis=-1)
    ang = np.concatenate([ang, ang], axis=-1).astype(f32)
    sign = np.where(np.arange(RET_DK) < RET_DK // 2, -1.0, 1.0).astype(f32)
    return np.cos(ang), np.sin(ang) * sign


@functools.lru_cache(maxsize=None)
def _decay_rates():
    return np.linspace(math.log(HY_TARGET) / HY_SLOW_PCT, math.log(HY_TARGET) / HY_FAST_PCT,
                       HY_W, dtype=np.float32)[None, :]


def _pad2(a, rows, cols):
    return jnp.pad(a, ((0, rows - a.shape[0]), (0, cols - a.shape[1])))


def _ffn_plan(seg, rmax, te, n_ffn_tiles):
    nt, ne = seg.shape
    cum_t = jnp.cumsum(seg, axis=0)
    tot = cum_t[-1]
    padded = ((tot + te - 1) // te) * te
    ends = jnp.cumsum(padded)
    offs = ends - padded
    n_used = ends[-1] // te
    expert_of = lambda rows: jnp.minimum(jnp.sum(ends[None, :] <= rows[:, None], axis=1), ne - 1)
    tiles = jnp.arange(n_ffn_tiles, dtype=jnp.int32)
    tile_expert = expert_of(jnp.minimum(tiles, n_used - 1) * te).astype(jnp.int32)

    rows = jnp.arange(n_ffn_tiles * te // ROW_ALIGN, dtype=jnp.int32) * ROW_ALIGN
    hot_e = expert_of(rows)[:, None] == jnp.arange(ne)[None, :]
    pick_e = lambda tab: jnp.sum(jnp.where(hot_e[:, None, :], tab[None], 0), axis=2)
    r_s = rows - pick_e(offs[None, :])[:, 0]
    cum_sel = pick_e(cum_t)
    i_s = jnp.minimum(jnp.sum(cum_sel <= r_s[:, None], axis=1), nt - 1)
    hot_i = i_s[:, None] == jnp.arange(nt)[None, :]
    seg_base = (jnp.arange(nt)[:, None] * rmax + jnp.cumsum(seg, axis=1) - seg) - (cum_t - seg)
    src_row = r_s + jnp.sum(jnp.where(hot_i, pick_e(seg_base), 0), axis=1)
    valid = (r_s < pick_e(tot[None, :])[:, 0]) & (rows < ends[-1])
    src_chunk = jnp.where(valid, src_row // ROW_ALIGN, rmax // ROW_ALIGN - 1).astype(jnp.int32)
    pad_rank = jnp.cumsum(jnp.where(valid, 0, 1)) - 1
    spare = nt * rmax // ROW_ALIGN + pad_rank % _spare_chunks(te)
    dst_chunk = jnp.where(valid, src_row // ROW_ALIGN, spare).astype(jnp.int32)
    return tile_expert, n_used.reshape(1).astype(jnp.int32), src_chunk, dst_chunk


def _spare_chunks(te):
    return 4 * (te // ROW_ALIGN)


def kernel(x_prompt, x_sample, state_ret, c, c_ctx, ada_w, ada_b, norm_mix_w, w_in, conv_w, conv_b,
           f_w1, f_b1, f_w2, f_b2, f_w3, f_b3, f_freq, f_wout, hy_skip, ret_decay_logit,
           w_branch_a, w_branch_b, w_out, norm_ffn_w, router_w, router_b, w_gate_up, b_gate_up,
           w_down, b_down, final_norm_w):
    bp, lp, d = x_prompt.shape
    bs, ls, _ = x_sample.shape
    depth = ada_w.shape[0]
    tp, ts = bp * lp, bs * ls
    t = tp + ts
    assert ls % lp == 0 and tp % ls == 0 and lp % RET_CHUNK == 0 and ls % GRID_W == 0
    tm = 512 if (tp % 512 == 0 and ls % 512 == 0) else lp
    tm_in = tm
    te = 512
    nt = t // tm
    rmax = _sorted_rows(tm)
    n_ffn_tiles = -(-(t * TOP_K + nt * N_EXPERTS * (ROW_ALIGN - 1)) // te) + N_EXPERTS

    dft = {l: tuple(jnp.asarray(a).astype(BF16) for a in _dft_mats(l)) for l in (lp, ls)}
    feats = {l: tuple(jnp.asarray(a) for a in _filter_features(l)) for l in (lp, ls)}
    rope_tabs = tuple(jnp.asarray(a) for a in _rope_tables(ls // GRID_W))
    deltas = jnp.asarray(_decay_rates())

    cvec = jnp.zeros((8, d), F32).at[:bs].set(c).at[bs].set(c_ctx)
    mod3 = _modulation(cvec, ada_w, ada_b).reshape(depth * 8, 1, 6 * d)
    n_p, n_s = tp // tm, ts // tm

    x = (x_prompt.reshape(tp, d), x_sample.reshape(ts, d))
    states = None
    for l in range(depth):
        mod_row = lambda i, l=l: l * 8 + jnp.where(i < n_p, bs, jnp.maximum(i - n_p, 0) // (ls // tm))
        w_in_bf = w_in[l].astype(BF16)
        proj = (_in_proj(x[0], norm_mix_w[l], mod3, lambda i, l=l: l * 8 + bs, w_in_bf, tm_in),
                _in_proj(x[1], norm_mix_w[l], mod3, lambda i, l=l: l * 8 + i // (ls // tm_in), w_in_bf, tm_in))

        w1 = _pad2(f_w1[l], LANES, LANES)
        w2 = _pad2(f_w2[l], LANES, LANES)
        w3 = _pad2(f_w3[l], LANES, LANES)
        wo = _pad2(f_wout[l], LANES, f_wout.shape[2])
        b1, b2, b3, fr = (_pad2(a[l][None, :], 1, LANES) for a in (f_b1, f_b2, f_b3, f_freq))
        za, zb = [], []
        for grp, (nb, ln) in enumerate(((bp, lp), (bs, ls))):
            taps, sums = _hyena_taps(ln, feats[ln][0], feats[ln][1], deltas, w1, b1, w2, b2, w3, b3, fr, wo)
            kc, ki, kn = _filter_spectra(ln, dft[ln][0], dft[ln][1], taps, sums)
            za.append(_hyena_branch(proj[grp], 0, nb, ln, dft[ln], conv_w[l], conv_b[l], kc, ki, kn, hy_skip[l]))
            if grp == 0:
                zb_g, states = _retention_branch(proj[grp], 0, nb, ln, ret_decay_logit[l], None, None, l,
                                                 depth, states)
            else:
                zb_g, _ = _retention_branch(proj[grp], 0, nb, ln, ret_decay_logit[l], rope_tabs, state_ret, l)
            zb.append(zb_g)

        rw = jnp.concatenate(_split_bf16(_pad2(router_w[l], d, LANES)), axis=1)
        rb = jnp.full((1, LANES), -jnp.inf, F32).at[0, :N_EXPERTS].set(router_b[l])
        x1, xl, route, seg = _mix(za, zb, proj, x, mod3, mod_row,
                                  w_branch_a[l].astype(BF16), w_branch_b[l].astype(BF16),
                                  w_out[l].astype(BF16), norm_ffn_w[l], rw, rb, tm, _spare_chunks(te) * ROW_ALIGN)
        tile_expert, n_used, src_chunk, dst_chunk = _ffn_plan(seg[:, 0, :N_EXPERTS].astype(jnp.int32), rmax, te,
                                                              n_ffn_tiles)
        yl = _expert_ffn(xl, tile_expert, n_used, src_chunk, dst_chunk, w_gate_up, b_gate_up, w_down, b_down,
                         l, te)
        last = l == depth - 1
        x = (_combine(yl, route, x1, mod3, mod_row, final_norm_w, 0, n_p, tm, last),
             _combine(yl, route, x1, mod3, mod_row, final_norm_w, n_p, n_s, tm, last))

    return (x[0].reshape(bp, lp, d), x[1].reshape(bs, ls, d), states)
```
